```python
import math
import jax, jax.numpy as jnp
from jax import lax
import numpy as np

D_MODEL = 1024
BATCH = 8
SEQ = 2048
DEPTH = 2
DEC_BATCH = 128
DEC_SEQ = 8
PAST_LEN = 2048
PAGE_SIZE = 128

N_A_LAYERS = DEPTH // 2
N_B_LAYERS = DEPTH - N_A_LAYERS
DN_ALPHA = (2.0 * DEPTH) ** 0.25
DN_BETA = (8.0 * DEPTH) ** -0.25
LN_EPS = 1e-5
SSM_EXPAND = 2
SSM_D_INNER = SSM_EXPAND * D_MODEL
SSM_HEAD_DIM = 64
SSM_HEADS = SSM_D_INNER // SSM_HEAD_DIM
SSM_GROUPS = 8
SSM_D_STATE = 128
SSM_CONV = 4
SSM_CHUNK = 128
SSM_CONV_DIM = SSM_D_INNER + 2 * SSM_GROUPS * SSM_D_STATE
SSM_IN_DIM = SSM_D_INNER + SSM_CONV_DIM + SSM_HEADS
ATT_HEADS = D_MODEL // 128
ATT_HEAD_DIM = D_MODEL // (2 * ATT_HEADS)
ATT_QK_DIM = 2 * ATT_HEADS * ATT_HEAD_DIM
ATT_V_DIM = ATT_HEADS * 2 * ATT_HEAD_DIM
Q_BLOCK = 128
NUM_BUCKETS = 32
MAX_DISTANCE = 128
MOE_GROUPS = 4
MOE_EXPERTS_PER_GROUP = 8
MOE_TOP_K = 2
MOE_FF = D_MODEL // 2

kernel_name = 'yoco_mamba2_diffattn_hmoe_step'


def _layer_norm(x, g, b):
    xf = x.astype(jnp.float32)
    mu = jnp.mean(xf, axis=-1, keepdims=True)
    var = jnp.mean(jnp.square(xf - mu), axis=-1, keepdims=True)
    return (xf - mu) * lax.rsqrt(var + LN_EPS) * g + b


def _rms_norm(x, w):
    xf = x.astype(jnp.float32)
    return xf * lax.rsqrt(jnp.mean(xf * xf, axis=-1, keepdims=True) + LN_EPS) * w


def _rel_bucket(dist):
    n = jnp.maximum(dist, 0)
    max_exact = NUM_BUCKETS // 2
    nf = jnp.maximum(n, 1).astype(jnp.float32)
    large = max_exact + (jnp.log(nf / max_exact) / math.log(MAX_DISTANCE / max_exact)
                         * (NUM_BUCKETS - max_exact)).astype(jnp.int32)
    large = jnp.minimum(large, NUM_BUCKETS - 1)
    return jnp.where(n < max_exact, n, large)


def _ssd_scan(x, dt, a, bm, cm, h0):
    b, L, H, P = x.shape
    G, N = bm.shape[2], bm.shape[3]
    hg = H // G
    q = math.gcd(L, SSM_CHUNK)
    nc = L // q
    f32 = jnp.float32
    x = x.astype(f32).reshape(b, nc, q, G, hg, P)
    dt = dt.reshape(b, nc, q, G, hg)
    bm = bm.astype(f32).reshape(b, nc, q, G, N)
    cm = cm.astype(f32).reshape(b, nc, q, G, N)
    acum = jnp.cumsum(dt * a.reshape(G, hg), axis=2)
    seg = acum[:, :, :, None] - acum[:, :, None, :]
    mask = jnp.tril(jnp.ones((q, q), dtype=bool))[:, :, None, None]
    lmat = jnp.exp(jnp.where(mask, seg, -jnp.inf))
    cb = jnp.einsum('bctgn,bcsgn->bctsg', cm, bm)
    w = cb[..., None] * lmat * dt[:, :, None]
    y_diag = jnp.einsum('bctsgj,bcsgjp->bctgjp', w, x)
    decay_end = jnp.exp(acum[:, :, -1:] - acum)
    states = jnp.einsum('bcsgn,bcsgj,bcsgjp->bcgjpn', bm, decay_end * dt, x)
    chunk_decay = jnp.exp(acum[:, :, -1])

    def step(h, inp):
        s, dec = inp
        return dec[..., None, None] * h + s, h

    h_final, h_prev = lax.scan(step, h0.astype(f32).reshape(b, G, hg, P, N),
                               (jnp.moveaxis(states, 1, 0), jnp.moveaxis(chunk_decay, 1, 0)))
    h_prev = jnp.moveaxis(h_prev, 0, 1)
    y_off = jnp.einsum('bctgn,bcgjpn,bctgj->bctgjp', cm, h_prev, jnp.exp(acum))
    y = (y_diag + y_off).reshape(b, L, H, P)
    return y, h_final.reshape(b, H, P, N)


def _mamba_mixer(x, ssm0, conv0, w_in, conv_w, conv_b, dt_bias, a_log, d_skip, norm_w, w_out):
    b, L, _ = x.shape
    zxbcdt = jnp.einsum('bld,de->ble', x, w_in)
    z = zxbcdt[..., :SSM_D_INNER]
    xbc = zxbcdt[..., SSM_D_INNER:SSM_D_INNER + SSM_CONV_DIM]
    dt_raw = zxbcdt[..., SSM_D_INNER + SSM_CONV_DIM:]
    xbc_full = jnp.concatenate([conv0.astype(xbc.dtype), xbc], axis=1)
    new_conv = xbc_full[:, xbc_full.shape[1] - (SSM_CONV - 1):]
    conv = conv_b
    for k in range(SSM_CONV):
        conv = conv + conv_w[k] * xbc_full[:, k:k + L]
    xbc_c = jax.nn.silu(conv)
    gn = SSM_GROUPS * SSM_D_STATE
    xs = xbc_c[..., :SSM_D_INNER].reshape(b, L, SSM_HEADS, SSM_HEAD_DIM)
    bm = xbc_c[..., SSM_D_INNER:SSM_D_INNER + gn].reshape(b, L, SSM_GROUPS, SSM_D_STATE)
    cm = xbc_c[..., SSM_D_INNER + gn:].reshape(b, L, SSM_GROUPS, SSM_D_STATE)
    dt = jax.nn.softplus(dt_raw.astype(jnp.float32) + dt_bias)
    a = -jnp.exp(a_log.astype(jnp.float32))
    y, h_final = _ssd_scan(xs, dt, a, bm, cm, ssm0)
    y = y + d_skip[:, None] * xs
    y = y.reshape(b, L, SSM_D_INNER) * jax.nn.silu(z.astype(jnp.float32))
    gs = SSM_D_INNER // SSM_GROUPS
    y = _rms_norm(y.reshape(b, L, SSM_GROUPS, gs), norm_w.reshape(SSM_GROUPS, gs)).reshape(b, L, SSM_D_INNER)
    out = jnp.einsum('ble,ed->bld', y, w_out)
    return out, h_final, new_conv


def _diff_attention_mixer(x, k_all, v_all, pos0, w_q, lam_vec, subln_w, w_o, rel_bias, layer_idx):
    b, L, _ = x.shape
    Lk = k_all.shape[1]
    q = jnp.einsum('bld,de->ble', x, w_q).reshape(b, L, 2 * ATT_HEADS, ATT_HEAD_DIM)
    lam_init = 0.8 - 0.6 * math.exp(-0.3 * layer_idx)
    lv = lam_vec.astype(jnp.float32)
    lam = jnp.exp(jnp.sum(lv[0] * lv[1])) - jnp.exp(jnp.sum(lv[2] * lv[3])) + lam_init
    blk = math.gcd(L, Q_BLOCK)
    nb = L // blk
    qb = jnp.moveaxis(q.reshape(b, nb, blk, 2 * ATT_HEADS, ATT_HEAD_DIM), 1, 0)
    qpos = (pos0 + jnp.arange(L, dtype=jnp.int32)).reshape(nb, blk)
    kpos = jnp.arange(Lk, dtype=jnp.int32)
    v_f = v_all.astype(jnp.float32)
    scale = ATT_HEAD_DIM ** -0.5

    def one_block(args):
        qi, pi = args
        s = jnp.einsum('bqhd,bkhd->bhqk', qi, k_all).astype(jnp.float32) * scale
        dist = pi[:, None] - kpos[None, :]
        bias = jnp.transpose(rel_bias[_rel_bucket(dist)], (2, 0, 1)).astype(jnp.float32)
        s = jnp.where(dist[None, None] >= 0, s + bias[None], -jnp.inf)
        p = jax.nn.softmax(s, axis=-1).reshape(b, ATT_HEADS, 2, blk, Lk)
        a_map = p[:, :, 0] - lam * p[:, :, 1]
        return jnp.einsum('bhqk,bkhe->bqhe', a_map, v_f)

    o = lax.map(one_block, (qb, qpos))
    o = jnp.moveaxis(o, 0, 1).reshape(b, L, ATT_HEADS, 2 * ATT_HEAD_DIM)
    o = _rms_norm(o, subln_w) * (1.0 - lam_init)
    return jnp.einsum('ble,ed->bld', o.reshape(b, L, ATT_V_DIM), w_o)


def _hier_moe(x, wg, bg, we, be, w1, w3, w2):
    b, L, D = x.shape
    t = x.reshape(b * L, D)
    g_logits = jnp.einsum('td,dg->tg', t, wg).astype(jnp.float32) + bg
    grp = jnp.argmax(g_logits, axis=-1)
    p_grp = jnp.max(jax.nn.softmax(g_logits, axis=-1), axis=-1)
    e_logits = jnp.einsum('td,gde->tge', t, we).astype(jnp.float32) + be
    e_sel = jnp.take_along_axis(e_logits, grp[:, None, None], axis=1)[:, 0]
    top_v, top_i = lax.top_k(e_sel, MOE_TOP_K)
    top_w = jax.nn.softmax(top_v, axis=-1) * p_grp[:, None]
    w_grp = jnp.sum(jax.nn.one_hot(top_i, MOE_EXPERTS_PER_GROUP, dtype=jnp.float32) * top_w[..., None], axis=1)
    comb = jax.nn.one_hot(grp, MOE_GROUPS, dtype=jnp.float32)[:, :, None] * w_grp[:, None, :]
    y = jnp.zeros((b * L, D), jnp.float32)
    for g in range(MOE_GROUPS):
        h = jax.nn.silu(jnp.einsum('td,edf->tef', t, w1[g])) * jnp.einsum('td,edf->tef', t, w3[g])
        y = y + jnp.einsum('tef,efd->td', h * comb[:, g, :, None], w2[g])
    return y.reshape(b, L, D)


def _trunk(x, ssm0, conv0, past_k, past_v, pos0,
           ln_g, ln_b, m_w_in, m_conv_w, m_conv_b, m_dt_bias, m_a_log, m_d, m_norm_w, m_w_out,
           kv_w, a_w_q, a_lambda, a_subln_w, a_w_o, rel_bias,
           moe_wg, moe_bg, moe_we, moe_be, moe_w1, moe_w3, moe_w2):
    b, L, _ = x.shape
    ssm_out, conv_out = [], []
    k_all = v_all = k_new = v_new = None
    for l in range(DEPTH):
        if l < N_A_LAYERS:
            h, s_new, c_new = _mamba_mixer(x, ssm0[l], conv0[l], m_w_in[l], m_conv_w[l], m_conv_b[l],
                                           m_dt_bias[l], m_a_log[l], m_d[l], m_norm_w[l], m_w_out[l])
            ssm_out.append(s_new)
            conv_out.append(c_new)
        else:
            j = l - N_A_LAYERS
            h = _diff_attention_mixer(x, k_all, v_all, pos0, a_w_q[j], a_lambda[j], a_subln_w[j],
                                      a_w_o[j], rel_bias, l)
        x = _layer_norm(DN_ALPHA * x + h, ln_g[l, 0], ln_b[l, 0])
        x = _layer_norm(DN_ALPHA * x + _hier_moe(x, moe_wg[l], moe_bg[l], moe_we[l], moe_be[l],
                                                 moe_w1[l], moe_w3[l], moe_w2[l]),
                        ln_g[l, 1], ln_b[l, 1])
        if l == N_A_LAYERS - 1:
            kv = jnp.einsum('bld,de->ble', x, kv_w)
            k_new = kv[..., :ATT_QK_DIM].reshape(b, L, 2 * ATT_HEADS, ATT_HEAD_DIM)
            v_new = kv[..., ATT_QK_DIM:].reshape(b, L, ATT_HEADS, 2 * ATT_HEAD_DIM)
            if past_k is None:
                k_all, v_all = k_new, v_new
            else:
                k_all = jnp.concatenate([past_k, k_new], axis=1)
                v_all = jnp.concatenate([past_v, v_new], axis=1)
    return x, jnp.stack(ssm_out), jnp.stack(conv_out), k_new, v_new


def setup_inputs(seed: int = 0) -> dict:
    key = jax.random.key(seed)
    ks = jax.random.split(key, 32)
    f32 = jnp.float32

    def nrm(k, shape, s):
        return jax.random.normal(k, shape, f32) * s

    n_pages = PAST_LEN // PAGE_SIZE
    n_used = DEC_BATCH * n_pages
    n_pool = n_used + max(n_used // 4, 1)
    page_table = jax.random.permutation(ks[0], n_pool)[:n_used].reshape(DEC_BATCH, n_pages).astype(jnp.int32)
    dt0 = jnp.exp(jax.random.uniform(ks[1], (N_A_LAYERS, SSM_HEADS), f32, math.log(1e-3), math.log(1e-1)))
    moe_shape = (DEPTH, MOE_GROUPS, MOE_EXPERTS_PER_GROUP)
    return {
        'x_prompt': nrm(ks[2], (BATCH, SEQ, D_MODEL), 1.0),
        'x_sample': nrm(ks[3], (DEC_BATCH, DEC_SEQ, D_MODEL), 1.0),
        'state_ssm': nrm(ks[4], (N_A_LAYERS, DEC_BATCH, SSM_HEADS, SSM_HEAD_DIM, SSM_D_STATE), 0.3),
        'state_conv': nrm(ks[5], (N_A_LAYERS, DEC_BATCH, SSM_CONV - 1, SSM_CONV_DIM), 1.0),
        'cache_k': nrm(ks[6], (n_pool, PAGE_SIZE, 2 * ATT_HEADS, ATT_HEAD_DIM), 1.0),
        'cache_v': nrm(ks[7], (n_pool, PAGE_SIZE, ATT_HEADS, 2 * ATT_HEAD_DIM), 1.0),
        'page_table': page_table,
        'ln_g': 1.0 + nrm(ks[8], (DEPTH, 2, D_MODEL), 0.02),
        'ln_b': nrm(ks[9], (DEPTH, 2, D_MODEL), 0.02),
        'm_w_in': nrm(ks[10], (N_A_LAYERS, D_MODEL, SSM_IN_DIM), D_MODEL ** -0.5),
        'm_conv_w': nrm(ks[11], (N_A_LAYERS, SSM_CONV, SSM_CONV_DIM), SSM_CONV ** -0.5),
        'm_conv_b': nrm(ks[12], (N_A_LAYERS, SSM_CONV_DIM), 0.02),
        'm_dt_bias': dt0 + jnp.log(-jnp.expm1(-dt0)),
        'm_a_log': jnp.log(jax.random.uniform(ks[13], (N_A_LAYERS, SSM_HEADS), f32, 1.0, 16.0)),
        'm_d': 1.0 + nrm(ks[14], (N_A_LAYERS, SSM_HEADS), 0.1),
        'm_norm_w': 1.0 + nrm(ks[15], (N_A_LAYERS, SSM_D_INNER), 0.02),
        'm_w_out': nrm(ks[16], (N_A_LAYERS, SSM_D_INNER, D_MODEL), SSM_D_INNER ** -0.5 * DN_BETA),
        'kv_w': nrm(ks[17], (D_MODEL, ATT_QK_DIM + ATT_V_DIM), D_MODEL ** -0.5),
        'a_w_q': nrm(ks[18], (N_B_LAYERS, D_MODEL, ATT_QK_DIM), D_MODEL ** -0.5),
        'a_lambda': nrm(ks[19], (N_B_LAYERS, 4, ATT_HEAD_DIM), 0.1),
        'a_subln_w': 1.0 + nrm(ks[20], (N_B_LAYERS, 2 * ATT_HEAD_DIM), 0.02),
        'a_w_o': nrm(ks[21], (N_B_LAYERS, ATT_V_DIM, D_MODEL), ATT_V_DIM ** -0.5 * DN_BETA),
        'rel_bias': nrm(ks[22], (NUM_BUCKETS, 2 * ATT_HEADS), 0.5),
        'moe_wg': nrm(ks[23], (DEPTH, D_MODEL, MOE_GROUPS), D_MODEL ** -0.5),
        'moe_bg': nrm(ks[24], (DEPTH, MOE_GROUPS), 0.01),
        'moe_we': nrm(ks[25], (DEPTH, MOE_GROUPS, D_MODEL, MOE_EXPERTS_PER_GROUP), D_MODEL ** -0.5),
        'moe_be': nrm(ks[26], (DEPTH, MOE_GROUPS, MOE_EXPERTS_PER_GROUP), 0.01),
        'moe_w1': nrm(ks[27], moe_shape + (D_MODEL, MOE_FF), D_MODEL ** -0.5),
        'moe_w3': nrm(ks[28], moe_shape + (D_MODEL, MOE_FF), D_MODEL ** -0.5),
        'moe_w2': nrm(ks[29], moe_shape + (MOE_FF, D_MODEL), MOE_FF ** -0.5 * DN_BETA),
    }


def reference(x_prompt, x_sample, state_ssm, state_conv, cache_k, cache_v, page_table,
              ln_g, ln_b, m_w_in, m_conv_w, m_conv_b, m_dt_bias, m_a_log, m_d, m_norm_w, m_w_out,
              kv_w, a_w_q, a_lambda, a_subln_w, a_w_o, rel_bias,
              moe_wg, moe_bg, moe_we, moe_be, moe_w1, moe_w3, moe_w2):
    b_p = x_prompt.shape[0]
    db = x_sample.shape[0]
    n_pages = page_table.shape[1]
    past_len = n_pages * cache_k.shape[1]
    past_k = cache_k[page_table].reshape(db, past_len, 2 * ATT_HEADS, ATT_HEAD_DIM)
    past_v = cache_v[page_table].reshape(db, past_len, ATT_HEADS, 2 * ATT_HEAD_DIM)
    ssm0 = jnp.zeros((N_A_LAYERS, b_p, SSM_HEADS, SSM_HEAD_DIM, SSM_D_STATE), jnp.float32)
    conv0 = jnp.zeros((N_A_LAYERS, b_p, SSM_CONV - 1, SSM_CONV_DIM), x_prompt.dtype)
    y_prompt, ssm_p, conv_p, k_p, v_p = _trunk(
        x_prompt, ssm0, conv0, None, None, 0,
        ln_g, ln_b, m_w_in, m_conv_w, m_conv_b, m_dt_bias, m_a_log, m_d, m_norm_w, m_w_out,
        kv_w, a_w_q, a_lambda, a_subln_w, a_w_o, rel_bias,
        moe_wg, moe_bg, moe_we, moe_be, moe_w1, moe_w3, moe_w2)
    y_sample, ssm_s, conv_s, k_s, v_s = _trunk(
        x_sample, state_ssm, state_conv, past_k, past_v, past_len,
        ln_g, ln_b, m_w_in, m_conv_w, m_conv_b, m_dt_bias, m_a_log, m_d, m_norm_w, m_w_out,
        kv_w, a_w_q, a_lambda, a_subln_w, a_w_o, rel_bias,
        moe_wg, moe_bg, moe_we, moe_be, moe_w1, moe_w3, moe_w2)
    return (y_prompt, y_sample, ssm_p, conv_p, k_p, v_p, ssm_s, conv_s, k_s, v_s)
```

```python
import functools
import math

import numpy as np
import jax
import jax.numpy as jnp
from jax import lax
from jax.experimental import pallas as pl
from jax.experimental.pallas import tpu as pltpu

F32 = jnp.float32
BF16 = jnp.bfloat16
HIGHEST = lax.Precision.HIGHEST

D_MODEL = 1024
DEPTH = 2
DN_ALPHA = (2.0 * DEPTH) ** 0.25
LN_EPS = 1e-5
SSM_D_INNER = 2048
SSM_HEAD_DIM = 64
SSM_HEADS = 32
SSM_GROUPS = 8
SSM_D_STATE = 128
SSM_CONV = 4
SSM_CHUNK = 128
SSM_CONV_DIM = SSM_D_INNER + 2 * SSM_GROUPS * SSM_D_STATE
HEADS_PER_GROUP = SSM_HEADS // SSM_GROUPS
GROUP_WIDTH = HEADS_PER_GROUP * SSM_HEAD_DIM
ATT_HEADS = 8
ATT_HEAD_DIM = 64
ATT_QK_DIM = 1024
ATT_V_DIM = 1024
NUM_BUCKETS = 32
MAX_DISTANCE = 128
MOE_GROUPS = 4
MOE_EXPERTS_PER_GROUP = 8
MOE_EXPERTS = MOE_GROUPS * MOE_EXPERTS_PER_GROUP
MOE_FF = 512
PAGE_SIZE = 128

LANES = 128
SUBLANES = 8
VMEM_LIMIT = 56 * 1024 * 1024

TOKEN_TILE = 256
FFN_TILE = 256
ATT_TILE = 256


def _cparams(*sem):
    return pltpu.CompilerParams(dimension_semantics=sem, vmem_limit_bytes=VMEM_LIMIT)


def _sigmoid(x):
    return 1.0 / (1.0 + jnp.exp(-x))


def _silu(x):
    return x * _sigmoid(x)


def _softplus(x):
    return jnp.maximum(x, 0.0) + jnp.log(1.0 + jnp.exp(-jnp.abs(x)))


def _layer_norm(u, g, b):
    mu = jnp.mean(u, axis=-1, keepdims=True)
    d = u - mu
    var = jnp.mean(d * d, axis=-1, keepdims=True)
    return d * lax.rsqrt(var + LN_EPS) * g + b


def _dot_nt(a, b):
    return lax.dot_general(a, b, (((1,), (1,)), ((), ())), preferred_element_type=F32)


def _dot_tn(a, b):
    return lax.dot_general(a, b, (((0,), (0,)), ((), ())), preferred_element_type=F32)


def _proj_kernel(x_ref, *refs, n_out, chunk):
    w_refs, o_refs = refs[:n_out], refs[n_out:]
    xb = x_ref[...].astype(BF16)
    for w_ref, o_ref in zip(w_refs, o_refs):
        n = w_ref.shape[1]
        for c0 in range(0, n, chunk):
            c1 = min(n, c0 + chunk)
            o_ref[:, c0:c1] = jnp.dot(xb, w_ref[:, c0:c1], preferred_element_type=F32).astype(o_ref.dtype)


def _proj(x, ws, out_dtypes, name):
    t, k = x.shape
    tm = min(TOKEN_TILE, t)
    in_specs = [pl.BlockSpec((tm, k), lambda i: (i, 0))]
    in_specs += [pl.BlockSpec(w.shape, lambda i: (0, 0)) for w in ws]
    out_specs = [pl.BlockSpec((tm, w.shape[1]), lambda i: (i, 0)) for w in ws]
    out_shape = [jax.ShapeDtypeStruct((t, w.shape[1]), dt) for w, dt in zip(ws, out_dtypes)]
    return pl.pallas_call(
        functools.partial(_proj_kernel, n_out=len(ws), chunk=512),
        grid=(t // tm,),
        in_specs=in_specs,
        out_specs=out_specs,
        out_shape=out_shape,
        compiler_params=_cparams("parallel"),
        name=name,
    )(x, *ws)


def _mamba_kernel(z_ref, x_ref, b_ref, c_ref, dt_ref, sel_ref, ptab_ref,
                  cwx_ref, cwb_ref, cwc_ref, cbx_ref, cbb_ref, cbc_ref,
                  c0x_ref, c0b_ref, c0c_ref, h0_ref, nw_ref,
                  y_ref, cox_ref, cob_ref, coc_ref, hout_ref,
                  sx, sb, sc, hs, *, q, nc):
    c = pl.program_id(2)
    pad = SUBLANES
    hist = SSM_CONV - 1

    @pl.when(c == 0)
    def _():
        sx[pad - hist:pad, :] = c0x_ref[0]
        sb[pad - hist:pad, :] = c0b_ref[0]
        sc[pad - hist:pad, :] = c0c_ref[0]
        hs[...] = h0_ref[0, 0]

    def conv(cur_ref, s, w_ref, bias_ref):
        s[pad:pad + q, :] = cur_ref[0].astype(F32)
        w = w_ref[...]
        acc = bias_ref[...]
        for k in range(SSM_CONV):
            acc = acc + w[k:k + 1, :] * s[pad - hist + k:pad - hist + k + q, :]
        tail = s[pad + q - hist:pad + q, :]
        s[pad - hist:pad, :] = tail
        return _silu(acc), tail

    xc, tail_x = conv(x_ref, sx, cwx_ref, cbx_ref)
    bc, tail_b = conv(b_ref, sb, cwb_ref, cbb_ref)
    cc, tail_c = conv(c_ref, sc, cwc_ref, cbc_ref)

    ptab = ptab_ref[0]
    dt4 = jnp.dot(dt_ref[0], sel_ref[0], precision=HIGHEST, preferred_element_type=F32)
    dt4 = _softplus(dt4 + ptab[0:1, :])
    da = dt4 * (-jnp.exp(ptab[1:2, :]))
    ti = lax.broadcasted_iota(jnp.int32, (q, q), 0)
    si = lax.broadcasted_iota(jnp.int32, (q, q), 1)
    tri = ti >= si
    acum = jnp.dot(tri.astype(F32), da, precision=HIGHEST, preferred_element_type=F32)
    acum_t = acum.T

    lane = lax.broadcasted_iota(jnp.int32, (1, GROUP_WIDTH), 1)

    def expand(v):
        r = v.shape[0]
        out = jnp.broadcast_to(v[:, HEADS_PER_GROUP - 1:HEADS_PER_GROUP], (r, GROUP_WIDTH))
        for j in range(HEADS_PER_GROUP - 2, -1, -1):
            out = jnp.where(lane < SSM_HEAD_DIM * (j + 1), jnp.broadcast_to(v[:, j:j + 1], (r, GROUP_WIDTH)), out)
        return out

    dt_e = expand(dt4)
    ac_e = expand(acum)
    ac_last = acum[q - 1:q, :]
    ac_last_e = expand(ac_last)

    xdt = xc * dt_e
    xdt_b = xdt.astype(BF16)
    bb = bc.astype(BF16)
    cb_ = cc.astype(BF16)
    cbm = _dot_nt(cb_, bb)
    hprev = hs[...]
    y = _dot_nt(cb_, hprev.astype(BF16)) * jnp.exp(ac_e)
    zero_b = jnp.zeros_like(xdt_b)
    for j in range(HEADS_PER_GROUP):
        seg = acum[:, j:j + 1] - acum_t[j:j + 1, :]
        lm = jnp.exp(jnp.where(tri, seg, -jnp.inf))
        wj = (cbm * lm).astype(BF16)
        xm = jnp.where((lane >= SSM_HEAD_DIM * j) & (lane < SSM_HEAD_DIM * (j + 1)), xdt_b, zero_b)
        y = y + jnp.dot(wj, xm, preferred_element_type=F32)

    xw = (xdt * jnp.exp(ac_last_e - ac_e)).astype(BF16)
    st = _dot_tn(xw, bb)
    cd = jnp.exp(ac_last)
    for j in range(HEADS_PER_GROUP):
        r0, r1 = SSM_HEAD_DIM * j, SSM_HEAD_DIM * (j + 1)
        hs[r0:r1, :] = jnp.broadcast_to(cd[:, j:j + 1], (SSM_HEAD_DIM, SSM_D_STATE)) * hprev[r0:r1, :] + st[r0:r1, :]

    y = y + expand(ptab[2:3, :]) * xc
    zf = z_ref[0].astype(F32)
    y = y * _silu(zf)
    ms = jnp.mean(y * y, axis=-1, keepdims=True)
    y_ref[0] = (y * lax.rsqrt(ms + LN_EPS) * nw_ref[...]).astype(y_ref.dtype)

    @pl.when(c == nc - 1)
    def _():
        cox_ref[0] = tail_x
        cob_ref[0] = tail_b
        coc_ref[0] = tail_c
        hout_ref[0, 0] = hs[...]


def _mamba_core(zx, dt, conv0, ssm0, sel, ptab, conv_w, conv_b, norm_w, bsz, seqlen):
    q = math.gcd(seqlen, SSM_CHUNK)
    nc = seqlen // q
    g = SSM_GROUPS
    zx3 = zx.reshape(bsz, seqlen, zx.shape[-1])
    dt3 = dt.reshape(bsz, seqlen, LANES)
    h0 = ssm0.reshape(bsz, g, GROUP_WIDTH, SSM_D_STATE)
    zb = SSM_D_INNER // GROUP_WIDTH
    bblk = (2 * SSM_D_INNER) // SSM_D_STATE
    cblk = bblk + g
    cwb0 = SSM_D_INNER // SSM_D_STATE
    cwc0 = cwb0 + g
    hist = SSM_CONV - 1
    in_specs = [
        pl.BlockSpec((1, q, GROUP_WIDTH), lambda b, gi, c: (b, c, gi)),
        pl.BlockSpec((1, q, GROUP_WIDTH), lambda b, gi, c: (b, c, zb + gi)),
        pl.BlockSpec((1, q, SSM_D_STATE), lambda b, gi, c: (b, c, bblk + gi)),
        pl.BlockSpec((1, q, SSM_D_STATE), lambda b, gi, c: (b, c, cblk + gi)),
        pl.BlockSpec((1, q, LANES), lambda b, gi, c: (b, c, 0)),
        pl.BlockSpec((1, LANES, LANES), lambda b, gi, c: (gi, 0, 0)),
        pl.BlockSpec((1, SUBLANES, LANES), lambda b, gi, c: (gi, 0, 0)),
        pl.BlockSpec((SSM_CONV, GROUP_WIDTH), lambda b, gi, c: (0, gi)),
        pl.BlockSpec((SSM_CONV, SSM_D_STATE), lambda b, gi, c: (0, cwb0 + gi)),
        pl.BlockSpec((SSM_CONV, SSM_D_STATE), lambda b, gi, c: (0, cwc0 + gi)),
        pl.BlockSpec((1, GROUP_WIDTH), lambda b, gi, c: (0, gi)),
        pl.BlockSpec((1, SSM_D_STATE), lambda b, gi, c: (0, cwb0 + gi)),
        pl.BlockSpec((1, SSM_D_STATE), lambda b, gi, c: (0, cwc0 + gi)),
        pl.BlockSpec((1, hist, GROUP_WIDTH), lambda b, gi, c: (b, 0, gi)),
        pl.BlockSpec((1, hist, SSM_D_STATE), lambda b, gi, c: (b, 0, cwb0 + gi)),
        pl.BlockSpec((1, hist, SSM_D_STATE), lambda b, gi, c: (b, 0, cwc0 + gi)),
        pl.BlockSpec((1, 1, GROUP_WIDTH, SSM_D_STATE), lambda b, gi, c: (b, gi, 0, 0)),
        pl.BlockSpec((1, GROUP_WIDTH), lambda b, gi, c: (0, gi)),
    ]
    out_specs = [
        pl.BlockSpec((1, q, GROUP_WIDTH), lambda b, gi, c: (b, c, gi)),
        pl.BlockSpec((1, hist, GROUP_WIDTH), lambda b, gi, c: (b, 0, gi)),
        pl.BlockSpec((1, hist, SSM_D_STATE), lambda b, gi, c: (b, 0, gi)),
        pl.BlockSpec((1, hist, SSM_D_STATE), lambda b, gi, c: (b, 0, gi)),
        pl.BlockSpec((1, 1, GROUP_WIDTH, SSM_D_STATE), lambda b, gi, c: (b, gi, 0, 0)),
    ]
    gn = g * SSM_D_STATE
    out_shape = [
        jax.ShapeDtypeStruct((bsz, seqlen, SSM_D_INNER), BF16),
        jax.ShapeDtypeStruct((bsz, hist, SSM_D_INNER), F32),
        jax.ShapeDtypeStruct((bsz, hist, gn), F32),
        jax.ShapeDtypeStruct((bsz, hist, gn), F32),
        jax.ShapeDtypeStruct((bsz, g, GROUP_WIDTH, SSM_D_STATE), F32),
    ]
    scratch = [
        pltpu.VMEM((q + SUBLANES, GROUP_WIDTH), F32),
        pltpu.VMEM((q + SUBLANES, SSM_D_STATE), F32),
        pltpu.VMEM((q + SUBLANES, SSM_D_STATE), F32),
        pltpu.VMEM((GROUP_WIDTH, SSM_D_STATE), F32),
    ]
    y, cox, cob, coc, hout = pl.pallas_call(
        functools.partial(_mamba_kernel, q=q, nc=nc),
        grid=(bsz, g, nc),
        in_specs=in_specs,
        out_specs=out_specs,
        out_shape=out_shape,
        scratch_shapes=scratch,
        compiler_params=_cparams("parallel", "parallel", "arbitrary"),
        name="mamba_core",
    )(zx3, zx3, zx3, zx3, dt3, sel, ptab, conv_w, conv_w, conv_w, conv_b, conv_b, conv_b,
      conv0, conv0, conv0, h0, norm_w)
    new_conv = jnp.concatenate([cox, cob, coc], axis=-1)
    return (y.reshape(bsz * seqlen, SSM_D_INNER), new_conv,
            hout.reshape(bsz, SSM_HEADS, SSM_HEAD_DIM, SSM_D_STATE))


def _post_mixer_kernel(y_ref, w_ref, xres_ref, g_ref, b_ref, wr_ref, br_ref,
                       x1_ref, route_ref, cnt_ref, base, *, tm):
    i = pl.program_id(0)

    @pl.when(i == 0)
    def _():
        base[...] = jnp.zeros_like(base)

    h = jnp.dot(y_ref[...], w_ref[...], preferred_element_type=F32)
    x1 = _layer_norm(DN_ALPHA * xres_ref[...] + h, g_ref[...], b_ref[...])
    x1_ref[...] = x1

    logits = jnp.dot(x1, wr_ref[...], precision=HIGHEST, preferred_element_type=F32) + br_ref[...]
    lane = lax.broadcasted_iota(jnp.int32, (tm, LANES), 1)
    neg = -jnp.inf
    gmask = lane < MOE_GROUPS
    gl = jnp.where(gmask, logits, neg)
    mg = jnp.max(gl, axis=1, keepdims=True)
    grp = jnp.min(jnp.where(gl == mg, lane, LANES), axis=1, keepdims=True)
    pg = 1.0 / jnp.sum(jnp.where(gmask, jnp.exp(logits - mg), 0.0), axis=1, keepdims=True)
    lo = MOE_GROUPS + MOE_EXPERTS_PER_GROUP * grp
    el = jnp.where((lane >= lo) & (lane < lo + MOE_EXPERTS_PER_GROUP), logits, neg)
    v1 = jnp.max(el, axis=1, keepdims=True)
    i1 = jnp.min(jnp.where(el == v1, lane, LANES), axis=1, keepdims=True)
    el2 = jnp.where(lane == i1, neg, el)
    v2 = jnp.max(el2, axis=1, keepdims=True)
    i2 = jnp.min(jnp.where(el2 == v2, lane, LANES), axis=1, keepdims=True)
    t = jnp.exp(v2 - v1)
    wa = pg / (1.0 + t)
    wb = pg * t / (1.0 + t)
    e1 = i1 - MOE_GROUPS
    e2 = i2 - MOE_GROUPS

    oh = ((lane == e1) | (lane == e2)).astype(BF16)
    ri = lax.broadcasted_iota(jnp.int32, (tm, tm), 0)
    ci = lax.broadcasted_iota(jnp.int32, (tm, tm), 1)
    before = jnp.dot((ri > ci).astype(BF16), oh, preferred_element_type=F32) + base[...]
    r1 = jnp.sum(jnp.where(lane == e1, before, 0.0), axis=1, keepdims=True)
    r2 = jnp.sum(jnp.where(lane == e2, before, 0.0), axis=1, keepdims=True)
    new_base = base[...] + jnp.sum(oh.astype(F32), axis=0, keepdims=True)
    base[...] = new_base
    cnt_ref[...] = new_base

    route = jnp.where(lane == 0, e1.astype(F32), 0.0)
    route = jnp.where(lane == 1, e2.astype(F32), route)
    route = jnp.where(lane == 2, r1, route)
    route = jnp.where(lane == 3, r2, route)
    route = jnp.where(lane == 4, wa, route)
    route = jnp.where(lane == 5, wb, route)
    route_ref[...] = route


def _post_mixer(y, w, xres, ln_g, ln_b, wr, br, name):
    t, kin = y.shape
    tm = min(TOKEN_TILE, t)
    return pl.pallas_call(
        functools.partial(_post_mixer_kernel, tm=tm),
        grid=(t // tm,),
        in_specs=[
            pl.BlockSpec((tm, kin), lambda i: (i, 0)),
            pl.BlockSpec((kin, D_MODEL), lambda i: (0, 0)),
            pl.BlockSpec((tm, D_MODEL), lambda i: (i, 0)),
            pl.BlockSpec((1, D_MODEL), lambda i: (0, 0)),
            pl.BlockSpec((1, D_MODEL), lambda i: (0, 0)),
            pl.BlockSpec((D_MODEL, LANES), lambda i: (0, 0)),
            pl.BlockSpec((1, LANES), lambda i: (0, 0)),
        ],
        out_specs=[
            pl.BlockSpec((tm, D_MODEL), lambda i: (i, 0)),
            pl.BlockSpec((tm, LANES), lambda i: (i, 0)),
            pl.BlockSpec((1, LANES), lambda i: (0, 0)),
        ],
        out_shape=[
            jax.ShapeDtypeStruct((t, D_MODEL), F32),
            jax.ShapeDtypeStruct((t, LANES), F32),
            jax.ShapeDtypeStruct((1, LANES), F32),
        ],
        scratch_shapes=[pltpu.VMEM((1, LANES), F32)],
        compiler_params=_cparams("arbitrary"),
        name=name,
    )(y, w, xres, ln_g, ln_b, wr, br)


def _row_copy(src_ref, src_row, dst_ref, dst_row, sem):
    return pltpu.make_async_copy(src_ref.at[pl.ds(src_row, 1)], dst_ref.at[pl.ds(dst_row, 1)], sem)


def _dispatch_kernel(dest_ref, x_ref, xs_ref, sem, *, tm):
    def start(r, carry):
        _row_copy(x_ref, r, xs_ref, dest_ref[0, 0, 2 * r], sem).start()
        _row_copy(x_ref, r, xs_ref, dest_ref[0, 0, 2 * r + 1], sem).start()
        return carry

    lax.fori_loop(0, tm, start, 0)

    def wait(r, carry):
        _row_copy(x_ref, 0, xs_ref, 0, sem).wait()
        return carry

    lax.fori_loop(0, 2 * tm, wait, 0)


def _dispatch(x1, dest3, tm):
    t = x1.shape[0]
    return pl.pallas_call(
        functools.partial(_dispatch_kernel, tm=tm),
        grid=(t // tm,),
        in_specs=[
            pl.BlockSpec((1, 1, 2 * tm), lambda i: (i, 0, 0), memory_space=pltpu.SMEM),
            pl.BlockSpec((tm, D_MODEL), lambda i: (i, 0)),
        ],
        out_specs=pl.BlockSpec(memory_space=pl.ANY),
        out_shape=jax.ShapeDtypeStruct((2 * t, D_MODEL), F32),
        scratch_shapes=[pltpu.SemaphoreType.DMA(())],
        compiler_params=_cparams("arbitrary"),
        name="moe_dispatch",
    )(dest3, x1)


def _ffn_kernel(wt_ref, we_ref, wlo_ref, whi_ref, wfirst_ref,
                xs_ref, w1_ref, w3_ref, w2_ref, ys_ref, *, tf):
    w = pl.program_id(0)
    lo = wlo_ref[w]
    hi = whi_ref[w]

    @pl.when(wfirst_ref[w] == 1)
    def _():
        ys_ref[...] = jnp.zeros_like(ys_ref)

    @pl.when(hi > lo)
    def _():
        rows = wt_ref[w] * tf + lax.broadcasted_iota(jnp.int32, (tf, 1), 0)
        mask = (rows >= lo) & (rows < hi)
        xb = xs_ref[...].astype(BF16)
        a = jnp.dot(xb, w1_ref[0].astype(BF16), preferred_element_type=F32)
        b = jnp.dot(xb, w3_ref[0].astype(BF16), preferred_element_type=F32)
        hmid = (_silu(a) * b).astype(BF16)
        y = jnp.dot(hmid, w2_ref[0].astype(BF16), preferred_element_type=F32)
        ys_ref[...] = jnp.where(mask, y, ys_ref[...])


def _ffn(xs, w1, w3, w2, work, tf):
    n = xs.shape[0]
    n_work = work[0].shape[0]
    grid_spec = pltpu.PrefetchScalarGridSpec(
        num_scalar_prefetch=5,
        grid=(n_work,),
        in_specs=[
            pl.BlockSpec((tf, D_MODEL), lambda w, wt, we, wlo, whi, wf: (wt[w], 0)),
            pl.BlockSpec((1, D_MODEL, MOE_FF), lambda w, wt, we, wlo, whi, wf: (we[w], 0, 0)),
            pl.BlockSpec((1, D_MODEL, MOE_FF), lambda w, wt, we, wlo, whi, wf: (we[w], 0, 0)),
            pl.BlockSpec((1, MOE_FF, D_MODEL), lambda w, wt, we, wlo, whi, wf: (we[w], 0, 0)),
        ],
        out_specs=pl.BlockSpec((tf, D_MODEL), lambda w, wt, we, wlo, whi, wf: (wt[w], 0)),
    )
    return pl.pallas_call(
        functools.partial(_ffn_kernel, tf=tf),
        grid_spec=grid_spec,
        out_shape=jax.ShapeDtypeStruct((n, D_MODEL), F32),
        compiler_params=_cparams("arbitrary"),
        name="moe_ffn",
    )(*work, xs, w1, w3, w2)


def _combine_kernel(dest_ref, x1_ref, route_ref, g_ref, b_ref, ys_ref, o_ref, buf0, buf1, sem, *, tm):
    def start(r, carry):
        _row_copy(ys_ref, dest_ref[0, 0, 2 * r], buf0, r, sem).start()
        _row_copy(ys_ref, dest_ref[0, 0, 2 * r + 1], buf1, r, sem).start()
        return carry

    lax.fori_loop(0, tm, start, 0)

    def wait(r, carry):
        _row_copy(ys_ref, 0, buf0, 0, sem).wait()
        return carry

    lax.fori_loop(0, 2 * tm, wait, 0)

    route = route_ref[...]
    y = route[:, 4:5] * buf0[...] + route[:, 5:6] * buf1[...]
    o_ref[...] = _layer_norm(DN_ALPHA * x1_ref[...] + y, g_ref[...], b_ref[...])


def _combine(x1, route, ys, dest3, ln_g, ln_b, tm, name):
    t = x1.shape[0]
    return pl.pallas_call(
        functools.partial(_combine_kernel, tm=tm),
        grid=(t // tm,),
        in_specs=[
            pl.BlockSpec((1, 1, 2 * tm), lambda i: (i, 0, 0), memory_space=pltpu.SMEM),
            pl.BlockSpec((tm, D_MODEL), lambda i: (i, 0)),
            pl.BlockSpec((tm, LANES), lambda i: (i, 0)),
            pl.BlockSpec((1, D_MODEL), lambda i: (0, 0)),
            pl.BlockSpec((1, D_MODEL), lambda i: (0, 0)),
            pl.BlockSpec(memory_space=pl.ANY),
        ],
        out_specs=pl.BlockSpec((tm, D_MODEL), lambda i: (i, 0)),
        out_shape=jax.ShapeDtypeStruct((t, D_MODEL), F32),
        scratch_shapes=[
            pltpu.VMEM((tm, D_MODEL), F32),
            pltpu.VMEM((tm, D_MODEL), F32),
            pltpu.SemaphoreType.DMA(()),
        ],
        compiler_params=_cparams("arbitrary"),
        name=name,
    )(dest3, x1, route, ln_g, ln_b, ys)


def _moe(x1, route, counts, w1, w3, w2, ln_g, ln_b, name):
    t = x1.shape[0]
    n = 2 * t
    tm = min(TOKEN_TILE, t)
    tf = min(FFN_TILE, n)
    cnt = counts[0, :MOE_EXPERTS].astype(jnp.int32)
    offs = jnp.concatenate([jnp.zeros((1,), jnp.int32), jnp.cumsum(cnt)])
    e = route[:, 0:2].astype(jnp.int32)
    r = route[:, 2:4].astype(jnp.int32)
    dest = offs[e] + r
    dest3 = dest.reshape(t // tm, 1, 2 * tm)

    n_tiles = n // tf
    n_work = n_tiles + MOE_EXPERTS
    first_tile = offs[:-1] // tf
    last_tile = (offs[1:] - 1) // tf
    ntile_e = jnp.where(cnt > 0, last_tile - first_tile + 1, 0)
    wstart = jnp.concatenate([jnp.zeros((1,), jnp.int32), jnp.cumsum(ntile_e)])
    total = wstart[-1]
    widx = jnp.arange(n_work, dtype=jnp.int32)
    we = jnp.clip(jnp.searchsorted(wstart, widx, side="right") - 1, 0, MOE_EXPERTS - 1).astype(jnp.int32)
    wt = first_tile[we] + (widx - wstart[we])
    valid = widx < total
    wt = jnp.where(valid, wt, n_tiles - 1).astype(jnp.int32)
    wlo = jnp.where(valid, offs[we], 0).astype(jnp.int32)
    whi = jnp.where(valid, offs[we + 1], 0).astype(jnp.int32)
    last_e = we[jnp.maximum(total - 1, 0)]
    we = jnp.where(valid, we, last_e).astype(jnp.int32)
    wfirst = jnp.concatenate([jnp.ones((1,), jnp.int32), (wt[1:] != wt[:-1]).astype(jnp.int32)])

    xs = _dispatch(x1, dest3, tm)
    ys = _ffn(xs, w1, w3, w2, (wt, we, wlo, whi, wfirst), tf)
    return _combine(x1, route, ys, dest3, ln_g, ln_b, tm, name)


def _bias_kernel(idx_ref, rbt_ref, o_ref, *, width):
    idx = idx_ref[...]
    bucket = lax.broadcasted_iota(jnp.int32, (NUM_BUCKETS, width), 0)
    onehot = (bucket == idx).astype(F32)
    vals = jnp.dot(rbt_ref[...], onehot, precision=HIGHEST, preferred_element_type=F32)
    o_ref[...] = jnp.where(idx < 0, -jnp.inf, vals)


def _bias_tiles(rel_bias, idx_np):
    n = idx_np.shape[0]
    width = 8192
    assert n % width == 0
    nh = rel_bias.shape[1]
    return pl.pallas_call(
        functools.partial(_bias_kernel, width=width),
        grid=(n // width,),
        in_specs=[
            pl.BlockSpec((1, width), lambda i: (0, i)),
            pl.BlockSpec((nh, NUM_BUCKETS), lambda i: (0, 0)),
        ],
        out_specs=pl.BlockSpec((nh, width), lambda i: (0, i)),
        out_shape=jax.ShapeDtypeStruct((nh, n), F32),
        compiler_params=_cparams("parallel"),
        name="rel_bias_tiles",
    )(jnp.asarray(idx_np.reshape(1, n)), rel_bias.T)


def _bucket_of_distance(dist):
    n = np.maximum(dist, 0)
    max_exact = NUM_BUCKETS // 2
    nf = np.maximum(n, 1).astype(np.float32)
    large = max_exact + (np.log(nf / np.float32(max_exact)) / np.float32(math.log(MAX_DISTANCE / max_exact))
                         * np.float32(NUM_BUCKETS - max_exact)).astype(np.int32)
    large = np.minimum(large, NUM_BUCKETS - 1)
    return np.where(n < max_exact, n, large).astype(np.int32)


def _bucket_tile(dist):
    return np.where(dist >= 0, _bucket_of_distance(dist), -1).astype(np.int32)


def _lambda_value(lam_ref, layer_idx):
    lv = lam_ref[...]
    s1 = jnp.sum(lv[0:1, :] * lv[1:2, :], axis=1, keepdims=True)
    s2 = jnp.sum(lv[2:3, :] * lv[3:4, :], axis=1, keepdims=True)
    lam_init = 0.8 - 0.6 * math.exp(-0.3 * layer_idx)
    return jnp.exp(s1) - jnp.exp(s2) + lam_init, lam_init


def _attn_prompt_kernel(q_ref, k_ref, v_ref, bias_ref, lam_ref, sw_ref, o_ref, *, tq, layer_idx):
    qi = pl.program_id(2)
    qb = q_ref[0]
    lane = lax.broadcasted_iota(jnp.int32, (tq, 2 * ATT_HEAD_DIM), 1)
    zero = jnp.zeros_like(qb)
    q2 = jnp.concatenate([jnp.where(lane < ATT_HEAD_DIM, qb, zero),
                          jnp.where(lane >= ATT_HEAD_DIM, qb, zero)], axis=0)

    def step(j, carry):
        m, l, acc = carry
        k0 = pl.multiple_of(j * tq, tq)
        kj = k_ref[0, pl.ds(k0, tq), :].astype(BF16)
        vj = v_ref[0, pl.ds(k0, tq), :].astype(BF16)
        s = _dot_nt(q2, kj) + bias_ref[0, jnp.minimum(qi - j, 2)]
        m_new = jnp.maximum(m, jnp.max(s, axis=1, keepdims=True))
        alpha = jnp.exp(m - m_new)
        p = jnp.exp(s - m_new)
        l = alpha * l + jnp.sum(p, axis=1, keepdims=True)
        acc = alpha * acc + jnp.dot(p.astype(BF16), vj, preferred_element_type=F32)
        return m_new, l, acc

    m0 = jnp.full((2 * tq, 1), -jnp.inf, F32)
    l0 = jnp.zeros((2 * tq, 1), F32)
    a0 = jnp.zeros((2 * tq, 2 * ATT_HEAD_DIM), F32)
    m, l, acc = lax.fori_loop(0, qi + 1, step, (m0, l0, a0))
    o = acc / l
    lam, lam_init = _lambda_value(lam_ref, layer_idx)
    d = o[:tq] - lam * o[tq:]
    ms = jnp.mean(d * d, axis=-1, keepdims=True)
    o_ref[0] = (d * lax.rsqrt(ms + LN_EPS) * sw_ref[...] * (1.0 - lam_init)).astype(o_ref.dtype)


def _attn_prompt(q, kv, bias, lam_vec, subln_w, bsz, seqlen, layer_idx):
    tq = min(ATT_TILE, seqlen)
    hd2 = 2 * ATT_HEAD_DIM
    q3 = q.reshape(bsz, seqlen, ATT_QK_DIM)
    kv3 = kv.reshape(bsz, seqlen, ATT_QK_DIM + ATT_V_DIM)
    vblk = ATT_QK_DIM // hd2
    out = pl.pallas_call(
        functools.partial(_attn_prompt_kernel, tq=tq, layer_idx=layer_idx),
        grid=(bsz, ATT_HEADS, seqlen // tq),
        in_specs=[
            pl.BlockSpec((1, tq, hd2), lambda b, h, i: (b, i, h)),
            pl.BlockSpec((1, seqlen, hd2), lambda b, h, i: (b, 0, h)),
            pl.BlockSpec((1, seqlen, hd2), lambda b, h, i: (b, 0, vblk + h)),
            pl.BlockSpec((1, 3, 2 * tq, tq), lambda b, h, i: (h, 0, 0, 0)),
            pl.BlockSpec(lam_vec.shape, lambda b, h, i: (0, 0)),
            pl.BlockSpec((1, hd2), lambda b, h, i: (0, 0)),
        ],
        out_specs=pl.BlockSpec((1, tq, hd2), lambda b, h, i: (b, i, h)),
        out_shape=jax.ShapeDtypeStruct((bsz, seqlen, ATT_V_DIM), BF16),
        compiler_params=_cparams("parallel", "parallel", "arbitrary"),
        name="attn_prompt",
    )(q3, kv3, kv3, bias, lam_vec, subln_w)
    return out.reshape(bsz * seqlen, ATT_V_DIM)


def _attn_sample_kernel(pt_ref, q_ref, kc_ref, vc_ref, kn_ref, vn_ref, bias_ref, lam_ref, sw_ref,
                        o_ref, m_s, l_s, acc_s, kpad, vpad, *, nq, n_pages, layer_idx):
    j = pl.program_id(1)
    maps = 2 * ATT_HEADS
    rows = maps * nq

    @pl.when(j == 0)
    def _():
        m_s[...] = jnp.full_like(m_s, -jnp.inf)
        l_s[...] = jnp.zeros_like(l_s)
        acc_s[...] = jnp.zeros_like(acc_s)

    def process(k_at, v_at, bias):
        qf = q_ref[0]
        s_parts = []
        for i in range(maps):
            qi = qf[:, ATT_HEAD_DIM * i:ATT_HEAD_DIM * (i + 1)]
            s_parts.append(_dot_nt(qi, k_at(i).astype(BF16)))
        s = jnp.concatenate(s_parts, axis=0) + bias
        m = m_s[...]
        m_new = jnp.maximum(m, jnp.max(s, axis=1, keepdims=True))
        alpha = jnp.exp(m - m_new)
        p = jnp.exp(s - m_new)
        l_s[...] = alpha * l_s[...] + jnp.sum(p, axis=1, keepdims=True)
        m_s[...] = m_new
        pb = p.astype(BF16)
        for h in range(ATT_HEADS):
            r0, r1 = 2 * nq * h, 2 * nq * (h + 1)
            pv = jnp.dot(pb[r0:r1, :], v_at(h).astype(BF16), preferred_element_type=F32)
            acc_s[r0:r1, :] = alpha[r0:r1, :] * acc_s[r0:r1, :] + pv

    @pl.when(j < n_pages)
    def _():
        bidx = jnp.where(j == n_pages - 1, 0, 2)
        process(lambda i: kc_ref[0, :, i, :], lambda h: vc_ref[0, :, h, :], bias_ref[bidx])

    @pl.when(j == n_pages)
    def _():
        kpad[...] = jnp.zeros_like(kpad)
        vpad[...] = jnp.zeros_like(vpad)
        kpad[0:nq] = kn_ref[0]
        vpad[0:nq] = vn_ref[0]
        process(lambda i: kpad[:, i, :], lambda h: vpad[:, h, :], bias_ref[1])
        o = acc_s[...] / l_s[...]
        lam, lam_init = _lambda_value(lam_ref, layer_idx)
        for h in range(ATT_HEADS):
            r0 = 2 * nq * h
            d = o[r0:r0 + nq, :] - lam * o[r0 + nq:r0 + 2 * nq, :]
            ms = jnp.mean(d * d, axis=-1, keepdims=True)
            c0 = 2 * ATT_HEAD_DIM * h
            o_ref[0, :, c0:c0 + 2 * ATT_HEAD_DIM] = (
                d * lax.rsqrt(ms + LN_EPS) * sw_ref[...] * (1.0 - lam_init)).astype(o_ref.dtype)


def _attn_sample(q, k_new, v_new, cache_k, cache_v, page_table, bias, lam_vec, subln_w, layer_idx):
    bsz, nq = k_new.shape[0], k_new.shape[1]
    n_pages = page_table.shape[1]
    page = cache_k.shape[1]
    maps = 2 * ATT_HEADS
    rows = maps * nq
    q3 = q.reshape(bsz, nq, ATT_QK_DIM)
    grid_spec = pltpu.PrefetchScalarGridSpec(
        num_scalar_prefetch=1,
        grid=(bsz, n_pages + 1),
        in_specs=[
            pl.BlockSpec((1, nq, ATT_QK_DIM), lambda b, j, pt: (b, 0, 0)),
            pl.BlockSpec((1, page, maps, ATT_HEAD_DIM),
                         lambda b, j, pt: (pt[b, jnp.minimum(j, n_pages - 1)], 0, 0, 0)),
            pl.BlockSpec((1, page, ATT_HEADS, 2 * ATT_HEAD_DIM),
                         lambda b, j, pt: (pt[b, jnp.minimum(j, n_pages - 1)], 0, 0, 0)),
            pl.BlockSpec((1, nq, maps, ATT_HEAD_DIM), lambda b, j, pt: (b, 0, 0, 0)),
            pl.BlockSpec((1, nq, ATT_HEADS, 2 * ATT_HEAD_DIM), lambda b, j, pt: (b, 0, 0, 0)),
            pl.BlockSpec((3, rows, page), lambda b, j, pt: (0, 0, 0)),
            pl.BlockSpec(lam_vec.shape, lambda b, j, pt: (0, 0)),
            pl.BlockSpec((1, 2 * ATT_HEAD_DIM), lambda b, j, pt: (0, 0)),
        ],
        out_specs=pl.BlockSpec((1, nq, ATT_V_DIM), lambda b, j, pt: (b, 0, 0)),
        scratch_shapes=[
            pltpu.VMEM((rows, 1), F32),
            pltpu.VMEM((rows, 1), F32),
            pltpu.VMEM((rows, 2 * ATT_HEAD_DIM), F32),
            pltpu.VMEM((page, maps, ATT_HEAD_DIM), F32),
            pltpu.VMEM((page, ATT_HEADS, 2 * ATT_HEAD_DIM), F32),
        ],
    )
    out = pl.pallas_call(
        functools.partial(_attn_sample_kernel, nq=nq, n_pages=n_pages, layer_idx=layer_idx),
        grid_spec=grid_spec,
        out_shape=jax.ShapeDtypeStruct((bsz, nq, ATT_V_DIM), BF16),
        compiler_params=_cparams("parallel", "arbitrary"),
        name="attn_sample",
    )(page_table, q3, cache_k, cache_v, k_new, v_new, bias, lam_vec, subln_w)
    return out.reshape(bsz * nq, ATT_V_DIM)


def _router_params(wg, bg, we, be):
    wexp = jnp.transpose(we, (1, 0, 2)).reshape(D_MODEL, MOE_EXPERTS)
    wr = jnp.concatenate([wg, wexp], axis=1)
    wr = jnp.pad(wr, ((0, 0), (0, LANES - wr.shape[1])))
    br = jnp.concatenate([bg, be.reshape(MOE_EXPERTS)])
    br = jnp.pad(br, (0, LANES - br.shape[0])).reshape(1, LANES)
    return wr, br


def _head_tables(dt_bias, a_log, d_skip):
    def per_group(v):
        return jnp.pad(v.reshape(SSM_GROUPS, HEADS_PER_GROUP), ((0, 0), (0, LANES - HEADS_PER_GROUP)))

    rows = jnp.stack([per_group(dt_bias), per_group(a_log), per_group(d_skip)], axis=1)
    ptab = jnp.pad(rows, ((0, 0), (0, SUBLANES - 3), (0, 0)))
    sel = np.zeros((SSM_GROUPS, LANES, LANES), np.float32)
    for g in range(SSM_GROUPS):
        for j in range(HEADS_PER_GROUP):
            sel[g, HEADS_PER_GROUP * g + j, j] = 1.0
    return ptab, jnp.asarray(sel)


def _prompt_bias_index(tq):
    r = np.arange(tq)[:, None]
    c = np.arange(tq)[None, :]
    tiles = [_bucket_tile(delta * tq + r - c) for delta in range(3)]
    return np.stack(tiles).reshape(-1)


def _sample_bias_index(nq, page, past_len):
    r = np.arange(nq)[:, None]
    c = np.arange(page)[None, :]
    last_page = _bucket_tile(past_len + r - (past_len - page) - c)
    own = np.where(c < nq, _bucket_tile(r - c), -1).astype(np.int32)
    far = _bucket_tile(np.full((nq, page), MAX_DISTANCE + page))
    return np.stack([last_page, own, far]).reshape(-1)


def kernel(x_prompt, x_sample, state_ssm, state_conv, cache_k, cache_v, page_table, ln_g, ln_b, m_w_in,
           m_conv_w, m_conv_b, m_dt_bias, m_a_log, m_d, m_norm_w, m_w_out, kv_w, a_w_q, a_lambda,
           a_subln_w, a_w_o, rel_bias, moe_wg, moe_bg, moe_we, moe_be, moe_w1, moe_w3, moe_w2):
    bp, lp, _ = x_prompt.shape
    bs, ls, _ = x_sample.shape
    n_pages = page_table.shape[1]
    page = cache_k.shape[1]
    past_len = n_pages * page
    assert page == PAGE_SIZE and past_len >= page + MAX_DISTANCE - 1 + ls

    w_in = m_w_in[0]
    w_zx = w_in[:, :SSM_D_INNER + SSM_CONV_DIM].astype(BF16)
    w_dt = jnp.pad(w_in[:, SSM_D_INNER + SSM_CONV_DIM:], ((0, 0), (0, LANES - SSM_HEADS))).astype(BF16)
    ptab, sel = _head_tables(m_dt_bias[0], m_a_log[0], m_d[0])
    conv_w = m_conv_w[0]
    conv_b = m_conv_b[0].reshape(1, SSM_CONV_DIM)
    norm_w = m_norm_w[0].reshape(1, SSM_D_INNER)
    w_out = m_w_out[0].astype(BF16)
    w_kvq = [kv_w.astype(BF16), (a_w_q[0] * (ATT_HEAD_DIM ** -0.5)).astype(BF16)]
    w_o = a_w_o[0].astype(BF16)
    lam_vec = a_lambda[0]
    subln_w = a_subln_w[0].reshape(1, 2 * ATT_HEAD_DIM)
    routers = [_router_params(moe_wg[l], moe_bg[l], moe_we[l], moe_be[l]) for l in range(DEPTH)]
    experts = [(moe_w1[l].reshape(MOE_EXPERTS, D_MODEL, MOE_FF),
                moe_w3[l].reshape(MOE_EXPERTS, D_MODEL, MOE_FF),
                moe_w2[l].reshape(MOE_EXPERTS, MOE_FF, D_MODEL)) for l in range(DEPTH)]
    lng = ln_g.reshape(DEPTH, 2, 1, D_MODEL)
    lnb = ln_b.reshape(DEPTH, 2, 1, D_MODEL)

    tq = min(ATT_TILE, lp)
    maps = 2 * ATT_HEADS
    idx_p = _prompt_bias_index(tq)
    idx_s = _sample_bias_index(ls, page, past_len)
    n_p = idx_p.shape[0]
    n_s = idx_s.shape[0]
    n_tot = -(-(n_p + n_s) // 8192) * 8192
    idx_all = np.concatenate([idx_p, idx_s, np.zeros((n_tot - n_p - n_s,), np.int32)])
    tiles = _bias_tiles(rel_bias, idx_all)
    bias_p = tiles[:, :n_p].reshape(ATT_HEADS, 2, 3, tq, tq)
    bias_p = jnp.transpose(bias_p, (0, 2, 1, 3, 4)).reshape(ATT_HEADS, 3, 2 * tq, tq)
    bias_s = tiles[:, n_p:n_p + n_s].reshape(maps, 3, ls, page)
    bias_s = jnp.transpose(bias_s, (1, 0, 2, 3)).reshape(3, maps * ls, page)

    def trunk(x, ssm0, conv0, bsz, seqlen, past):
        t = bsz * seqlen
        xf = x.reshape(t, D_MODEL)
        zx, dt = _proj(xf, [w_zx, w_dt], [BF16, F32], "in_proj")
        y, new_conv, new_ssm = _mamba_core(zx, dt, conv0, ssm0, sel, ptab, conv_w, conv_b, norm_w, bsz, seqlen)
        x1, route, counts = _post_mixer(y, w_out, xf, lng[0, 0], lnb[0, 0], *routers[0], "mamba_out_ln_router")
        x2 = _moe(x1, route, counts, *experts[0], lng[0, 1], lnb[0, 1], "moe0_combine_ln")
        kv, qs = _proj(x2, w_kvq, [F32, BF16], "kvq_proj")
        k_new = kv[:, :ATT_QK_DIM].reshape(bsz, seqlen, maps, ATT_HEAD_DIM)
        v_new = kv[:, ATT_QK_DIM:].reshape(bsz, seqlen, ATT_HEADS, 2 * ATT_HEAD_DIM)
        if past is None:
            o = _attn_prompt(qs, kv, bias_p, lam_vec, subln_w, bsz, seqlen, DEPTH - 1)
        else:
            o = _attn_sample(qs, k_new, v_new, cache_k, cache_v, page_table, bias_s, lam_vec, subln_w, DEPTH - 1)
        x3, route, counts = _post_mixer(o, w_o, x2, lng[1, 0], lnb[1, 0], *routers[1], "attn_out_ln_router")
        x4 = _moe(x3, route, counts, *experts[1], lng[1, 1], lnb[1, 1], "moe1_combine_ln")
        return (x4.reshape(bsz, seqlen, D_MODEL), new_ssm[None], new_conv[None], k_new, v_new)

    ssm0_p = jnp.zeros((bp, SSM_HEADS, SSM_HEAD_DIM, SSM_D_STATE), F32)
    conv0_p = jnp.zeros((bp, SSM_CONV - 1, SSM_CONV_DIM), F32)
    y_p, ssm_p, conv_p, k_p, v_p = trunk(x_prompt, ssm0_p, conv0_p, bp, lp, None)
    y_s, ssm_s, conv_s, k_s, v_s = trunk(x_sample, state_ssm[0], state_conv[0], bs, ls, True)
    return (y_p, y_s, ssm_p, conv_p, k_p, v_p, ssm_s, conv_s, k_s, v_s)
```

```python
import functools
import math

import numpy as np
import jax
import jax.numpy as jnp
from jax import lax
from jax.experimental import pallas as pl
from jax.experimental.pallas import tpu as pltpu

F32 = jnp.float32
BF16 = jnp.bfloat16
HIGHEST = lax.Precision.HIGHEST

D_MODEL = 1024
DEPTH = 2
DN_ALPHA = (2.0 * DEPTH) ** 0.25
LN_EPS = 1e-5
SSM_D_INNER = 2048
SSM_HEAD_DIM = 64
SSM_HEADS = 32
SSM_GROUPS = 8
SSM_D_STATE = 128
SSM_CONV = 4
SSM_CHUNK = 128
SSM_CONV_DIM = SSM_D_INNER + 2 * SSM_GROUPS * SSM_D_STATE
HEADS_PER_GROUP = SSM_HEADS // SSM_GROUPS
GROUP_WIDTH = HEADS_PER_GROUP * SSM_HEAD_DIM
ATT_HEADS = 8
ATT_HEAD_DIM = 64
ATT_QK_DIM = 1024
ATT_V_DIM = 1024
NUM_BUCKETS = 32
MAX_DISTANCE = 128
MOE_GROUPS = 4
MOE_EXPERTS_PER_GROUP = 8
MOE_EXPERTS = MOE_GROUPS * MOE_EXPERTS_PER_GROUP
MOE_FF = 512
PAGE_SIZE = 128

LANES = 128
SUBLANES = 8
VMEM_LIMIT = 56 * 1024 * 1024

TOKEN_TILE = 256
FFN_TILE = 256
ATT_TILE = 256
MAMBA_GROUPS_PER_STEP = 2
MAMBA_SEQS_PER_STEP = 4
SAMPLE_PAGES_PER_STEP = 4
DMA_ISSUE_UNROLL = 8


def _cparams(*sem):
    return pltpu.CompilerParams(dimension_semantics=sem, vmem_limit_bytes=VMEM_LIMIT)


def _sigmoid(x):
    return 1.0 / (1.0 + jnp.exp(-x))


def _silu(x):
    return x * _sigmoid(x)


def _softplus(x):
    return jnp.maximum(x, 0.0) + jnp.log(1.0 + jnp.exp(-jnp.abs(x)))


def _layer_norm(u, g, b):
    mu = jnp.mean(u, axis=-1, keepdims=True)
    d = u - mu
    var = jnp.mean(d * d, axis=-1, keepdims=True)
    return d * lax.rsqrt(var + LN_EPS) * g + b


def _dot_nt(a, b):
    return lax.dot_general(a, b, (((1,), (1,)), ((), ())), preferred_element_type=F32)


def _dot_tn(a, b):
    return lax.dot_general(a, b, (((0,), (0,)), ((), ())), preferred_element_type=F32)


def _proj_kernel(x_ref, *refs, n_out, chunk):
    w_refs, o_refs = refs[:n_out], refs[n_out:]
    xb = x_ref[...].astype(BF16)
    for w_ref, o_ref in zip(w_refs, o_refs):
        n = w_ref.shape[1]
        for c0 in range(0, n, chunk):
            c1 = min(n, c0 + chunk)
            o_ref[:, c0:c1] = jnp.dot(xb, w_ref[:, c0:c1], preferred_element_type=F32).astype(o_ref.dtype)


def _proj(x, ws, out_dtypes, name):
    t, k = x.shape
    tm = min(TOKEN_TILE, t)
    in_specs = [pl.BlockSpec((tm, k), lambda i: (i, 0))]
    in_specs += [pl.BlockSpec(w.shape, lambda i: (0, 0)) for w in ws]
    out_specs = [pl.BlockSpec((tm, w.shape[1]), lambda i: (i, 0)) for w in ws]
    out_shape = [jax.ShapeDtypeStruct((t, w.shape[1]), dt) for w, dt in zip(ws, out_dtypes)]
    return pl.pallas_call(
        functools.partial(_proj_kernel, n_out=len(ws), chunk=512),
        grid=(t // tm,),
        in_specs=in_specs,
        out_specs=out_specs,
        out_shape=out_shape,
        compiler_params=_cparams("parallel"),
        name=name,
    )(x, *ws)


def _mamba_kernel(z_ref, x_ref, b_ref, c_ref, dt_ref, sel_ref, ptab_ref,
                  cwx_ref, cwb_ref, cwc_ref, cbx_ref, cbb_ref, cbc_ref,
                  c0x_ref, c0b_ref, c0c_ref, h0_ref, nw_ref,
                  y_ref, cox_ref, cob_ref, coc_ref, hout_ref,
                  sx, sb, sc, hs, *, q, nc, bb, gp):
    c = pl.program_id(2)
    pad = SUBLANES
    hist = SSM_CONV - 1
    gw, ns = GROUP_WIDTH, SSM_D_STATE

    @pl.when(c == 0)
    def _():
        sx[:, pad - hist:pad, :] = c0x_ref[...]
        sb[:, pad - hist:pad, :] = c0b_ref[...]
        sc[:, pad - hist:pad, :] = c0c_ref[...]
        hs[...] = h0_ref[...]

    ti = lax.broadcasted_iota(jnp.int32, (q, q), 0)
    si = lax.broadcasted_iota(jnp.int32, (q, q), 1)
    tri = ti >= si
    tri_f = tri.astype(F32)
    lane = lax.broadcasted_iota(jnp.int32, (1, gw), 1)

    def expand(v):
        r = v.shape[0]
        out = jnp.broadcast_to(v[:, HEADS_PER_GROUP - 1:HEADS_PER_GROUP], (r, gw))
        for j in range(HEADS_PER_GROUP - 2, -1, -1):
            out = jnp.where(lane < SSM_HEAD_DIM * (j + 1), jnp.broadcast_to(v[:, j:j + 1], (r, gw)), out)
        return out

    def conv(cur, s, u, c0, c1, w_ref, bias_ref):
        s[u, pad:pad + q, c0:c1] = cur.astype(F32)
        acc = bias_ref[:, c0:c1]
        for k in range(SSM_CONV):
            acc = acc + w_ref[k:k + 1, c0:c1] * s[u, pad - hist + k:pad - hist + k + q, c0:c1]
        tail = s[u, pad + q - hist:pad + q, c0:c1]
        s[u, pad - hist:pad, c0:c1] = tail
        return _silu(acc), tail

    tails = []
    for u in range(bb):
        for v in range(gp):
            x0, x1 = gw * v, gw * (v + 1)
            n0, n1 = ns * v, ns * (v + 1)
            xc, tail_x = conv(x_ref[u, :, x0:x1], sx, u, x0, x1, cwx_ref, cbx_ref)
            bc, tail_b = conv(b_ref[u, :, n0:n1], sb, u, n0, n1, cwb_ref, cbb_ref)
            cc, tail_c = conv(c_ref[u, :, n0:n1], sc, u, n0, n1, cwc_ref, cbc_ref)

            ptab = ptab_ref[v]
            dt4 = jnp.dot(dt_ref[u], sel_ref[v], precision=HIGHEST, preferred_element_type=F32)
            dt4 = _softplus(dt4 + ptab[0:1, :])
            da = dt4 * (-jnp.exp(ptab[1:2, :]))
            acum = jnp.dot(tri_f, da, precision=HIGHEST, preferred_element_type=F32)
            acum_t = acum.T

            dt_e = expand(dt4)
            ac_e = expand(acum)
            ac_last = acum[q - 1:q, :]
            ac_last_e = expand(ac_last)

            xdt = xc * dt_e
            xdt_b = xdt.astype(BF16)
            bb_ = bc.astype(BF16)
            cb_ = cc.astype(BF16)
            cbm = _dot_nt(cb_, bb_)
            hprev = hs[u, v]
            y = _dot_nt(cb_, hprev.astype(BF16)) * jnp.exp(ac_e)
            zero_b = jnp.zeros_like(xdt_b)
            for j in range(HEADS_PER_GROUP):
                seg = acum[:, j:j + 1] - acum_t[j:j + 1, :]
                lm = jnp.exp(jnp.where(tri, seg, -jnp.inf))
                wj = (cbm * lm).astype(BF16)
                xm = jnp.where((lane >= SSM_HEAD_DIM * j) & (lane < SSM_HEAD_DIM * (j + 1)), xdt_b, zero_b)
                y = y + jnp.dot(wj, xm, preferred_element_type=F32)

            xw = (xdt * jnp.exp(ac_last_e - ac_e)).astype(BF16)
            st = _dot_tn(xw, bb_)
            cd = jnp.exp(ac_last)
            for j in range(HEADS_PER_GROUP):
                r0, r1 = SSM_HEAD_DIM * j, SSM_HEAD_DIM * (j + 1)
                hs[u, v, r0:r1, :] = (jnp.broadcast_to(cd[:, j:j + 1], (SSM_HEAD_DIM, ns)) * hprev[r0:r1, :]
                                      + st[r0:r1, :])

            y = y + expand(ptab[2:3, :]) * xc
            zf = z_ref[u, :, x0:x1].astype(F32)
            y = y * _silu(zf)
            ms = jnp.mean(y * y, axis=-1, keepdims=True)
            y_ref[u, :, x0:x1] = (y * lax.rsqrt(ms + LN_EPS) * nw_ref[:, x0:x1]).astype(y_ref.dtype)
            tails.append((u, x0, x1, n0, n1, tail_x, tail_b, tail_c))

    @pl.when(c == nc - 1)
    def _():
        for u, x0, x1, n0, n1, tail_x, tail_b, tail_c in tails:
            cox_ref[u, :, x0:x1] = tail_x
            cob_ref[u, :, n0:n1] = tail_b
            coc_ref[u, :, n0:n1] = tail_c
        hout_ref[...] = hs[...]


def _mamba_core(zx, dt, conv0, ssm0, sel, ptab, conv_w, conv_b, norm_w, bsz, seqlen):
    q = math.gcd(seqlen, SSM_CHUNK)
    nc = seqlen // q
    g = SSM_GROUPS
    gp = MAMBA_GROUPS_PER_STEP
    bb = 1 if nc > 1 else math.gcd(bsz, MAMBA_SEQS_PER_STEP)
    gw, ns = gp * GROUP_WIDTH, gp * SSM_D_STATE
    zx3 = zx.reshape(bsz, seqlen, zx.shape[-1])
    dt3 = dt.reshape(bsz, seqlen, LANES)
    h0 = ssm0.reshape(bsz, g, GROUP_WIDTH, SSM_D_STATE)
    zb = SSM_D_INNER // gw
    bblk = (2 * SSM_D_INNER) // ns
    cblk = bblk + g // gp
    cwb0 = SSM_D_INNER // ns
    cwc0 = cwb0 + g // gp
    hist = SSM_CONV - 1
    in_specs = [
        pl.BlockSpec((bb, q, gw), lambda b, gi, c: (b, c, gi)),
        pl.BlockSpec((bb, q, gw), lambda b, gi, c: (b, c, zb + gi)),
        pl.BlockSpec((bb, q, ns), lambda b, gi, c: (b, c, bblk + gi)),
        pl.BlockSpec((bb, q, ns), lambda b, gi, c: (b, c, cblk + gi)),
        pl.BlockSpec((bb, q, LANES), lambda b, gi, c: (b, c, 0)),
        pl.BlockSpec((gp, LANES, LANES), lambda b, gi, c: (gi, 0, 0)),
        pl.BlockSpec((gp, SUBLANES, LANES), lambda b, gi, c: (gi, 0, 0)),
        pl.BlockSpec((SSM_CONV, gw), lambda b, gi, c: (0, gi)),
        pl.BlockSpec((SSM_CONV, ns), lambda b, gi, c: (0, cwb0 + gi)),
        pl.BlockSpec((SSM_CONV, ns), lambda b, gi, c: (0, cwc0 + gi)),
        pl.BlockSpec((1, gw), lambda b, gi, c: (0, gi)),
        pl.BlockSpec((1, ns), lambda b, gi, c: (0, cwb0 + gi)),
        pl.BlockSpec((1, ns), lambda b, gi, c: (0, cwc0 + gi)),
        pl.BlockSpec((bb, hist, gw), lambda b, gi, c: (b, 0, gi)),
        pl.BlockSpec((bb, hist, ns), lambda b, gi, c: (b, 0, cwb0 + gi)),
        pl.BlockSpec((bb, hist, ns), lambda b, gi, c: (b, 0, cwc0 + gi)),
        pl.BlockSpec((bb, gp, GROUP_WIDTH, SSM_D_STATE), lambda b, gi, c: (b, gi, 0, 0)),
        pl.BlockSpec((1, gw), lambda b, gi, c: (0, gi)),
    ]
    out_specs = [
        pl.BlockSpec((bb, q, gw), lambda b, gi, c: (b, c, gi)),
        pl.BlockSpec((bb, hist, gw), lambda b, gi, c: (b, 0, gi)),
        pl.BlockSpec((bb, hist, ns), lambda b, gi, c: (b, 0, gi)),
        pl.BlockSpec((bb, hist, ns), lambda b, gi, c: (b, 0, gi)),
        pl.BlockSpec((bb, gp, GROUP_WIDTH, SSM_D_STATE), lambda b, gi, c: (b, gi, 0, 0)),
    ]
    gn = g * SSM_D_STATE
    out_shape = [
        jax.ShapeDtypeStruct((bsz, seqlen, SSM_D_INNER), BF16),
        jax.ShapeDtypeStruct((bsz, hist, SSM_D_INNER), F32),
        jax.ShapeDtypeStruct((bsz, hist, gn), F32),
        jax.ShapeDtypeStruct((bsz, hist, gn), F32),
        jax.ShapeDtypeStruct((bsz, g, GROUP_WIDTH, SSM_D_STATE), F32),
    ]
    scratch = [
        pltpu.VMEM((bb, q + SUBLANES, gw), F32),
        pltpu.VMEM((bb, q + SUBLANES, ns), F32),
        pltpu.VMEM((bb, q + SUBLANES, ns), F32),
        pltpu.VMEM((bb, gp, GROUP_WIDTH, SSM_D_STATE), F32),
    ]
    y, cox, cob, coc, hout = pl.pallas_call(
        functools.partial(_mamba_kernel, q=q, nc=nc, bb=bb, gp=gp),
        grid=(bsz // bb, g // gp, nc),
        in_specs=in_specs,
        out_specs=out_specs,
        out_shape=out_shape,
        scratch_shapes=scratch,
        compiler_params=_cparams("parallel", "parallel", "arbitrary"),
        name="mamba_core",
    )(zx3, zx3, zx3, zx3, dt3, sel, ptab, conv_w, conv_w, conv_w, conv_b, conv_b, conv_b,
      conv0, conv0, conv0, h0, norm_w)
    new_conv = jnp.concatenate([cox, cob, coc], axis=-1)
    return (y.reshape(bsz * seqlen, SSM_D_INNER), new_conv,
            hout.reshape(bsz, SSM_HEADS, SSM_HEAD_DIM, SSM_D_STATE))


def _post_mixer_kernel(y_ref, w_ref, xres_ref, g_ref, b_ref, wr_ref, br_ref,
                       x1_ref, route_ref, cnt_ref, base, *, tm):
    i = pl.program_id(0)

    @pl.when(i == 0)
    def _():
        base[...] = jnp.zeros_like(base)

    h = jnp.dot(y_ref[...], w_ref[...], preferred_element_type=F32)
    x1 = _layer_norm(DN_ALPHA * xres_ref[...] + h, g_ref[...], b_ref[...])
    x1_ref[...] = x1

    logits = jnp.dot(x1, wr_ref[...], precision=HIGHEST, preferred_element_type=F32) + br_ref[...]
    lane = lax.broadcasted_iota(jnp.int32, (tm, LANES), 1)
    neg = -jnp.inf
    gmask = lane < MOE_GROUPS
    gl = jnp.where(gmask, logits, neg)
    mg = jnp.max(gl, axis=1, keepdims=True)
    grp = jnp.min(jnp.where(gl == mg, lane, LANES), axis=1, keepdims=True)
    pg = 1.0 / jnp.sum(jnp.where(gmask, jnp.exp(logits - mg), 0.0), axis=1, keepdims=True)
    lo = MOE_GROUPS + MOE_EXPERTS_PER_GROUP * grp
    el = jnp.where((lane >= lo) & (lane < lo + MOE_EXPERTS_PER_GROUP), logits, neg)
    v1 = jnp.max(el, axis=1, keepdims=True)
    i1 = jnp.min(jnp.where(el == v1, lane, LANES), axis=1, keepdims=True)
    el2 = jnp.where(lane == i1, neg, el)
    v2 = jnp.max(el2, axis=1, keepdims=True)
    i2 = jnp.min(jnp.where(el2 == v2, lane, LANES), axis=1, keepdims=True)
    t = jnp.exp(v2 - v1)
    wa = pg / (1.0 + t)
    wb = pg * t / (1.0 + t)
    e1 = i1 - MOE_GROUPS
    e2 = i2 - MOE_GROUPS

    oh = ((lane == e1) | (lane == e2)).astype(BF16)
    ri = lax.broadcasted_iota(jnp.int32, (tm, tm), 0)
    ci = lax.broadcasted_iota(jnp.int32, (tm, tm), 1)
    before = jnp.dot((ri > ci).astype(BF16), oh, preferred_element_type=F32) + base[...]
    r1 = jnp.sum(jnp.where(lane == e1, before, 0.0), axis=1, keepdims=True)
    r2 = jnp.sum(jnp.where(lane == e2, before, 0.0), axis=1, keepdims=True)
    new_base = base[...] + jnp.sum(oh.astype(F32), axis=0, keepdims=True)
    base[...] = new_base
    cnt_ref[...] = new_base

    route = jnp.where(lane == 0, e1.astype(F32), 0.0)
    route = jnp.where(lane == 1, e2.astype(F32), route)
    route = jnp.where(lane == 2, r1, route)
    route = jnp.where(lane == 3, r2, route)
    route = jnp.where(lane == 4, wa, route)
    route = jnp.where(lane == 5, wb, route)
    route_ref[...] = route


def _post_mixer(y, w, xres, ln_g, ln_b, wr, br, name):
    t, kin = y.shape
    tm = min(TOKEN_TILE, t)
    return pl.pallas_call(
        functools.partial(_post_mixer_kernel, tm=tm),
        grid=(t // tm,),
        in_specs=[
            pl.BlockSpec((tm, kin), lambda i: (i, 0)),
            pl.BlockSpec((kin, D_MODEL), lambda i: (0, 0)),
            pl.BlockSpec((tm, D_MODEL), lambda i: (i, 0)),
            pl.BlockSpec((1, D_MODEL), lambda i: (0, 0)),
            pl.BlockSpec((1, D_MODEL), lambda i: (0, 0)),
            pl.BlockSpec((D_MODEL, LANES), lambda i: (0, 0)),
            pl.BlockSpec((1, LANES), lambda i: (0, 0)),
        ],
        out_specs=[
            pl.BlockSpec((tm, D_MODEL), lambda i: (i, 0)),
            pl.BlockSpec((tm, LANES), lambda i: (i, 0)),
            pl.BlockSpec((1, LANES), lambda i: (0, 0)),
        ],
        out_shape=[
            jax.ShapeDtypeStruct((t, D_MODEL), F32),
            jax.ShapeDtypeStruct((t, LANES), F32),
            jax.ShapeDtypeStruct((1, LANES), F32),
        ],
        scratch_shapes=[pltpu.VMEM((1, LANES), F32)],
        compiler_params=_cparams("arbitrary"),
        name=name,
    )(y, w, xres, ln_g, ln_b, wr, br)


def _row_copy(src_ref, src_row, dst_ref, dst_row, sem):
    return pltpu.make_async_copy(src_ref.at[pl.ds(src_row, 1)], dst_ref.at[pl.ds(dst_row, 1)], sem)


def _dispatch_kernel(dest_ref, x_ref, xs_ref, sem, *, tm):
    def start(r, carry):
        _row_copy(x_ref, r, xs_ref, dest_ref[0, 0, 2 * r], sem).start()
        _row_copy(x_ref, r, xs_ref, dest_ref[0, 0, 2 * r + 1], sem).start(priority=1)
        return carry

    lax.fori_loop(0, tm, start, 0, unroll=DMA_ISSUE_UNROLL)
    for _ in range(2):
        pltpu.make_async_copy(x_ref, xs_ref.at[pl.ds(0, tm)], sem).wait()


def _dispatch(x1, dest3, tm):
    t = x1.shape[0]
    return pl.pallas_call(
        functools.partial(_dispatch_kernel, tm=tm),
        grid=(t // tm,),
        in_specs=[
            pl.BlockSpec((1, 1, 2 * tm), lambda i: (i, 0, 0), memory_space=pltpu.SMEM),
            pl.BlockSpec((tm, D_MODEL), lambda i: (i, 0)),
        ],
        out_specs=pl.BlockSpec(memory_space=pl.ANY),
        out_shape=jax.ShapeDtypeStruct((2 * t, D_MODEL), F32),
        scratch_shapes=[pltpu.SemaphoreType.DMA(())],
        compiler_params=_cparams("arbitrary"),
        name="moe_dispatch",
    )(dest3, x1)


def _ffn_kernel(wt_ref, we_ref, wlo_ref, whi_ref, wfirst_ref,
                xs_ref, w1_ref, w3_ref, w2_ref, ys_ref, *, tf):
    w = pl.program_id(0)
    lo = wlo_ref[w]
    hi = whi_ref[w]

    @pl.when(wfirst_ref[w] == 1)
    def _():
        ys_ref[...] = jnp.zeros_like(ys_ref)

    @pl.when(hi > lo)
    def _():
        rows = wt_ref[w] * tf + lax.broadcasted_iota(jnp.int32, (tf, 1), 0)
        mask = (rows >= lo) & (rows < hi)
        xb = xs_ref[...].astype(BF16)
        a = jnp.dot(xb, w1_ref[0].astype(BF16), preferred_element_type=F32)
        b = jnp.dot(xb, w3_ref[0].astype(BF16), preferred_element_type=F32)
        hmid = (_silu(a) * b).astype(BF16)
        y = jnp.dot(hmid, w2_ref[0].astype(BF16), preferred_element_type=F32)
        ys_ref[...] = jnp.where(mask, y, ys_ref[...])


def _ffn(xs, w1, w3, w2, work, tf):
    n = xs.shape[0]
    n_work = work[0].shape[0]
    grid_spec = pltpu.PrefetchScalarGridSpec(
        num_scalar_prefetch=5,
        grid=(n_work,),
        in_specs=[
            pl.BlockSpec((tf, D_MODEL), lambda w, wt, we, wlo, whi, wf: (wt[w], 0)),
            pl.BlockSpec((1, D_MODEL, MOE_FF), lambda w, wt, we, wlo, whi, wf: (we[w], 0, 0)),
            pl.BlockSpec((1, D_MODEL, MOE_FF), lambda w, wt, we, wlo, whi, wf: (we[w], 0, 0)),
            pl.BlockSpec((1, MOE_FF, D_MODEL), lambda w, wt, we, wlo, whi, wf: (we[w], 0, 0)),
        ],
        out_specs=pl.BlockSpec((tf, D_MODEL), lambda w, wt, we, wlo, whi, wf: (wt[w], 0)),
    )
    return pl.pallas_call(
        functools.partial(_ffn_kernel, tf=tf),
        grid_spec=grid_spec,
        out_shape=jax.ShapeDtypeStruct((n, D_MODEL), F32),
        compiler_params=_cparams("arbitrary"),
        name="moe_ffn",
    )(*work, xs, w1, w3, w2)


def _combine_kernel(dest_ref, x1_ref, route_ref, g_ref, b_ref, ys_ref, o_ref, buf0, buf1, sem, *, tm):
    def start(r, carry):
        _row_copy(ys_ref, dest_ref[0, 0, 2 * r], buf0, r, sem).start()
        _row_copy(ys_ref, dest_ref[0, 0, 2 * r + 1], buf1, r, sem).start(priority=1)
        return carry

    lax.fori_loop(0, tm, start, 0, unroll=DMA_ISSUE_UNROLL)
    pltpu.make_async_copy(ys_ref.at[pl.ds(0, tm)], buf0, sem).wait()
    pltpu.make_async_copy(ys_ref.at[pl.ds(0, tm)], buf1, sem).wait()

    route = route_ref[...]
    y = route[:, 4:5] * buf0[...] + route[:, 5:6] * buf1[...]
    o_ref[...] = _layer_norm(DN_ALPHA * x1_ref[...] + y, g_ref[...], b_ref[...])


def _combine(x1, route, ys, dest3, ln_g, ln_b, tm, name):
    t = x1.shape[0]
    return pl.pallas_call(
        functools.partial(_combine_kernel, tm=tm),
        grid=(t // tm,),
        in_specs=[
            pl.BlockSpec((1, 1, 2 * tm), lambda i: (i, 0, 0), memory_space=pltpu.SMEM),
            pl.BlockSpec((tm, D_MODEL), lambda i: (i, 0)),
            pl.BlockSpec((tm, LANES), lambda i: (i, 0)),
            pl.BlockSpec((1, D_MODEL), lambda i: (0, 0)),
            pl.BlockSpec((1, D_MODEL), lambda i: (0, 0)),
            pl.BlockSpec(memory_space=pl.ANY),
        ],
        out_specs=pl.BlockSpec((tm, D_MODEL), lambda i: (i, 0)),
        out_shape=jax.ShapeDtypeStruct((t, D_MODEL), F32),
        scratch_shapes=[
            pltpu.VMEM((tm, D_MODEL), F32),
            pltpu.VMEM((tm, D_MODEL), F32),
            pltpu.SemaphoreType.DMA(()),
        ],
        compiler_params=_cparams("arbitrary"),
        name=name,
    )(dest3, x1, route, ln_g, ln_b, ys)


def _moe(x1, route, counts, w1, w3, w2, ln_g, ln_b, name):
    t = x1.shape[0]
    n = 2 * t
    tm = min(TOKEN_TILE, t)
    tf = min(FFN_TILE, n)
    cnt = counts[0, :MOE_EXPERTS].astype(jnp.int32)
    offs = jnp.concatenate([jnp.zeros((1,), jnp.int32), jnp.cumsum(cnt)])
    e = route[:, 0:2].astype(jnp.int32)
    r = route[:, 2:4].astype(jnp.int32)
    dest = offs[e] + r
    dest3 = dest.reshape(t // tm, 1, 2 * tm)

    n_tiles = n // tf
    n_work = n_tiles + MOE_EXPERTS
    first_tile = offs[:-1] // tf
    last_tile = (offs[1:] - 1) // tf
    ntile_e = jnp.where(cnt > 0, last_tile - first_tile + 1, 0)
    wstart = jnp.concatenate([jnp.zeros((1,), jnp.int32), jnp.cumsum(ntile_e)])
    total = wstart[-1]
    widx = jnp.arange(n_work, dtype=jnp.int32)
    we = jnp.clip(jnp.searchsorted(wstart, widx, side="right") - 1, 0, MOE_EXPERTS - 1).astype(jnp.int32)
    wt = first_tile[we] + (widx - wstart[we])
    valid = widx < total
    wt = jnp.where(valid, wt, n_tiles - 1).astype(jnp.int32)
    wlo = jnp.where(valid, offs[we], 0).astype(jnp.int32)
    whi = jnp.where(valid, offs[we + 1], 0).astype(jnp.int32)
    last_e = we[jnp.maximum(total - 1, 0)]
    we = jnp.where(valid, we, last_e).astype(jnp.int32)
    wfirst = jnp.concatenate([jnp.ones((1,), jnp.int32), (wt[1:] != wt[:-1]).astype(jnp.int32)])

    xs = _dispatch(x1, dest3, tm)
    ys = _ffn(xs, w1, w3, w2, (wt, we, wlo, whi, wfirst), tf)
    return _combine(x1, route, ys, dest3, ln_g, ln_b, tm, name)


def _bias_kernel(idx_ref, rbt_ref, o_ref, *, width):
    idx = idx_ref[...]
    bucket = lax.broadcasted_iota(jnp.int32, (NUM_BUCKETS, width), 0)
    onehot = (bucket == idx).astype(F32)
    vals = jnp.dot(rbt_ref[...], onehot, precision=HIGHEST, preferred_element_type=F32)
    o_ref[...] = jnp.where(idx < 0, -jnp.inf, vals)


def _bias_tiles(rel_bias, idx_np):
    n = idx_np.shape[0]
    width = 8192
    assert n % width == 0
    nh = rel_bias.shape[1]
    return pl.pallas_call(
        functools.partial(_bias_kernel, width=width),
        grid=(n // width,),
        in_specs=[
            pl.BlockSpec((1, width), lambda i: (0, i)),
            pl.BlockSpec((nh, NUM_BUCKETS), lambda i: (0, 0)),
        ],
        out_specs=pl.BlockSpec((nh, width), lambda i: (0, i)),
        out_shape=jax.ShapeDtypeStruct((nh, n), F32),
        compiler_params=_cparams("parallel"),
        name="rel_bias_tiles",
    )(jnp.asarray(idx_np.reshape(1, n)), rel_bias.T)


def _bucket_of_distance(dist):
    n = np.maximum(dist, 0)
    max_exact = NUM_BUCKETS // 2
    nf = np.maximum(n, 1).astype(np.float32)
    large = max_exact + (np.log(nf / np.float32(max_exact)) / np.float32(math.log(MAX_DISTANCE / max_exact))
                         * np.float32(NUM_BUCKETS - max_exact)).astype(np.int32)
    large = np.minimum(large, NUM_BUCKETS - 1)
    return np.where(n < max_exact, n, large).astype(np.int32)


def _bucket_tile(dist):
    return np.where(dist >= 0, _bucket_of_distance(dist), -1).astype(np.int32)


def _lambda_value(lam_ref, layer_idx):
    lv = lam_ref[...]
    s1 = jnp.sum(lv[0:1, :] * lv[1:2, :], axis=1, keepdims=True)
    s2 = jnp.sum(lv[2:3, :] * lv[3:4, :], axis=1, keepdims=True)
    lam_init = 0.8 - 0.6 * math.exp(-0.3 * layer_idx)
    return jnp.exp(s1) - jnp.exp(s2) + lam_init, lam_init


def _attn_prompt_kernel(q_ref, k_ref, v_ref, bias_ref, lam_ref, sw_ref, o_ref, *, tq, layer_idx):
    qi = pl.program_id(2)
    qb = q_ref[0]
    lane = lax.broadcasted_iota(jnp.int32, (tq, 2 * ATT_HEAD_DIM), 1)
    zero = jnp.zeros_like(qb)
    q2 = jnp.concatenate([jnp.where(lane < ATT_HEAD_DIM, qb, zero),
                          jnp.where(lane >= ATT_HEAD_DIM, qb, zero)], axis=0)

    def step(j, carry):
        m, l, acc = carry
        k0 = pl.multiple_of(j * tq, tq)
        kj = k_ref[0, pl.ds(k0, tq), :].astype(BF16)
        vj = v_ref[0, pl.ds(k0, tq), :].astype(BF16)
        s = _dot_nt(q2, kj) + bias_ref[0, jnp.minimum(qi - j, 2)]
        m_new = jnp.maximum(m, jnp.max(s, axis=1, keepdims=True))
        alpha = jnp.exp(m - m_new)
        p = jnp.exp(s - m_new)
        l = alpha * l + jnp.sum(p, axis=1, keepdims=True)
        acc = alpha * acc + jnp.dot(p.astype(BF16), vj, preferred_element_type=F32)
        return m_new, l, acc

    m0 = jnp.full((2 * tq, 1), -jnp.inf, F32)
    l0 = jnp.zeros((2 * tq, 1), F32)
    a0 = jnp.zeros((2 * tq, 2 * ATT_HEAD_DIM), F32)
    m, l, acc = lax.fori_loop(0, qi + 1, step, (m0, l0, a0))
    o = acc / l
    lam, lam_init = _lambda_value(lam_ref, layer_idx)
    d = o[:tq] - lam * o[tq:]
    ms = jnp.mean(d * d, axis=-1, keepdims=True)
    o_ref[0] = (d * lax.rsqrt(ms + LN_EPS) * sw_ref[...] * (1.0 - lam_init)).astype(o_ref.dtype)


def _attn_prompt(q, kv, bias, lam_vec, subln_w, bsz, seqlen, layer_idx):
    tq = min(ATT_TILE, seqlen)
    hd2 = 2 * ATT_HEAD_DIM
    q3 = q.reshape(bsz, seqlen, ATT_QK_DIM)
    kv3 = kv.reshape(bsz, seqlen, ATT_QK_DIM + ATT_V_DIM)
    vblk = ATT_QK_DIM // hd2
    out = pl.pallas_call(
        functools.partial(_attn_prompt_kernel, tq=tq, layer_idx=layer_idx),
        grid=(bsz, ATT_HEADS, seqlen // tq),
        in_specs=[
            pl.BlockSpec((1, tq, hd2), lambda b, h, i: (b, i, h)),
            pl.BlockSpec((1, seqlen, hd2), lambda b, h, i: (b, 0, h)),
            pl.BlockSpec((1, seqlen, hd2), lambda b, h, i: (b, 0, vblk + h)),
            pl.BlockSpec((1, 3, 2 * tq, tq), lambda b, h, i: (h, 0, 0, 0)),
            pl.BlockSpec(lam_vec.shape, lambda b, h, i: (0, 0)),
            pl.BlockSpec((1, hd2), lambda b, h, i: (0, 0)),
        ],
        out_specs=pl.BlockSpec((1, tq, hd2), lambda b, h, i: (b, i, h)),
        out_shape=jax.ShapeDtypeStruct((bsz, seqlen, ATT_V_DIM), BF16),
        compiler_params=_cparams("parallel", "parallel", "arbitrary"),
        name="attn_prompt",
    )(q3, kv3, kv3, bias, lam_vec, subln_w)
    return out.reshape(bsz * seqlen, ATT_V_DIM)


def _attn_sample_kernel(pt_ref, q_ref, *refs, nq, n_steps, pp, layer_idx):
    kt_refs = refs[:pp]
    vm_refs = refs[pp:2 * pp]
    (kn_ref, vn_ref, bias_ref, exp_ref, pmask_ref, lam_ref, sw_ref,
     o_ref, qbd, m_s, l_s, acc_s, kpad, vpad) = refs[2 * pp:]
    j = pl.program_id(1)
    maps = 2 * ATT_HEADS
    rows = maps * nq
    hd2 = 2 * ATT_HEAD_DIM
    page = PAGE_SIZE

    @pl.when(j == 0)
    def _():
        m_s[...] = jnp.full_like(m_s, -jnp.inf)
        l_s[...] = jnp.zeros_like(l_s)
        acc_s[...] = jnp.zeros_like(acc_s)
        qt = jnp.concatenate([q_ref[0].astype(F32)] * maps, axis=0)
        ri = lax.broadcasted_iota(jnp.int32, (rows, ATT_QK_DIM), 0)
        ci = lax.broadcasted_iota(jnp.int32, (rows, ATT_QK_DIM), 1)
        keep = (ri // nq) == (ci // ATT_HEAD_DIM)
        qbd[...] = jnp.where(keep, qt, 0.0).astype(qbd.dtype)

    def softmax_update(s):
        m = m_s[...]
        m_new = jnp.maximum(m, jnp.max(s, axis=1, keepdims=True))
        alpha = jnp.exp(m - m_new)
        p = jnp.exp(s - m_new)
        l_s[...] = alpha * l_s[...] + jnp.sum(p, axis=1, keepdims=True)
        m_s[...] = m_new
        return alpha, p.astype(BF16)

    @pl.when(j < n_steps)
    def _():
        q2 = qbd[...]
        s_parts = []
        for u in range(pp):
            kt = kt_refs[u][0].reshape(ATT_QK_DIM, page).astype(BF16)
            s_u = jnp.dot(q2, kt, preferred_element_type=F32)
            if u == pp - 1:
                s_u = s_u + bias_ref[jnp.where(j == n_steps - 1, 0, 2)]
            else:
                s_u = s_u + bias_ref[2]
            s_parts.append(s_u)
        alpha, pb = softmax_update(jnp.concatenate(s_parts, axis=1))
        pv = jnp.zeros((rows, hd2), F32)
        for u in range(pp):
            pe = jnp.dot(pb[:, page * u:page * (u + 1)], exp_ref[...], preferred_element_type=F32)
            pe = pe.astype(BF16) * pmask_ref[...]
            vm = vm_refs[u][0].reshape(page * ATT_HEADS, hd2).astype(BF16)
            pv = pv + jnp.dot(pe, vm, preferred_element_type=F32)
        acc_s[...] = alpha * acc_s[...] + pv

    @pl.when(j == n_steps)
    def _():
        kpad[...] = jnp.zeros_like(kpad)
        vpad[...] = jnp.zeros_like(vpad)
        kpad[0:nq, :] = kn_ref[0]
        vpad[0:nq, :] = vn_ref[0]
        s = _dot_nt(qbd[...], kpad[...].astype(BF16)) + bias_ref[1]
        alpha, pb = softmax_update(s)
        for h in range(ATT_HEADS):
            r0, r1 = 2 * nq * h, 2 * nq * (h + 1)
            pvh = jnp.dot(pb[r0:r1, :], vpad[:, hd2 * h:hd2 * (h + 1)].astype(BF16), preferred_element_type=F32)
            acc_s[r0:r1, :] = alpha[r0:r1, :] * acc_s[r0:r1, :] + pvh
        o = acc_s[...] / l_s[...]
        lam, lam_init = _lambda_value(lam_ref, layer_idx)
        for h in range(ATT_HEADS):
            r0 = 2 * nq * h
            d = o[r0:r0 + nq, :] - lam * o[r0 + nq:r0 + 2 * nq, :]
            ms = jnp.mean(d * d, axis=-1, keepdims=True)
            o_ref[0, :, hd2 * h:hd2 * (h + 1)] = (
                d * lax.rsqrt(ms + LN_EPS) * sw_ref[...] * (1.0 - lam_init)).astype(o_ref.dtype)


def _attn_sample(q, kv, cache_k, cache_v, page_table, bias, lam_vec, subln_w, bsz, nq, layer_idx):
    n_pages = page_table.shape[1]
    page = cache_k.shape[1]
    maps = 2 * ATT_HEADS
    rows = maps * nq
    hd2 = 2 * ATT_HEAD_DIM
    pp = math.gcd(n_pages, SAMPLE_PAGES_PER_STEP)
    n_steps = n_pages // pp
    q3 = q.reshape(bsz, nq, ATT_QK_DIM)
    kv3 = kv.reshape(bsz, nq, ATT_QK_DIM + ATT_V_DIM)
    cache_kt = jnp.transpose(cache_k, (0, 2, 3, 1))
    kk = np.arange(page)[:, None]
    cc = np.arange(page * ATT_HEADS)[None, :]
    expand = jnp.asarray((cc // ATT_HEADS == kk).astype(np.float32), dtype=BF16)
    rr = np.arange(rows)[:, None]
    pmask = jnp.asarray(((rr // (2 * nq)) == (cc % ATT_HEADS)).astype(np.float32), dtype=BF16)

    def page_map(u):
        return lambda b, j, pt: (pt[b, jnp.minimum(j * pp + u, n_pages - 1)], 0, 0, 0)

    in_specs = [pl.BlockSpec((1, nq, ATT_QK_DIM), lambda b, j, pt: (b, 0, 0))]
    in_specs += [pl.BlockSpec((1, maps, ATT_HEAD_DIM, page), page_map(u)) for u in range(pp)]
    in_specs += [pl.BlockSpec((1, page, ATT_HEADS, hd2), page_map(u)) for u in range(pp)]
    in_specs += [
        pl.BlockSpec((1, nq, ATT_QK_DIM), lambda b, j, pt: (b, 0, 0)),
        pl.BlockSpec((1, nq, ATT_V_DIM), lambda b, j, pt: (b, 0, 1)),
        pl.BlockSpec((3, rows, page), lambda b, j, pt: (0, 0, 0)),
        pl.BlockSpec((page, page * ATT_HEADS), lambda b, j, pt: (0, 0)),
        pl.BlockSpec((rows, page * ATT_HEADS), lambda b, j, pt: (0, 0)),
        pl.BlockSpec(lam_vec.shape, lambda b, j, pt: (0, 0)),
        pl.BlockSpec((1, hd2), lambda b, j, pt: (0, 0)),
    ]
    grid_spec = pltpu.PrefetchScalarGridSpec(
        num_scalar_prefetch=1,
        grid=(bsz, n_steps + 1),
        in_specs=in_specs,
        out_specs=pl.BlockSpec((1, nq, ATT_V_DIM), lambda b, j, pt: (b, 0, 0)),
        scratch_shapes=[
            pltpu.VMEM((rows, ATT_QK_DIM), BF16),
            pltpu.VMEM((rows, 1), F32),
            pltpu.VMEM((rows, 1), F32),
            pltpu.VMEM((rows, hd2), F32),
            pltpu.VMEM((page, ATT_QK_DIM), F32),
            pltpu.VMEM((page, ATT_V_DIM), F32),
        ],
    )
    out = pl.pallas_call(
        functools.partial(_attn_sample_kernel, nq=nq, n_steps=n_steps, pp=pp, layer_idx=layer_idx),
        grid_spec=grid_spec,
        out_shape=jax.ShapeDtypeStruct((bsz, nq, ATT_V_DIM), BF16),
        compiler_params=_cparams("parallel", "arbitrary"),
        name="attn_sample",
    )(page_table, q3, *([cache_kt] * pp), *([cache_v] * pp), kv3, kv3, bias, expand, pmask, lam_vec, subln_w)
    return out.reshape(bsz * nq, ATT_V_DIM)


def _router_params(wg, bg, we, be):
    wexp = jnp.transpose(we, (1, 0, 2)).reshape(D_MODEL, MOE_EXPERTS)
    wr = jnp.concatenate([wg, wexp], axis=1)
    wr = jnp.pad(wr, ((0, 0), (0, LANES - wr.shape[1])))
    br = jnp.concatenate([bg, be.reshape(MOE_EXPERTS)])
    br = jnp.pad(br, (0, LANES - br.shape[0])).reshape(1, LANES)
    return wr, br


def _head_tables(dt_bias, a_log, d_skip):
    def per_group(v):
        return jnp.pad(v.reshape(SSM_GROUPS, HEADS_PER_GROUP), ((0, 0), (0, LANES - HEADS_PER_GROUP)))

    rows = jnp.stack([per_group(dt_bias), per_group(a_log), per_group(d_skip)], axis=1)
    ptab = jnp.pad(rows, ((0, 0), (0, SUBLANES - 3), (0, 0)))
    sel = np.zeros((SSM_GROUPS, LANES, LANES), np.float32)
    for g in range(SSM_GROUPS):
        for j in range(HEADS_PER_GROUP):
            sel[g, HEADS_PER_GROUP * g + j, j] = 1.0
    return ptab, jnp.asarray(sel)


def _prompt_bias_index(tq):
    r = np.arange(tq)[:, None]
    c = np.arange(tq)[None, :]
    tiles = [_bucket_tile(delta * tq + r - c) for delta in range(3)]
    return np.stack(tiles).reshape(-1)


def _sample_bias_index(nq, page, past_len):
    r = np.arange(nq)[:, None]
    c = np.arange(page)[None, :]
    last_page = _bucket_tile(past_len + r - (past_len - page) - c)
    own = np.where(c < nq, _bucket_tile(r - c), -1).astype(np.int32)
    far = _bucket_tile(np.full((nq, page), MAX_DISTANCE + page))
    return np.stack([last_page, own, far]).reshape(-1)


def kernel(x_prompt, x_sample, state_ssm, state_conv, cache_k, cache_v, page_table, ln_g, ln_b, m_w_in,
           m_conv_w, m_conv_b, m_dt_bias, m_a_log, m_d, m_norm_w, m_w_out, kv_w, a_w_q, a_lambda,
           a_subln_w, a_w_o, rel_bias, moe_wg, moe_bg, moe_we, moe_be, moe_w1, moe_w3, moe_w2):
    bp, lp, _ = x_prompt.shape
    bs, ls, _ = x_sample.shape
    n_pages = page_table.shape[1]
    page = cache_k.shape[1]
    past_len = n_pages * page
    assert page == PAGE_SIZE and page >= MAX_DISTANCE and min(ATT_TILE, lp) >= MAX_DISTANCE

    w_in = m_w_in[0]
    w_zx = w_in[:, :SSM_D_INNER + SSM_CONV_DIM].astype(BF16)
    w_dt = jnp.pad(w_in[:, SSM_D_INNER + SSM_CONV_DIM:], ((0, 0), (0, LANES - SSM_HEADS))).astype(BF16)
    ptab, sel = _head_tables(m_dt_bias[0], m_a_log[0], m_d[0])
    conv_w = m_conv_w[0]
    conv_b = m_conv_b[0].reshape(1, SSM_CONV_DIM)
    norm_w = m_norm_w[0].reshape(1, SSM_D_INNER)
    w_out = m_w_out[0].astype(BF16)
    w_kvq = [kv_w.astype(BF16), (a_w_q[0] * (ATT_HEAD_DIM ** -0.5)).astype(BF16)]
    w_o = a_w_o[0].astype(BF16)
    lam_vec = a_lambda[0]
    subln_w = a_subln_w[0].reshape(1, 2 * ATT_HEAD_DIM)
    routers = [_router_params(moe_wg[l], moe_bg[l], moe_we[l], moe_be[l]) for l in range(DEPTH)]
    experts = [(moe_w1[l].reshape(MOE_EXPERTS, D_MODEL, MOE_FF),
                moe_w3[l].reshape(MOE_EXPERTS, D_MODEL, MOE_FF),
                moe_w2[l].reshape(MOE_EXPERTS, MOE_FF, D_MODEL)) for l in range(DEPTH)]
    lng = ln_g.reshape(DEPTH, 2, 1, D_MODEL)
    lnb = ln_b.reshape(DEPTH, 2, 1, D_MODEL)

    tq = min(ATT_TILE, lp)
    maps = 2 * ATT_HEADS
    idx_p = _prompt_bias_index(tq)
    idx_s = _sample_bias_index(ls, page, past_len)
    n_p = idx_p.shape[0]
    n_s = idx_s.shape[0]
    n_tot = -(-(n_p + n_s) // 8192) * 8192
    idx_all = np.concatenate([idx_p, idx_s, np.zeros((n_tot - n_p - n_s,), np.int32)])
    tiles = _bias_tiles(rel_bias, idx_all)
    bias_p = tiles[:, :n_p].reshape(ATT_HEADS, 2, 3, tq, tq)
    bias_p = jnp.transpose(bias_p, (0, 2, 1, 3, 4)).reshape(ATT_HEADS, 3, 2 * tq, tq)
    bias_s = tiles[:, n_p:n_p + n_s].reshape(maps, 3, ls, page)
    bias_s = jnp.transpose(bias_s, (1, 0, 2, 3)).reshape(3, maps * ls, page)

    def trunk(x, ssm0, conv0, bsz, seqlen, past):
        t = bsz * seqlen
        xf = x.reshape(t, D_MODEL)
        zx, dt = _proj(xf, [w_zx, w_dt], [BF16, F32], "in_proj")
        y, new_conv, new_ssm = _mamba_core(zx, dt, conv0, ssm0, sel, ptab, conv_w, conv_b, norm_w, bsz, seqlen)
        x1, route, counts = _post_mixer(y, w_out, xf, lng[0, 0], lnb[0, 0], *routers[0], "mamba_out_ln_router")
        x2 = _moe(x1, route, counts, *experts[0], lng[0, 1], lnb[0, 1], "moe0_combine_ln")
        kv, qs = _proj(x2, w_kvq, [F32, BF16], "kvq_proj")
        k_new = kv[:, :ATT_QK_DIM].reshape(bsz, seqlen, maps, ATT_HEAD_DIM)
        v_new = kv[:, ATT_QK_DIM:].reshape(bsz, seqlen, ATT_HEADS, 2 * ATT_HEAD_DIM)
        if past is None:
            o = _attn_prompt(qs, kv, bias_p, lam_vec, subln_w, bsz, seqlen, DEPTH - 1)
        else:
            o = _attn_sample(qs, kv, cache_k, cache_v, page_table, bias_s, lam_vec, subln_w, bsz, seqlen, DEPTH - 1)
        x3, route, counts = _post_mixer(o, w_o, x2, lng[1, 0], lnb[1, 0], *routers[1], "attn_out_ln_router")
        x4 = _moe(x3, route, counts, *experts[1], lng[1, 1], lnb[1, 1], "moe1_combine_ln")
        return (x4.reshape(bsz, seqlen, D_MODEL), new_ssm[None], new_conv[None], k_new, v_new)

    ssm0_p = jnp.zeros((bp, SSM_HEADS, SSM_HEAD_DIM, SSM_D_STATE), F32)
    conv0_p = jnp.zeros((bp, SSM_CONV - 1, SSM_CONV_DIM), F32)
    y_p, ssm_p, conv_p, k_p, v_p = trunk(x_prompt, ssm0_p, conv0_p, bp, lp, None)
    y_s, ssm_s, conv_s, k_s, v_s = trunk(x_sample, state_ssm[0], state_conv[0], bs, ls, True)
    return (y_p, y_s, ssm_p, conv_p, k_p, v_p, ssm_s, conv_s, k_s, v_s)
```

```python
import functools
import math

import numpy as np
import jax
import jax.numpy as jnp
from jax import lax
from jax.experimental import pallas as pl
from jax.experimental.pallas import tpu as pltpu

F32 = jnp.float32
BF16 = jnp.bfloat16
HIGHEST = lax.Precision.HIGHEST

D_MODEL = 1024
DEPTH = 2
DN_ALPHA = (2.0 * DEPTH) ** 0.25
LN_EPS = 1e-5
LOG2E = math.log2(math.e)
SSM_D_INNER = 2048
SSM_HEAD_DIM = 64
SSM_HEADS = 32
SSM_GROUPS = 8
SSM_D_STATE = 128
SSM_CONV = 4
SSM_CHUNK = 128
SSM_CONV_DIM = SSM_D_INNER + 2 * SSM_GROUPS * SSM_D_STATE
HEADS_PER_GROUP = SSM_HEADS // SSM_GROUPS
GROUP_WIDTH = HEADS_PER_GROUP * SSM_HEAD_DIM
ATT_HEADS = 8
ATT_HEAD_DIM = 64
ATT_QK_DIM = 1024
ATT_V_DIM = 1024
NUM_BUCKETS = 32
MAX_DISTANCE = 128
MOE_GROUPS = 4
MOE_EXPERTS_PER_GROUP = 8
MOE_EXPERTS = MOE_GROUPS * MOE_EXPERTS_PER_GROUP
MOE_FF = 512
PAGE_SIZE = 128

LANES = 128
SUBLANES = 8
VMEM_LIMIT = 56 * 1024 * 1024

TOKEN_TILE = 256
FFN_TILE = 256
ATT_TILE = 256
MAMBA_GROUPS_PER_STEP = 4
MAMBA_SEQS_PER_STEP = 2
SAMPLE_PAGES_PER_STEP = 8


def _cparams(*sem):
    return pltpu.CompilerParams(dimension_semantics=sem, vmem_limit_bytes=VMEM_LIMIT)


def _sigmoid(x):
    return 1.0 / (1.0 + jnp.exp(-x))


def _silu(x):
    return x * _sigmoid(x)


def _softplus(x):
    return jnp.maximum(x, 0.0) + jnp.log(1.0 + jnp.exp(-jnp.abs(x)))


def _layer_norm(u, g, b):
    mu = jnp.mean(u, axis=-1, keepdims=True)
    d = u - mu
    var = jnp.mean(d * d, axis=-1, keepdims=True)
    return d * lax.rsqrt(var + LN_EPS) * g + b


def _dot_nt(a, b):
    return lax.dot_general(a, b, (((1,), (1,)), ((), ())), preferred_element_type=F32)


def _dot_tn(a, b):
    return lax.dot_general(a, b, (((0,), (0,)), ((), ())), preferred_element_type=F32)


def _proj_kernel(x_ref, *refs, n_out, chunk):
    w_refs, o_refs = refs[:n_out], refs[n_out:]
    xb = x_ref[...].astype(BF16)
    for w_ref, o_ref in zip(w_refs, o_refs):
        n = w_ref.shape[1]
        for c0 in range(0, n, chunk):
            c1 = min(n, c0 + chunk)
            o_ref[:, c0:c1] = jnp.dot(xb, w_ref[:, c0:c1], preferred_element_type=F32).astype(o_ref.dtype)


def _proj(x, ws, out_dtypes, name):
    t, k = x.shape
    tm = min(TOKEN_TILE, t)
    in_specs = [pl.BlockSpec((tm, k), lambda i: (i, 0))]
    in_specs += [pl.BlockSpec(w.shape, lambda i: (0, 0)) for w in ws]
    out_specs = [pl.BlockSpec((tm, w.shape[1]), lambda i: (i, 0)) for w in ws]
    out_shape = [jax.ShapeDtypeStruct((t, w.shape[1]), dt) for w, dt in zip(ws, out_dtypes)]
    return pl.pallas_call(
        functools.partial(_proj_kernel, n_out=len(ws), chunk=512),
        grid=(t // tm,),
        in_specs=in_specs,
        out_specs=out_specs,
        out_shape=out_shape,
        compiler_params=_cparams("parallel"),
        name=name,
    )(x, *ws)


def _mamba_kernel(z_ref, x_ref, b_ref, c_ref, dt_ref, sel_ref, ptab_ref,
                  cwx_ref, cwb_ref, cwc_ref, cbx_ref, cbb_ref, cbc_ref,
                  c0x_ref, c0b_ref, c0c_ref, h0_ref, nw_ref,
                  y_ref, cox_ref, cob_ref, coc_ref, hout_ref,
                  sx, sb, sc, hs, *, q, nc, bb, gp):
    c = pl.program_id(2)
    pad = SUBLANES
    hist = SSM_CONV - 1
    gw, ns = GROUP_WIDTH, SSM_D_STATE

    @pl.when(c == 0)
    def _():
        sx[:, pad - hist:pad, :] = c0x_ref[...]
        sb[:, pad - hist:pad, :] = c0b_ref[...]
        sc[:, pad - hist:pad, :] = c0c_ref[...]
        hs[...] = h0_ref[...]

    ti = lax.broadcasted_iota(jnp.int32, (q, q), 0)
    si = lax.broadcasted_iota(jnp.int32, (q, q), 1)
    tri = ti >= si
    tri_f = tri.astype(F32)
    lane = lax.broadcasted_iota(jnp.int32, (1, gw), 1)

    def expand(v):
        r = v.shape[0]
        out = jnp.broadcast_to(v[:, HEADS_PER_GROUP - 1:HEADS_PER_GROUP], (r, gw))
        for j in range(HEADS_PER_GROUP - 2, -1, -1):
            out = jnp.where(lane < SSM_HEAD_DIM * (j + 1), jnp.broadcast_to(v[:, j:j + 1], (r, gw)), out)
        return out

    shift = (jnp.concatenate([(ti - si == d).astype(BF16) for d in range(hist, 0, -1)], axis=0)
             if q > SUBLANES else None)

    def conv(cur, s, u, c0, c1, w_ref, bias_ref):
        cur_f = cur.astype(F32)
        s[u, pad:pad + q, c0:c1] = cur_f
        rows = q if shift is None else SUBLANES
        acc = bias_ref[:, c0:c1]
        for k in range(SSM_CONV):
            acc = acc + w_ref[k:k + 1, c0:c1] * s[u, pad - hist + k:pad - hist + k + rows, c0:c1]
        if shift is not None:
            rest = bias_ref[:, c0:c1] + w_ref[hist:hist + 1, c0:c1] * cur_f
            moved = jnp.dot(shift, cur, preferred_element_type=F32)
            for k in range(hist):
                rest = rest + w_ref[k:k + 1, c0:c1] * moved[k * q:(k + 1) * q]
            acc = jnp.concatenate([acc, rest[SUBLANES:]], axis=0)
        tail = s[u, pad + q - hist:pad + q, c0:c1]
        s[u, pad - hist:pad, c0:c1] = tail
        return _silu(acc), tail

    tails = []
    for u in range(bb):
        for v in range(gp):
            x0, x1 = gw * v, gw * (v + 1)
            n0, n1 = ns * v, ns * (v + 1)
            xc, tail_x = conv(x_ref[u, :, x0:x1], sx, u, x0, x1, cwx_ref, cbx_ref)
            bc, tail_b = conv(b_ref[u, :, n0:n1], sb, u, n0, n1, cwb_ref, cbb_ref)
            cc, tail_c = conv(c_ref[u, :, n0:n1], sc, u, n0, n1, cwc_ref, cbc_ref)

            ptab = ptab_ref[v]
            dt4 = jnp.dot(dt_ref[u], sel_ref[v], precision=HIGHEST, preferred_element_type=F32)
            dt4 = _softplus(dt4 + ptab[0:1, :])
            da = dt4 * (-jnp.exp(ptab[1:2, :]))
            acum = jnp.dot(tri_f, da, precision=HIGHEST, preferred_element_type=F32)
            acum_t = acum.T

            dt_e = expand(dt4)
            ac_e = expand(acum)
            ac_last = acum[q - 1:q, :]
            ac_last_e = expand(ac_last)

            xdt = xc * dt_e
            xdt_b = xdt.astype(BF16)
            bb_ = bc.astype(BF16)
            cb_ = cc.astype(BF16)
            cbm = _dot_nt(cb_, bb_)
            hprev = hs[u, v]
            y = _dot_nt(cb_, hprev.astype(BF16)) * jnp.exp(ac_e)
            zero_b = jnp.zeros_like(xdt_b)
            for j in range(HEADS_PER_GROUP):
                seg = acum[:, j:j + 1] - acum_t[j:j + 1, :]
                lm = jnp.exp(jnp.where(tri, seg, -jnp.inf))
                wj = (cbm * lm).astype(BF16)
                xm = jnp.where((lane >= SSM_HEAD_DIM * j) & (lane < SSM_HEAD_DIM * (j + 1)), xdt_b, zero_b)
                y = y + jnp.dot(wj, xm, preferred_element_type=F32)

            xw = (xdt * jnp.exp(ac_last_e - ac_e)).astype(BF16)
            st = _dot_tn(xw, bb_)
            cd = jnp.exp(ac_last)
            for j in range(HEADS_PER_GROUP):
                r0, r1 = SSM_HEAD_DIM * j, SSM_HEAD_DIM * (j + 1)
                hs[u, v, r0:r1, :] = (jnp.broadcast_to(cd[:, j:j + 1], (SSM_HEAD_DIM, ns)) * hprev[r0:r1, :]
                                      + st[r0:r1, :])

            y = y + expand(ptab[2:3, :]) * xc
            zf = z_ref[u, :, x0:x1].astype(F32)
            y = y * _silu(zf)
            ms = jnp.mean(y * y, axis=-1, keepdims=True)
            y_ref[u, :, x0:x1] = (y * lax.rsqrt(ms + LN_EPS) * nw_ref[:, x0:x1]).astype(y_ref.dtype)
            tails.append((u, x0, x1, n0, n1, tail_x, tail_b, tail_c))

    @pl.when(c == nc - 1)
    def _():
        for u, x0, x1, n0, n1, tail_x, tail_b, tail_c in tails:
            cox_ref[u, :, x0:x1] = tail_x
            cob_ref[u, :, n0:n1] = tail_b
            coc_ref[u, :, n0:n1] = tail_c
        hout_ref[...] = hs[...]


def _mamba_core(zx, dt, conv0, ssm0, sel, ptab, conv_w, conv_b, norm_w, bsz, seqlen):
    q = math.gcd(seqlen, SSM_CHUNK)
    nc = seqlen // q
    g = SSM_GROUPS
    gp = MAMBA_GROUPS_PER_STEP
    bb = 1 if nc > 1 else math.gcd(bsz, MAMBA_SEQS_PER_STEP)
    gw, ns = gp * GROUP_WIDTH, gp * SSM_D_STATE
    zx3 = zx.reshape(bsz, seqlen, zx.shape[-1])
    dt3 = dt.reshape(bsz, seqlen, LANES)
    h0 = ssm0.reshape(bsz, g, GROUP_WIDTH, SSM_D_STATE)
    zb = SSM_D_INNER // gw
    bblk = (2 * SSM_D_INNER) // ns
    cblk = bblk + g // gp
    cwb0 = SSM_D_INNER // ns
    cwc0 = cwb0 + g // gp
    hist = SSM_CONV - 1
    in_specs = [
        pl.BlockSpec((bb, q, gw), lambda b, gi, c: (b, c, gi)),
        pl.BlockSpec((bb, q, gw), lambda b, gi, c: (b, c, zb + gi)),
        pl.BlockSpec((bb, q, ns), lambda b, gi, c: (b, c, bblk + gi)),
        pl.BlockSpec((bb, q, ns), lambda b, gi, c: (b, c, cblk + gi)),
        pl.BlockSpec((bb, q, LANES), lambda b, gi, c: (b, c, 0)),
        pl.BlockSpec((gp, LANES, LANES), lambda b, gi, c: (gi, 0, 0)),
        pl.BlockSpec((gp, SUBLANES, LANES), lambda b, gi, c: (gi, 0, 0)),
        pl.BlockSpec((SSM_CONV, gw), lambda b, gi, c: (0, gi)),
        pl.BlockSpec((SSM_CONV, ns), lambda b, gi, c: (0, cwb0 + gi)),
        pl.BlockSpec((SSM_CONV, ns), lambda b, gi, c: (0, cwc0 + gi)),
        pl.BlockSpec((1, gw), lambda b, gi, c: (0, gi)),
        pl.BlockSpec((1, ns), lambda b, gi, c: (0, cwb0 + gi)),
        pl.BlockSpec((1, ns), lambda b, gi, c: (0, cwc0 + gi)),
        pl.BlockSpec((bb, hist, gw), lambda b, gi, c: (b, 0, gi)),
        pl.BlockSpec((bb, hist, ns), lambda b, gi, c: (b, 0, cwb0 + gi)),
        pl.BlockSpec((bb, hist, ns), lambda b, gi, c: (b, 0, cwc0 + gi)),
        pl.BlockSpec((bb, gp, GROUP_WIDTH, SSM_D_STATE), lambda b, gi, c: (b, gi, 0, 0)),
        pl.BlockSpec((1, gw), lambda b, gi, c: (0, gi)),
    ]
    out_specs = [
        pl.BlockSpec((bb, q, gw), lambda b, gi, c: (b, c, gi)),
        pl.BlockSpec((bb, hist, gw), lambda b, gi, c: (b, 0, gi)),
        pl.BlockSpec((bb, hist, ns), lambda b, gi, c: (b, 0, gi)),
        pl.BlockSpec((bb, hist, ns), lambda b, gi, c: (b, 0, gi)),
        pl.BlockSpec((bb, gp, GROUP_WIDTH, SSM_D_STATE), lambda b, gi, c: (b, gi, 0, 0)),
    ]
    gn = g * SSM_D_STATE
    out_shape = [
        jax.ShapeDtypeStruct((bsz, seqlen, SSM_D_INNER), BF16),
        jax.ShapeDtypeStruct((bsz, hist, SSM_D_INNER), F32),
        jax.ShapeDtypeStruct((bsz, hist, gn), F32),
        jax.ShapeDtypeStruct((bsz, hist, gn), F32),
        jax.ShapeDtypeStruct((bsz, g, GROUP_WIDTH, SSM_D_STATE), F32),
    ]
    scratch = [
        pltpu.VMEM((bb, q + SUBLANES, gw), F32),
        pltpu.VMEM((bb, q + SUBLANES, ns), F32),
        pltpu.VMEM((bb, q + SUBLANES, ns), F32),
        pltpu.VMEM((bb, gp, GROUP_WIDTH, SSM_D_STATE), F32),
    ]
    y, cox, cob, coc, hout = pl.pallas_call(
        functools.partial(_mamba_kernel, q=q, nc=nc, bb=bb, gp=gp),
        grid=(bsz // bb, g // gp, nc),
        in_specs=in_specs,
        out_specs=out_specs,
        out_shape=out_shape,
        scratch_shapes=scratch,
        compiler_params=_cparams("parallel", "parallel", "arbitrary"),
        name="mamba_core",
    )(zx3, zx3, zx3, zx3, dt3, sel, ptab, conv_w, conv_w, conv_w, conv_b, conv_b, conv_b,
      conv0, conv0, conv0, h0, norm_w)
    new_conv = jnp.concatenate([cox, cob, coc], axis=-1)
    return (y.reshape(bsz * seqlen, SSM_D_INNER), new_conv,
            hout.reshape(bsz, SSM_HEADS, SSM_HEAD_DIM, SSM_D_STATE))


def _post_mixer_kernel(y_ref, w_ref, xres_ref, g_ref, b_ref, wr_ref, br_ref,
                       x1_ref, route_ref, cnt_ref, base, *, tm):
    i = pl.program_id(0)

    @pl.when(i == 0)
    def _():
        base[...] = jnp.zeros_like(base)

    h = jnp.dot(y_ref[...], w_ref[...], preferred_element_type=F32)
    x1 = _layer_norm(DN_ALPHA * xres_ref[...] + h, g_ref[...], b_ref[...])
    x1_ref[...] = x1

    x_hi = x1.astype(BF16)
    x_lo = (x1 - x_hi.astype(F32)).astype(BF16)
    both = jnp.dot(x_hi, wr_ref[...], preferred_element_type=F32)
    logits = (both[:, :LANES] + both[:, LANES:]
              + jnp.dot(x_lo, wr_ref[:, :LANES], preferred_element_type=F32) + br_ref[...])
    lane = lax.broadcasted_iota(jnp.int32, (tm, LANES), 1)
    neg = -jnp.inf
    gmask = lane < MOE_GROUPS
    gl = jnp.where(gmask, logits, neg)
    mg = jnp.max(gl, axis=1, keepdims=True)
    grp = jnp.min(jnp.where(gl == mg, lane, LANES), axis=1, keepdims=True)
    pg = 1.0 / jnp.sum(jnp.where(gmask, jnp.exp(logits - mg), 0.0), axis=1, keepdims=True)
    lo = MOE_GROUPS + MOE_EXPERTS_PER_GROUP * grp
    el = jnp.where((lane >= lo) & (lane < lo + MOE_EXPERTS_PER_GROUP), logits, neg)
    v1 = jnp.max(el, axis=1, keepdims=True)
    i1 = jnp.min(jnp.where(el == v1, lane, LANES), axis=1, keepdims=True)
    el2 = jnp.where(lane == i1, neg, el)
    v2 = jnp.max(el2, axis=1, keepdims=True)
    i2 = jnp.min(jnp.where(el2 == v2, lane, LANES), axis=1, keepdims=True)
    t = jnp.exp(v2 - v1)
    wa = pg / (1.0 + t)
    wb = pg * t / (1.0 + t)
    e1 = i1 - MOE_GROUPS
    e2 = i2 - MOE_GROUPS

    oh = ((lane == e1) | (lane == e2)).astype(BF16)
    ri = lax.broadcasted_iota(jnp.int32, (tm, tm), 0)
    ci = lax.broadcasted_iota(jnp.int32, (tm, tm), 1)
    before = jnp.dot((ri > ci).astype(BF16), oh, preferred_element_type=F32) + base[...]
    r1 = jnp.sum(jnp.where(lane == e1, before, 0.0), axis=1, keepdims=True)
    r2 = jnp.sum(jnp.where(lane == e2, before, 0.0), axis=1, keepdims=True)
    new_base = base[...] + jnp.sum(oh.astype(F32), axis=0, keepdims=True)
    base[...] = new_base
    cnt_ref[...] = new_base

    route = jnp.where(lane == 0, e1.astype(F32), 0.0)
    route = jnp.where(lane == 1, e2.astype(F32), route)
    route = jnp.where(lane == 2, r1, route)
    route = jnp.where(lane == 3, r2, route)
    route = jnp.where(lane == 4, wa, route)
    route = jnp.where(lane == 5, wb, route)
    route_ref[...] = route


def _post_mixer(y, w, xres, ln_g, ln_b, wr, br, name):
    t, kin = y.shape
    tm = min(TOKEN_TILE, t)
    return pl.pallas_call(
        functools.partial(_post_mixer_kernel, tm=tm),
        grid=(t // tm,),
        in_specs=[
            pl.BlockSpec((tm, kin), lambda i: (i, 0)),
            pl.BlockSpec((kin, D_MODEL), lambda i: (0, 0)),
            pl.BlockSpec((tm, D_MODEL), lambda i: (i, 0)),
            pl.BlockSpec((1, D_MODEL), lambda i: (0, 0)),
            pl.BlockSpec((1, D_MODEL), lambda i: (0, 0)),
            pl.BlockSpec((D_MODEL, 2 * LANES), lambda i: (0, 0)),
            pl.BlockSpec((1, LANES), lambda i: (0, 0)),
        ],
        out_specs=[
            pl.BlockSpec((tm, D_MODEL), lambda i: (i, 0)),
            pl.BlockSpec((tm, LANES), lambda i: (i, 0)),
            pl.BlockSpec((1, LANES), lambda i: (0, 0)),
        ],
        out_shape=[
            jax.ShapeDtypeStruct((t, D_MODEL), F32),
            jax.ShapeDtypeStruct((t, LANES), F32),
            jax.ShapeDtypeStruct((1, LANES), F32),
        ],
        scratch_shapes=[pltpu.VMEM((1, LANES), F32)],
        compiler_params=_cparams("arbitrary"),
        name=name,
    )(y, w, xres, ln_g, ln_b, wr, br)


def _row_copy(src_ref, src_row, dst_ref, dst_row, sem):
    return pltpu.make_async_copy(src_ref.at[pl.ds(src_row, 1)], dst_ref.at[pl.ds(dst_row, 1)], sem)


def _dispatch_kernel(dest_ref, x_ref, xs_ref, sem, *, tm):
    def start(r, carry):
        _row_copy(x_ref, r, xs_ref, dest_ref[0, 0, 2 * r], sem).start()
        _row_copy(x_ref, r, xs_ref, dest_ref[0, 0, 2 * r + 1], sem).start(priority=1)
        return carry

    lax.fori_loop(0, tm, start, 0, unroll=True)
    for _ in range(2):
        pltpu.make_async_copy(x_ref, xs_ref.at[pl.ds(0, tm)], sem).wait()


def _dispatch(x1, dest3, tm):
    t = x1.shape[0]
    return pl.pallas_call(
        functools.partial(_dispatch_kernel, tm=tm),
        grid=(t // tm,),
        in_specs=[
            pl.BlockSpec((1, 1, 2 * tm), lambda i: (i, 0, 0), memory_space=pltpu.SMEM),
            pl.BlockSpec((tm, D_MODEL), lambda i: (i, 0)),
        ],
        out_specs=pl.BlockSpec(memory_space=pl.ANY),
        out_shape=jax.ShapeDtypeStruct((2 * t, D_MODEL), F32),
        scratch_shapes=[pltpu.SemaphoreType.DMA(())],
        compiler_params=_cparams("arbitrary"),
        name="moe_dispatch",
    )(dest3, x1)


def _ffn_kernel(wt_ref, we_ref, wlo_ref, whi_ref, wfirst_ref,
                xs_ref, w1_ref, w3_ref, w2_ref, ys_ref, *, tf):
    w = pl.program_id(0)
    lo = wlo_ref[w]
    hi = whi_ref[w]

    @pl.when(wfirst_ref[w] == 1)
    def _():
        ys_ref[...] = jnp.zeros_like(ys_ref)

    @pl.when(hi > lo)
    def _():
        rows = wt_ref[w] * tf + lax.broadcasted_iota(jnp.int32, (tf, 1), 0)
        mask = (rows >= lo) & (rows < hi)
        xb = xs_ref[...].astype(BF16)
        a = jnp.dot(xb, w1_ref[0].astype(BF16), preferred_element_type=F32)
        b = jnp.dot(xb, w3_ref[0].astype(BF16), preferred_element_type=F32)
        hmid = (_silu(a) * b).astype(BF16)
        y = jnp.dot(hmid, w2_ref[0].astype(BF16), preferred_element_type=F32)
        ys_ref[...] = jnp.where(mask, y, ys_ref[...])


def _ffn(xs, w1, w3, w2, work, tf):
    n = xs.shape[0]
    n_work = work[0].shape[0]
    grid_spec = pltpu.PrefetchScalarGridSpec(
        num_scalar_prefetch=5,
        grid=(n_work,),
        in_specs=[
            pl.BlockSpec((tf, D_MODEL), lambda w, wt, we, wlo, whi, wf: (wt[w], 0)),
            pl.BlockSpec((1, D_MODEL, MOE_FF), lambda w, wt, we, wlo, whi, wf: (we[w], 0, 0)),
            pl.BlockSpec((1, D_MODEL, MOE_FF), lambda w, wt, we, wlo, whi, wf: (we[w], 0, 0)),
            pl.BlockSpec((1, MOE_FF, D_MODEL), lambda w, wt, we, wlo, whi, wf: (we[w], 0, 0)),
        ],
        out_specs=pl.BlockSpec((tf, D_MODEL), lambda w, wt, we, wlo, whi, wf: (wt[w], 0)),
    )
    return pl.pallas_call(
        functools.partial(_ffn_kernel, tf=tf),
        grid_spec=grid_spec,
        out_shape=jax.ShapeDtypeStruct((n, D_MODEL), F32),
        compiler_params=_cparams("arbitrary"),
        name="moe_ffn",
    )(*work, xs, w1, w3, w2)


def _combine_kernel(dest_ref, x1_ref, route_ref, g_ref, b_ref, ys_ref, o_ref, buf0, buf1, sem, *, tm):
    def start(r, carry):
        _row_copy(ys_ref, dest_ref[0, 0, 2 * r], buf0, r, sem).start()
        _row_copy(ys_ref, dest_ref[0, 0, 2 * r + 1], buf1, r, sem).start(priority=1)
        return carry

    lax.fori_loop(0, tm, start, 0, unroll=True)
    pltpu.make_async_copy(ys_ref.at[pl.ds(0, tm)], buf0, sem).wait()
    pltpu.make_async_copy(ys_ref.at[pl.ds(0, tm)], buf1, sem).wait()

    route = route_ref[...]
    y = route[:, 4:5] * buf0[...] + route[:, 5:6] * buf1[...]
    o_ref[...] = _layer_norm(DN_ALPHA * x1_ref[...] + y, g_ref[...], b_ref[...])


def _combine(x1, route, ys, dest3, ln_g, ln_b, tm, name):
    t = x1.shape[0]
    return pl.pallas_call(
        functools.partial(_combine_kernel, tm=tm),
        grid=(t // tm,),
        in_specs=[
            pl.BlockSpec((1, 1, 2 * tm), lambda i: (i, 0, 0), memory_space=pltpu.SMEM),
            pl.BlockSpec((tm, D_MODEL), lambda i: (i, 0)),
            pl.BlockSpec((tm, LANES), lambda i: (i, 0)),
            pl.BlockSpec((1, D_MODEL), lambda i: (0, 0)),
            pl.BlockSpec((1, D_MODEL), lambda i: (0, 0)),
            pl.BlockSpec(memory_space=pl.ANY),
        ],
        out_specs=pl.BlockSpec((tm, D_MODEL), lambda i: (i, 0)),
        out_shape=jax.ShapeDtypeStruct((t, D_MODEL), F32),
        scratch_shapes=[
            pltpu.VMEM((tm, D_MODEL), F32),
            pltpu.VMEM((tm, D_MODEL), F32),
            pltpu.SemaphoreType.DMA(()),
        ],
        compiler_params=_cparams("arbitrary"),
        name=name,
    )(dest3, x1, route, ln_g, ln_b, ys)


def _moe(x1, route, counts, w1, w3, w2, expert_base, ln_g, ln_b, name):
    t = x1.shape[0]
    n = 2 * t
    tm = min(TOKEN_TILE, t)
    tf = min(FFN_TILE, n)
    cnt = counts[0, :MOE_EXPERTS].astype(jnp.int32)
    offs = jnp.concatenate([jnp.zeros((1,), jnp.int32), jnp.cumsum(cnt)])
    e = route[:, 0:2].astype(jnp.int32)
    r = route[:, 2:4].astype(jnp.int32)
    dest = offs[e] + r
    dest3 = dest.reshape(t // tm, 1, 2 * tm)

    n_tiles = n // tf
    n_work = n_tiles + MOE_EXPERTS
    first_tile = offs[:-1] // tf
    last_tile = (offs[1:] - 1) // tf
    ntile_e = jnp.where(cnt > 0, last_tile - first_tile + 1, 0)
    wstart = jnp.concatenate([jnp.zeros((1,), jnp.int32), jnp.cumsum(ntile_e)])
    total = wstart[-1]
    widx = jnp.arange(n_work, dtype=jnp.int32)
    we = jnp.clip(jnp.searchsorted(wstart, widx, side="right") - 1, 0, MOE_EXPERTS - 1).astype(jnp.int32)
    wt = first_tile[we] + (widx - wstart[we])
    valid = widx < total
    wt = jnp.where(valid, wt, n_tiles - 1).astype(jnp.int32)
    wlo = jnp.where(valid, offs[we], 0).astype(jnp.int32)
    whi = jnp.where(valid, offs[we + 1], 0).astype(jnp.int32)
    last_e = we[jnp.maximum(total - 1, 0)]
    we = jnp.where(valid, we, last_e).astype(jnp.int32)
    wfirst = jnp.concatenate([jnp.ones((1,), jnp.int32), (wt[1:] != wt[:-1]).astype(jnp.int32)])

    xs = _dispatch(x1, dest3, tm)
    ys = _ffn(xs, w1, w3, w2, (wt, we + expert_base, wlo, whi, wfirst), tf)
    return _combine(x1, route, ys, dest3, ln_g, ln_b, tm, name)


def _bias_kernel(idx_ref, rbt_ref, o_ref, *, width):
    idx = idx_ref[...]
    bucket = lax.broadcasted_iota(jnp.int32, (NUM_BUCKETS, width), 0)
    onehot = (bucket == idx).astype(F32)
    vals = jnp.dot(rbt_ref[...], onehot, precision=HIGHEST, preferred_element_type=F32)
    o_ref[...] = jnp.where(idx < 0, -jnp.inf, vals)


def _bias_tiles(rel_bias, idx_np):
    n = idx_np.shape[0]
    width = 8192
    assert n % width == 0
    nh = rel_bias.shape[1]
    return pl.pallas_call(
        functools.partial(_bias_kernel, width=width),
        grid=(n // width,),
        in_specs=[
            pl.BlockSpec((1, width), lambda i: (0, i)),
            pl.BlockSpec((nh, NUM_BUCKETS), lambda i: (0, 0)),
        ],
        out_specs=pl.BlockSpec((nh, width), lambda i: (0, i)),
        out_shape=jax.ShapeDtypeStruct((nh, n), F32),
        compiler_params=_cparams("parallel"),
        name="rel_bias_tiles",
    )(jnp.asarray(idx_np.reshape(1, n)), rel_bias.T)


def _bucket_of_distance(dist):
    n = np.maximum(dist, 0)
    max_exact = NUM_BUCKETS // 2
    nf = np.maximum(n, 1).astype(np.float32)
    large = max_exact + (np.log(nf / np.float32(max_exact)) / np.float32(math.log(MAX_DISTANCE / max_exact))
                         * np.float32(NUM_BUCKETS - max_exact)).astype(np.int32)
    large = np.minimum(large, NUM_BUCKETS - 1)
    return np.where(n < max_exact, n, large).astype(np.int32)


def _bucket_tile(dist):
    return np.where(dist >= 0, _bucket_of_distance(dist), -1).astype(np.int32)


def _kvq_prompt_kernel(x_ref, wkt_ref, wv_ref, wq_ref, kt_ref, ktb_ref, v_ref, vb_ref, q_ref, *, chunk):
    xb = x_ref[...].astype(BF16)
    hd2 = 2 * ATT_HEAD_DIM
    for c0 in range(0, ATT_QK_DIM, chunk):
        kt = _dot_nt(wkt_ref[c0:c0 + chunk, :], xb)
        kt_ref[0, c0:c0 + chunk, :] = kt
        ktb_ref[0, 0, c0:c0 + chunk, :] = kt.astype(BF16)
    for c0 in range(0, ATT_V_DIM, chunk):
        v = jnp.dot(xb, wv_ref[:, c0:c0 + chunk], preferred_element_type=F32)
        vb_ref[:, c0:c0 + chunk] = v.astype(BF16)
        for h0 in range(0, chunk, hd2):
            v_ref[:, (c0 + h0) // hd2, :] = v[:, h0:h0 + hd2]
    for c0 in range(0, ATT_QK_DIM, chunk):
        q_ref[:, c0:c0 + chunk] = jnp.dot(xb, wq_ref[:, c0:c0 + chunk], preferred_element_type=F32).astype(BF16)


def _kvq_prompt(x, wkt, wv, wq, bsz, seqlen):
    t = bsz * seqlen
    tm = min(ATT_TILE, seqlen)
    nkb = seqlen // tm
    hd2 = 2 * ATT_HEAD_DIM
    full = lambda i: (0, 0)
    return pl.pallas_call(
        functools.partial(_kvq_prompt_kernel, chunk=512),
        grid=(t // tm,),
        in_specs=[
            pl.BlockSpec((tm, D_MODEL), lambda i: (i, 0)),
            pl.BlockSpec(wkt.shape, full),
            pl.BlockSpec(wv.shape, full),
            pl.BlockSpec(wq.shape, full),
        ],
        out_specs=[
            pl.BlockSpec((1, ATT_QK_DIM, tm), lambda i: (i // nkb, 0, i % nkb)),
            pl.BlockSpec((1, 1, ATT_QK_DIM, tm), lambda i: (i // nkb, i % nkb, 0, 0)),
            pl.BlockSpec((tm, ATT_HEADS, hd2), lambda i: (i, 0, 0)),
            pl.BlockSpec((tm, ATT_V_DIM), lambda i: (i, 0)),
            pl.BlockSpec((tm, ATT_QK_DIM), lambda i: (i, 0)),
        ],
        out_shape=[
            jax.ShapeDtypeStruct((bsz, ATT_QK_DIM, seqlen), F32),
            jax.ShapeDtypeStruct((bsz, nkb, ATT_QK_DIM, tm), BF16),
            jax.ShapeDtypeStruct((t, ATT_HEADS, hd2), F32),
            jax.ShapeDtypeStruct((t, ATT_V_DIM), BF16),
            jax.ShapeDtypeStruct((t, ATT_QK_DIM), BF16),
        ],
        compiler_params=_cparams("parallel"),
        name="kvq_proj_prompt",
    )(x, wkt, wv, wq)


def _lambda_value(lam_ref, layer_idx):
    lv = lam_ref[...]
    s1 = jnp.sum(lv[0:1, :] * lv[1:2, :], axis=1, keepdims=True)
    s2 = jnp.sum(lv[2:3, :] * lv[3:4, :], axis=1, keepdims=True)
    lam_init = 0.8 - 0.6 * math.exp(-0.3 * layer_idx)
    return jnp.exp(s1) - jnp.exp(s2) + lam_init, lam_init


def _attn_prompt_kernel(q_ref, kt_ref, v_ref, bias_ref, lam_ref, sw_ref, o_ref, *, tq, nq, layer_idx):
    lane = lax.broadcasted_iota(jnp.int32, (tq, 2 * ATT_HEAD_DIM), 1)
    lam, lam_init = _lambda_value(lam_ref, layer_idx)
    for qi in range(nq):
        qb = q_ref[0, qi * tq:(qi + 1) * tq, :]
        zero = jnp.zeros_like(qb)
        q2 = jnp.concatenate([jnp.where(lane < ATT_HEAD_DIM, qb, zero),
                              jnp.where(lane >= ATT_HEAD_DIM, qb, zero)], axis=0)
        m = l = acc = None
        for j in range(qi + 1):
            s = jnp.dot(q2, kt_ref[0, j], preferred_element_type=F32)
            if qi - j < 2:
                s = s + bias_ref[0, qi - j]
            vj = v_ref[0, j * tq:(j + 1) * tq, :]
            bm = jnp.max(s, axis=1, keepdims=True)
            if j == 0:
                m = bm
                p = jnp.exp2(s - m)
                l = jnp.sum(p, axis=1, keepdims=True)
                acc = jnp.dot(p.astype(BF16), vj, preferred_element_type=F32)
            else:
                m_new = jnp.maximum(m, bm)
                alpha = jnp.exp2(m - m_new)
                p = jnp.exp2(s - m_new)
                l = alpha * l + jnp.sum(p, axis=1, keepdims=True)
                acc = alpha * acc + jnp.dot(p.astype(BF16), vj, preferred_element_type=F32)
                m = m_new
        o = acc / l
        d = o[:tq] - lam * o[tq:]
        ms = jnp.mean(d * d, axis=-1, keepdims=True)
        o_ref[0, qi * tq:(qi + 1) * tq, :] = (
            d * lax.rsqrt(ms + LN_EPS) * sw_ref[...] * (1.0 - lam_init)).astype(o_ref.dtype)


def _attn_prompt(q, ktb, vb, bias, lam_vec, subln_w, bsz, seqlen, layer_idx):
    tq = min(ATT_TILE, seqlen)
    nq = seqlen // tq
    hd2 = 2 * ATT_HEAD_DIM
    q3 = q.reshape(bsz, seqlen, ATT_QK_DIM)
    v3 = vb.reshape(bsz, seqlen, ATT_V_DIM)
    out = pl.pallas_call(
        functools.partial(_attn_prompt_kernel, tq=tq, nq=nq, layer_idx=layer_idx),
        grid=(bsz, ATT_HEADS),
        in_specs=[
            pl.BlockSpec((1, seqlen, hd2), lambda b, h: (b, 0, h)),
            pl.BlockSpec((1, nq, hd2, tq), lambda b, h: (b, 0, h, 0)),
            pl.BlockSpec((1, seqlen, hd2), lambda b, h: (b, 0, h)),
            pl.BlockSpec((1, 2, 2 * tq, tq), lambda b, h: (h, 0, 0, 0)),
            pl.BlockSpec(lam_vec.shape, lambda b, h: (0, 0)),
            pl.BlockSpec((1, hd2), lambda b, h: (0, 0)),
        ],
        out_specs=pl.BlockSpec((1, seqlen, hd2), lambda b, h: (b, 0, h)),
        out_shape=jax.ShapeDtypeStruct((bsz, seqlen, ATT_V_DIM), BF16),
        compiler_params=_cparams("parallel", "parallel"),
        name="attn_prompt",
    )(q3, ktb, v3, bias, lam_vec, subln_w)
    return out.reshape(bsz * seqlen, ATT_V_DIM)


def _attn_sample_kernel(pt_ref, q_ref, *refs, nq, n_steps, pp, layer_idx):
    kt_refs = refs[:pp]
    vm_refs = refs[pp:2 * pp]
    (kn_ref, vn_ref, bias_ref, exp_ref, pmask_ref, lam_ref, sw_ref,
     o_ref, qbd, m_s, l_s, acc_s, kpad, vpad) = refs[2 * pp:]
    j = pl.program_id(1)
    maps = 2 * ATT_HEADS
    rows = maps * nq
    hd2 = 2 * ATT_HEAD_DIM
    page = PAGE_SIZE

    @pl.when(j == 0)
    def _():
        m_s[...] = jnp.full_like(m_s, -jnp.inf)
        l_s[...] = jnp.zeros_like(l_s)
        acc_s[...] = jnp.zeros_like(acc_s)
        qt = jnp.concatenate([q_ref[0].astype(F32)] * maps, axis=0)
        ri = lax.broadcasted_iota(jnp.int32, (rows, ATT_QK_DIM), 0)
        ci = lax.broadcasted_iota(jnp.int32, (rows, ATT_QK_DIM), 1)
        keep = (ri // nq) == (ci // ATT_HEAD_DIM)
        qbd[...] = jnp.where(keep, qt, 0.0).astype(qbd.dtype)

    def softmax_update(s):
        m = m_s[...]
        m_new = jnp.maximum(m, jnp.max(s, axis=1, keepdims=True))
        alpha = jnp.exp2(m - m_new)
        p = jnp.exp2(s - m_new)
        l_s[...] = alpha * l_s[...] + jnp.sum(p, axis=1, keepdims=True)
        m_s[...] = m_new
        return alpha, p.astype(BF16)

    @pl.when(j < n_steps)
    def _():
        q2 = qbd[...]
        s_parts = []
        for u in range(pp):
            kt = kt_refs[u][0].reshape(ATT_QK_DIM, page).astype(BF16)
            s_u = jnp.dot(q2, kt, preferred_element_type=F32)
            if u == pp - 1:
                s_u = s_u + bias_ref[jnp.where(j == n_steps - 1, 0, 2)]
            s_parts.append(s_u)
        alpha, pb = softmax_update(jnp.concatenate(s_parts, axis=1))
        pv = jnp.zeros((rows, hd2), F32)
        for u in range(pp):
            pe = jnp.dot(pb[:, page * u:page * (u + 1)], exp_ref[...], preferred_element_type=F32)
            pe = pe.astype(BF16) * pmask_ref[...]
            vm = vm_refs[u][0].reshape(page * ATT_HEADS, hd2).astype(BF16)
            pv = pv + jnp.dot(pe, vm, preferred_element_type=F32)
        acc_s[...] = alpha * acc_s[...] + pv

    @pl.when(j == n_steps)
    def _():
        kpad[...] = jnp.zeros_like(kpad)
        vpad[...] = jnp.zeros_like(vpad)
        kpad[0:nq, :] = kn_ref[0]
        vpad[0:nq, :] = vn_ref[0]
        s = _dot_nt(qbd[...], kpad[...].astype(BF16)) + bias_ref[1]
        alpha, pb = softmax_update(s)
        for h in range(ATT_HEADS):
            r0, r1 = 2 * nq * h, 2 * nq * (h + 1)
            pvh = jnp.dot(pb[r0:r1, :], vpad[:, hd2 * h:hd2 * (h + 1)].astype(BF16), preferred_element_type=F32)
            acc_s[r0:r1, :] = alpha[r0:r1, :] * acc_s[r0:r1, :] + pvh
        o = acc_s[...] / l_s[...]
        lam, lam_init = _lambda_value(lam_ref, layer_idx)
        for h in range(ATT_HEADS):
            r0 = 2 * nq * h
            d = o[r0:r0 + nq, :] - lam * o[r0 + nq:r0 + 2 * nq, :]
            ms = jnp.mean(d * d, axis=-1, keepdims=True)
            o_ref[0, :, hd2 * h:hd2 * (h + 1)] = (
                d * lax.rsqrt(ms + LN_EPS) * sw_ref[...] * (1.0 - lam_init)).astype(o_ref.dtype)


def _attn_sample(q, kv, cache_k, cache_v, page_table, bias, lam_vec, subln_w, bsz, nq, layer_idx):
    n_pages = page_table.shape[1]
    page = cache_k.shape[1]
    maps = 2 * ATT_HEADS
    rows = maps * nq
    hd2 = 2 * ATT_HEAD_DIM
    pp = math.gcd(n_pages, SAMPLE_PAGES_PER_STEP)
    n_steps = n_pages // pp
    q3 = q.reshape(bsz, nq, ATT_QK_DIM)
    kv3 = kv.reshape(bsz, nq, ATT_QK_DIM + ATT_V_DIM)
    cache_kt = jnp.transpose(cache_k, (0, 2, 3, 1))
    kk = np.arange(page)[:, None]
    cc = np.arange(page * ATT_HEADS)[None, :]
    expand = jnp.asarray((cc // ATT_HEADS == kk).astype(np.float32), dtype=BF16)
    rr = np.arange(rows)[:, None]
    pmask = jnp.asarray(((rr // (2 * nq)) == (cc % ATT_HEADS)).astype(np.float32), dtype=BF16)

    def page_map(u):
        return lambda b, j, pt: (pt[b, jnp.minimum(j * pp + u, n_pages - 1)], 0, 0, 0)

    in_specs = [pl.BlockSpec((1, nq, ATT_QK_DIM), lambda b, j, pt: (b, 0, 0))]
    in_specs += [pl.BlockSpec((1, maps, ATT_HEAD_DIM, page), page_map(u)) for u in range(pp)]
    in_specs += [pl.BlockSpec((1, page, ATT_HEADS, hd2), page_map(u)) for u in range(pp)]
    in_specs += [
        pl.BlockSpec((1, nq, ATT_QK_DIM), lambda b, j, pt: (b, 0, 0)),
        pl.BlockSpec((1, nq, ATT_V_DIM), lambda b, j, pt: (b, 0, 1)),
        pl.BlockSpec((3, rows, page), lambda b, j, pt: (0, 0, 0)),
        pl.BlockSpec((page, page * ATT_HEADS), lambda b, j, pt: (0, 0)),
        pl.BlockSpec((rows, page * ATT_HEADS), lambda b, j, pt: (0, 0)),
        pl.BlockSpec(lam_vec.shape, lambda b, j, pt: (0, 0)),
        pl.BlockSpec((1, hd2), lambda b, j, pt: (0, 0)),
    ]
    grid_spec = pltpu.PrefetchScalarGridSpec(
        num_scalar_prefetch=1,
        grid=(bsz, n_steps + 1),
        in_specs=in_specs,
        out_specs=pl.BlockSpec((1, nq, ATT_V_DIM), lambda b, j, pt: (b, 0, 0)),
        scratch_shapes=[
            pltpu.VMEM((rows, ATT_QK_DIM), BF16),
            pltpu.VMEM((rows, 1), F32),
            pltpu.VMEM((rows, 1), F32),
            pltpu.VMEM((rows, hd2), F32),
            pltpu.VMEM((page, ATT_QK_DIM), F32),
            pltpu.VMEM((page, ATT_V_DIM), F32),
        ],
    )
    out = pl.pallas_call(
        functools.partial(_attn_sample_kernel, nq=nq, n_steps=n_steps, pp=pp, layer_idx=layer_idx),
        grid_spec=grid_spec,
        out_shape=jax.ShapeDtypeStruct((bsz, nq, ATT_V_DIM), BF16),
        compiler_params=_cparams("parallel", "arbitrary"),
        name="attn_sample",
    )(page_table, q3, *([cache_kt] * pp), *([cache_v] * pp), kv3, kv3, bias, expand, pmask, lam_vec, subln_w)
    return out.reshape(bsz * nq, ATT_V_DIM)


def _router_params(wg, bg, we, be):
    wexp = jnp.transpose(we, (1, 0, 2)).reshape(D_MODEL, MOE_EXPERTS)
    wr = jnp.concatenate([wg, wexp], axis=1)
    wr = jnp.pad(wr, ((0, 0), (0, LANES - wr.shape[1])))
    w_hi = wr.astype(BF16)
    w_lo = (wr - w_hi.astype(F32)).astype(BF16)
    wr = jnp.concatenate([w_hi, w_lo], axis=1)
    br = jnp.concatenate([bg, be.reshape(MOE_EXPERTS)])
    br = jnp.pad(br, (0, LANES - br.shape[0])).reshape(1, LANES)
    return wr, br


def _head_tables(dt_bias, a_log, d_skip):
    def per_group(v):
        return jnp.pad(v.reshape(SSM_GROUPS, HEADS_PER_GROUP), ((0, 0), (0, LANES - HEADS_PER_GROUP)))

    rows = jnp.stack([per_group(dt_bias), per_group(a_log), per_group(d_skip)], axis=1)
    ptab = jnp.pad(rows, ((0, 0), (0, SUBLANES - 3), (0, 0)))
    sel = np.zeros((SSM_GROUPS, LANES, LANES), np.float32)
    for g in range(SSM_GROUPS):
        for j in range(HEADS_PER_GROUP):
            sel[g, HEADS_PER_GROUP * g + j, j] = 1.0
    return ptab, jnp.asarray(sel)


def _prompt_bias_index(tq):
    r = np.arange(tq)[:, None]
    c = np.arange(tq)[None, :]
    tiles = [_bucket_tile(delta * tq + r - c) for delta in range(3)]
    return np.stack(tiles).reshape(-1)


def _sample_bias_index(nq, page, past_len):
    r = np.arange(nq)[:, None]
    c = np.arange(page)[None, :]
    last_page = _bucket_tile(past_len + r - (past_len - page) - c)
    own = np.where(c < nq, _bucket_tile(r - c), -1).astype(np.int32)
    far = _bucket_tile(np.full((nq, page), MAX_DISTANCE + page))
    return np.stack([last_page, own, far]).reshape(-1)


def kernel(x_prompt, x_sample, state_ssm, state_conv, cache_k, cache_v, page_table, ln_g, ln_b, m_w_in,
           m_conv_w, m_conv_b, m_dt_bias, m_a_log, m_d, m_norm_w, m_w_out, kv_w, a_w_q, a_lambda,
           a_subln_w, a_w_o, rel_bias, moe_wg, moe_bg, moe_we, moe_be, moe_w1, moe_w3, moe_w2):
    bp, lp, _ = x_prompt.shape
    bs, ls, _ = x_sample.shape
    n_pages = page_table.shape[1]
    page = cache_k.shape[1]
    past_len = n_pages * page
    assert page == PAGE_SIZE and page >= MAX_DISTANCE and min(ATT_TILE, lp) >= MAX_DISTANCE

    w_in = m_w_in[0]
    w_zx = w_in[:, :SSM_D_INNER + SSM_CONV_DIM].astype(BF16)
    w_dt = jnp.pad(w_in[:, SSM_D_INNER + SSM_CONV_DIM:], ((0, 0), (0, LANES - SSM_HEADS))).astype(BF16)
    ptab, sel = _head_tables(m_dt_bias[0], m_a_log[0], m_d[0])
    conv_w = m_conv_w[0]
    conv_b = m_conv_b[0].reshape(1, SSM_CONV_DIM)
    norm_w = m_norm_w[0].reshape(1, SSM_D_INNER)
    w_out = m_w_out[0].astype(BF16)
    w_q = (a_w_q[0] * (LOG2E * ATT_HEAD_DIM ** -0.5)).astype(BF16)
    w_kv = kv_w.astype(BF16)
    w_kt = kv_w[:, :ATT_QK_DIM].T.astype(BF16)
    w_v = w_kv[:, ATT_QK_DIM:]
    w_o = a_w_o[0].astype(BF16)
    lam_vec = a_lambda[0]
    subln_w = a_subln_w[0].reshape(1, 2 * ATT_HEAD_DIM)
    routers = [_router_params(moe_wg[l], moe_bg[l], moe_we[l], moe_be[l]) for l in range(DEPTH)]
    experts = (moe_w1.reshape(DEPTH * MOE_EXPERTS, D_MODEL, MOE_FF),
               moe_w3.reshape(DEPTH * MOE_EXPERTS, D_MODEL, MOE_FF),
               moe_w2.reshape(DEPTH * MOE_EXPERTS, MOE_FF, D_MODEL))
    lng = ln_g.reshape(DEPTH, 2, 1, D_MODEL)
    lnb = ln_b.reshape(DEPTH, 2, 1, D_MODEL)

    tq = min(ATT_TILE, lp)
    maps = 2 * ATT_HEADS
    idx_p = _prompt_bias_index(tq)
    idx_s = _sample_bias_index(ls, page, past_len)
    n_p = idx_p.shape[0]
    n_s = idx_s.shape[0]
    n_tot = -(-(n_p + n_s) // 8192) * 8192
    idx_all = np.concatenate([idx_p, idx_s, np.zeros((n_tot - n_p - n_s,), np.int32)])
    tiles = _bias_tiles(rel_bias, idx_all)
    bias_p = tiles[:, :n_p].reshape(ATT_HEADS, 2, 3, tq, tq)
    bias_p = (bias_p[:, :, 0:2] - bias_p[:, :, 2:3]) * LOG2E
    bias_p = jnp.transpose(bias_p, (0, 2, 1, 3, 4)).reshape(ATT_HEADS, 2, 2 * tq, tq)
    bias_s = tiles[:, n_p:n_p + n_s].reshape(maps, 3, ls, page)
    bias_s = (bias_s - bias_s[:, 2:3]) * LOG2E
    bias_s = jnp.transpose(bias_s, (1, 0, 2, 3)).reshape(3, maps * ls, page)

    def trunk(x, ssm0, conv0, bsz, seqlen, past):
        t = bsz * seqlen
        xf = x.reshape(t, D_MODEL)
        zx, dt = _proj(xf, [w_zx, w_dt], [BF16, F32], "in_proj")
        y, new_conv, new_ssm = _mamba_core(zx, dt, conv0, ssm0, sel, ptab, conv_w, conv_b, norm_w, bsz, seqlen)
        x1, route, counts = _post_mixer(y, w_out, xf, lng[0, 0], lnb[0, 0], *routers[0], "mamba_out_ln_router")
        x2 = _moe(x1, route, counts, *experts, 0, lng[0, 1], lnb[0, 1], "moe0_combine_ln")
        if past is None:
            kt, ktb, v4, vb, qs = _kvq_prompt(x2, w_kt, w_v, w_q, bsz, seqlen)
            k_new = jnp.transpose(kt.reshape(bsz, maps, ATT_HEAD_DIM, seqlen), (0, 3, 1, 2))
            v_new = v4.reshape(bsz, seqlen, ATT_HEADS, 2 * ATT_HEAD_DIM)
            o = _attn_prompt(qs, ktb, vb, bias_p, lam_vec, subln_w, bsz, seqlen, DEPTH - 1)
        else:
            kv, qs = _proj(x2, [w_kv, w_q], [F32, BF16], "kvq_proj")
            k_new = kv[:, :ATT_QK_DIM].reshape(bsz, seqlen, maps, ATT_HEAD_DIM)
            v_new = kv[:, ATT_QK_DIM:].reshape(bsz, seqlen, ATT_HEADS, 2 * ATT_HEAD_DIM)
            o = _attn_sample(qs, kv, cache_k, cache_v, page_table, bias_s, lam_vec, subln_w, bsz, seqlen, DEPTH - 1)
        x3, route, counts = _post_mixer(o, w_o, x2, lng[1, 0], lnb[1, 0], *routers[1], "attn_out_ln_router")
        x4 = _moe(x3, route, counts, *experts, MOE_EXPERTS, lng[1, 1], lnb[1, 1], "moe1_combine_ln")
        return (x4.reshape(bsz, seqlen, D_MODEL), new_ssm[None], new_conv[None], k_new, v_new)

    ssm0_p = jnp.zeros((bp, SSM_HEADS, SSM_HEAD_DIM, SSM_D_STATE), F32)
    conv0_p = jnp.zeros((bp, SSM_CONV - 1, SSM_CONV_DIM), F32)
    y_p, ssm_p, conv_p, k_p, v_p = trunk(x_prompt, ssm0_p, conv0_p, bp, lp, None)
    y_s, ssm_s, conv_s, k_s, v_s = trunk(x_sample, state_ssm[0], state_conv[0], bs, ls, True)
    return (y_p, y_s, ssm_p, conv_p, k_p, v_p, ssm_s, conv_s, k_s, v_s)
```

```python
import functools
import math

import numpy as np
import jax
import jax.numpy as jnp
from jax import lax
from jax.experimental import pallas as pl
from jax.experimental.pallas import tpu as pltpu

F32 = jnp.float32
BF16 = jnp.bfloat16
HIGHEST = lax.Precision.HIGHEST

D_MODEL = 1024
DEPTH = 2
DN_ALPHA = (2.0 * DEPTH) ** 0.25
LN_EPS = 1e-5
LOG2E = math.log2(math.e)
SSM_D_INNER = 2048
SSM_HEAD_DIM = 64
SSM_HEADS = 32
SSM_GROUPS = 8
SSM_D_STATE = 128
SSM_CONV = 4
SSM_CHUNK = 128
SSM_CONV_DIM = SSM_D_INNER + 2 * SSM_GROUPS * SSM_D_STATE
HEADS_PER_GROUP = SSM_HEADS // SSM_GROUPS
GROUP_WIDTH = HEADS_PER_GROUP * SSM_HEAD_DIM
ATT_HEADS = 8
ATT_HEAD_DIM = 64
ATT_QK_DIM = 1024
ATT_V_DIM = 1024
NUM_BUCKETS = 32
MAX_DISTANCE = 128
MOE_GROUPS = 4
MOE_EXPERTS_PER_GROUP = 8
MOE_EXPERTS = MOE_GROUPS * MOE_EXPERTS_PER_GROUP
MOE_FF = 512
PAGE_SIZE = 128

LANES = 128
SUBLANES = 8
VMEM_LIMIT = 56 * 1024 * 1024

TOKEN_TILE = 256
FFN_TILE = 256
ATT_TILE = 256
MAMBA_SEQS_PER_STEP = 2
SAMPLE_PAGES_PER_STEP = 8


def _cparams(*sem):
    return pltpu.CompilerParams(dimension_semantics=sem, vmem_limit_bytes=VMEM_LIMIT)


def _sigmoid(x):
    return 1.0 / (1.0 + jnp.exp(-x))


def _silu(x):
    return x * _sigmoid(x)


def _softplus(x):
    return jnp.maximum(x, 0.0) + jnp.log(1.0 + jnp.exp(-jnp.abs(x)))


def _layer_norm(u, g, b):
    mu = jnp.mean(u, axis=-1, keepdims=True)
    d = u - mu
    var = jnp.mean(d * d, axis=-1, keepdims=True)
    return d * lax.rsqrt(var + LN_EPS) * g + b


def _dot_nt(a, b):
    return lax.dot_general(a, b, (((1,), (1,)), ((), ())), preferred_element_type=F32)


def _dot_tn(a, b):
    return lax.dot_general(a, b, (((0,), (0,)), ((), ())), preferred_element_type=F32)


def _proj_kernel(x_ref, *refs, n_out, chunk):
    w_refs, o_refs = refs[:n_out], refs[n_out:]
    xb = x_ref[...].astype(BF16)
    for w_ref, o_ref in zip(w_refs, o_refs):
        n = w_ref.shape[1]
        for c0 in range(0, n, chunk):
            c1 = min(n, c0 + chunk)
            o_ref[:, c0:c1] = jnp.dot(xb, w_ref[:, c0:c1], preferred_element_type=F32).astype(o_ref.dtype)


def _proj(x, ws, out_dtypes, name):
    t, k = x.shape
    tm = min(TOKEN_TILE, t)
    in_specs = [pl.BlockSpec((tm, k), lambda i: (i, 0))]
    in_specs += [pl.BlockSpec(w.shape, lambda i: (0, 0)) for w in ws]
    out_specs = [pl.BlockSpec((tm, w.shape[1]), lambda i: (i, 0)) for w in ws]
    out_shape = [jax.ShapeDtypeStruct((t, w.shape[1]), dt) for w, dt in zip(ws, out_dtypes)]
    return pl.pallas_call(
        functools.partial(_proj_kernel, n_out=len(ws), chunk=512),
        grid=(t // tm,),
        in_specs=in_specs,
        out_specs=out_specs,
        out_shape=out_shape,
        compiler_params=_cparams("parallel"),
        name=name,
    )(x, *ws)


def _mamba_kernel(z_ref, x_ref, b_ref, c_ref, dt_ref, ptab_ref,
                  cwx_ref, cwb_ref, cwc_ref, cbx_ref, cbb_ref, cbc_ref,
                  c0x_ref, c0b_ref, c0c_ref, h0_ref, nw_ref,
                  y_ref, cox_ref, cob_ref, coc_ref, hout_ref,
                  sx, sb, sc, hs, *, q, nc, bb):
    c = pl.program_id(1)
    pad = SUBLANES
    hist = SSM_CONV - 1
    gw, ns, hpg = GROUP_WIDTH, SSM_D_STATE, HEADS_PER_GROUP

    @pl.when(c == 0)
    def _():
        sx[:, pad - hist:pad, :] = c0x_ref[...]
        sb[:, pad - hist:pad, :] = c0b_ref[...]
        sc[:, pad - hist:pad, :] = c0c_ref[...]
        hs[...] = h0_ref[...]

    ti = lax.broadcasted_iota(jnp.int32, (q, q), 0)
    si = lax.broadcasted_iota(jnp.int32, (q, q), 1)
    tri = ti >= si
    tri_f = tri.astype(F32)
    lane = lax.broadcasted_iota(jnp.int32, (1, gw), 1)
    shift = (jnp.concatenate([(ti - si == d).astype(BF16) for d in range(hist, 0, -1)], axis=0)
             if q > SUBLANES else None)
    ptab = ptab_ref[...]

    def expand(v, g):
        r = v.shape[0]
        h0 = hpg * g
        out = jnp.broadcast_to(v[:, h0 + hpg - 1:h0 + hpg], (r, gw))
        for j in range(hpg - 2, -1, -1):
            out = jnp.where(lane < SSM_HEAD_DIM * (j + 1), jnp.broadcast_to(v[:, h0 + j:h0 + j + 1], (r, gw)), out)
        return out

    def conv(cur, s, u, c0, c1, w_ref, bias_ref):
        cur_f = cur.astype(F32)
        s[u, pad:pad + q, c0:c1] = cur_f
        rows = q if shift is None else SUBLANES
        acc = bias_ref[:, c0:c1]
        for k in range(SSM_CONV):
            acc = acc + w_ref[k:k + 1, c0:c1] * s[u, pad - hist + k:pad - hist + k + rows, c0:c1]
        if shift is not None:
            rest = bias_ref[:, c0:c1] + w_ref[hist:hist + 1, c0:c1] * cur_f
            moved = jnp.dot(shift, cur, preferred_element_type=F32)
            for k in range(hist):
                rest = rest + w_ref[k:k + 1, c0:c1] * moved[k * q:(k + 1) * q]
            acc = jnp.concatenate([acc, rest[SUBLANES:]], axis=0)
        tail = s[u, pad + q - hist:pad + q, c0:c1]
        s[u, pad - hist:pad, c0:c1] = tail
        return _silu(acc), tail

    tails = []
    for u in range(bb):
        dt_all = _softplus(dt_ref[u] + ptab[0:1, :])
        da = dt_all * (-jnp.exp(ptab[1:2, :]))
        acum = jnp.dot(tri_f, da, precision=HIGHEST, preferred_element_type=F32)
        acum_t = acum.T
        ac_last = acum[q - 1:q, :]
        cd = jnp.exp(ac_last)
        for g in range(SSM_GROUPS):
            x0, x1 = gw * g, gw * (g + 1)
            n0, n1 = ns * g, ns * (g + 1)
            xc, tail_x = conv(x_ref[u, :, x0:x1], sx, u, x0, x1, cwx_ref, cbx_ref)
            bc, tail_b = conv(b_ref[u, :, n0:n1], sb, u, n0, n1, cwb_ref, cbb_ref)
            cc, tail_c = conv(c_ref[u, :, n0:n1], sc, u, n0, n1, cwc_ref, cbc_ref)

            dt_e = expand(dt_all, g)
            ac_e = expand(acum, g)
            ac_last_e = expand(ac_last, g)

            xdt = xc * dt_e
            xdt_b = xdt.astype(BF16)
            bb_ = bc.astype(BF16)
            cb_ = cc.astype(BF16)
            cbm = _dot_nt(cb_, bb_)
            hprev = hs[u, g]
            y = _dot_nt(cb_, hprev.astype(BF16)) * jnp.exp(ac_e)
            zero_b = jnp.zeros_like(xdt_b)
            for j in range(hpg):
                hd = hpg * g + j
                seg = acum[:, hd:hd + 1] - acum_t[hd:hd + 1, :]
                lm = jnp.exp(jnp.where(tri, seg, -jnp.inf))
                wj = (cbm * lm).astype(BF16)
                xm = jnp.where((lane >= SSM_HEAD_DIM * j) & (lane < SSM_HEAD_DIM * (j + 1)), xdt_b, zero_b)
                y = y + jnp.dot(wj, xm, preferred_element_type=F32)

            xw = (xdt * jnp.exp(ac_last_e - ac_e)).astype(BF16)
            st = _dot_tn(xw, bb_)
            for j in range(hpg):
                hd = hpg * g + j
                r0, r1 = SSM_HEAD_DIM * j, SSM_HEAD_DIM * (j + 1)
                hs[u, g, r0:r1, :] = (jnp.broadcast_to(cd[:, hd:hd + 1], (SSM_HEAD_DIM, ns)) * hprev[r0:r1, :]
                                      + st[r0:r1, :])

            y = y + expand(ptab[2:3, :], g) * xc
            zf = z_ref[u, :, x0:x1].astype(F32)
            y = y * _silu(zf)
            ms = jnp.mean(y * y, axis=-1, keepdims=True)
            y_ref[u, :, x0:x1] = (y * lax.rsqrt(ms + LN_EPS) * nw_ref[:, x0:x1]).astype(y_ref.dtype)
            tails.append((u, x0, x1, n0, n1, tail_x, tail_b, tail_c))

    @pl.when(c == nc - 1)
    def _():
        for u, x0, x1, n0, n1, tail_x, tail_b, tail_c in tails:
            cox_ref[u, :, x0:x1] = tail_x
            cob_ref[u, :, n0:n1] = tail_b
            coc_ref[u, :, n0:n1] = tail_c
        hout_ref[...] = hs[...]


def _mamba_core(zx, dt, conv0, ssm0, ptab, conv_w, conv_b, norm_w, bsz, seqlen):
    q = math.gcd(seqlen, SSM_CHUNK)
    nc = seqlen // q
    g = SSM_GROUPS
    bb = 1 if nc > 1 else math.gcd(bsz, MAMBA_SEQS_PER_STEP)
    di, gn = SSM_D_INNER, g * SSM_D_STATE
    zx3 = zx.reshape(bsz, seqlen, zx.shape[-1])
    dt3 = dt.reshape(bsz, seqlen, LANES)
    h0 = ssm0.reshape(bsz, g, GROUP_WIDTH, SSM_D_STATE)
    hist = SSM_CONV - 1
    in_specs = [
        pl.BlockSpec((bb, q, di), lambda b, c: (b, c, 0)),
        pl.BlockSpec((bb, q, di), lambda b, c: (b, c, 1)),
        pl.BlockSpec((bb, q, gn), lambda b, c: (b, c, 2 * di // gn)),
        pl.BlockSpec((bb, q, gn), lambda b, c: (b, c, 2 * di // gn + 1)),
        pl.BlockSpec((bb, q, LANES), lambda b, c: (b, c, 0)),
        pl.BlockSpec((SUBLANES, LANES), lambda b, c: (0, 0)),
        pl.BlockSpec((SSM_CONV, di), lambda b, c: (0, 0)),
        pl.BlockSpec((SSM_CONV, gn), lambda b, c: (0, di // gn)),
        pl.BlockSpec((SSM_CONV, gn), lambda b, c: (0, di // gn + 1)),
        pl.BlockSpec((1, di), lambda b, c: (0, 0)),
        pl.BlockSpec((1, gn), lambda b, c: (0, di // gn)),
        pl.BlockSpec((1, gn), lambda b, c: (0, di // gn + 1)),
        pl.BlockSpec((bb, hist, di), lambda b, c: (b, 0, 0)),
        pl.BlockSpec((bb, hist, gn), lambda b, c: (b, 0, di // gn)),
        pl.BlockSpec((bb, hist, gn), lambda b, c: (b, 0, di // gn + 1)),
        pl.BlockSpec((bb, g, GROUP_WIDTH, SSM_D_STATE), lambda b, c: (b, 0, 0, 0)),
        pl.BlockSpec((1, di), lambda b, c: (0, 0)),
    ]
    out_specs = [
        pl.BlockSpec((bb, q, di), lambda b, c: (b, c, 0)),
        pl.BlockSpec((bb, hist, di), lambda b, c: (b, 0, 0)),
        pl.BlockSpec((bb, hist, gn), lambda b, c: (b, 0, 0)),
        pl.BlockSpec((bb, hist, gn), lambda b, c: (b, 0, 0)),
        pl.BlockSpec((bb, g, GROUP_WIDTH, SSM_D_STATE), lambda b, c: (b, 0, 0, 0)),
    ]
    out_shape = [
        jax.ShapeDtypeStruct((bsz, seqlen, di), BF16),
        jax.ShapeDtypeStruct((bsz, hist, di), F32),
        jax.ShapeDtypeStruct((bsz, hist, gn), F32),
        jax.ShapeDtypeStruct((bsz, hist, gn), F32),
        jax.ShapeDtypeStruct((bsz, g, GROUP_WIDTH, SSM_D_STATE), F32),
    ]
    scratch = [
        pltpu.VMEM((bb, q + SUBLANES, di), F32),
        pltpu.VMEM((bb, q + SUBLANES, gn), F32),
        pltpu.VMEM((bb, q + SUBLANES, gn), F32),
        pltpu.VMEM((bb, g, GROUP_WIDTH, SSM_D_STATE), F32),
    ]
    y, cox, cob, coc, hout = pl.pallas_call(
        functools.partial(_mamba_kernel, q=q, nc=nc, bb=bb),
        grid=(bsz // bb, nc),
        in_specs=in_specs,
        out_specs=out_specs,
        out_shape=out_shape,
        scratch_shapes=scratch,
        compiler_params=_cparams("parallel", "arbitrary"),
        name="mamba_core",
    )(zx3, zx3, zx3, zx3, dt3, ptab, conv_w, conv_w, conv_w, conv_b, conv_b, conv_b,
      conv0, conv0, conv0, h0, norm_w)
    new_conv = jnp.concatenate([cox, cob, coc], axis=-1)
    return (y.reshape(bsz * seqlen, SSM_D_INNER), new_conv,
            hout.reshape(bsz, SSM_HEADS, SSM_HEAD_DIM, SSM_D_STATE))


def _post_mixer_kernel(y_ref, w_ref, xres_ref, g_ref, b_ref, wr_ref, br_ref,
                       x1_ref, route_ref, route_t_ref, cnt_ref, base, *, tm):
    i = pl.program_id(0)

    @pl.when(i == 0)
    def _():
        base[...] = jnp.zeros_like(base)

    h = jnp.dot(y_ref[...], w_ref[...], preferred_element_type=F32)
    x1 = _layer_norm(DN_ALPHA * xres_ref[...] + h, g_ref[...], b_ref[...])
    x1_ref[...] = x1

    x_hi = x1.astype(BF16)
    x_lo = (x1 - x_hi.astype(F32)).astype(BF16)
    both = jnp.dot(x_hi, wr_ref[...], preferred_element_type=F32)
    logits = (both[:, :LANES] + both[:, LANES:]
              + jnp.dot(x_lo, wr_ref[:, :LANES], preferred_element_type=F32) + br_ref[...])
    lane = lax.broadcasted_iota(jnp.int32, (tm, LANES), 1)
    neg = -jnp.inf
    gmask = lane < MOE_GROUPS
    gl = jnp.where(gmask, logits, neg)
    mg = jnp.max(gl, axis=1, keepdims=True)
    grp = jnp.min(jnp.where(gl == mg, lane, LANES), axis=1, keepdims=True)
    pg = 1.0 / jnp.sum(jnp.where(gmask, jnp.exp(logits - mg), 0.0), axis=1, keepdims=True)
    lo = MOE_GROUPS + MOE_EXPERTS_PER_GROUP * grp
    el = jnp.where((lane >= lo) & (lane < lo + MOE_EXPERTS_PER_GROUP), logits, neg)
    v1 = jnp.max(el, axis=1, keepdims=True)
    i1 = jnp.min(jnp.where(el == v1, lane, LANES), axis=1, keepdims=True)
    el2 = jnp.where(lane == i1, neg, el)
    v2 = jnp.max(el2, axis=1, keepdims=True)
    i2 = jnp.min(jnp.where(el2 == v2, lane, LANES), axis=1, keepdims=True)
    t = jnp.exp(v2 - v1)
    wa = pg / (1.0 + t)
    wb = pg * t / (1.0 + t)
    e1 = i1 - MOE_GROUPS
    e2 = i2 - MOE_GROUPS

    oh = ((lane == e1) | (lane == e2)).astype(BF16)
    ri = lax.broadcasted_iota(jnp.int32, (tm, tm), 0)
    ci = lax.broadcasted_iota(jnp.int32, (tm, tm), 1)
    before = jnp.dot((ri > ci).astype(BF16), oh, preferred_element_type=F32) + base[...]
    r1 = jnp.sum(jnp.where(lane == e1, before, 0.0), axis=1, keepdims=True)
    r2 = jnp.sum(jnp.where(lane == e2, before, 0.0), axis=1, keepdims=True)
    new_base = base[...] + jnp.sum(oh.astype(F32), axis=0, keepdims=True)
    base[...] = new_base
    cnt_ref[...] = new_base

    route = jnp.where(lane == 0, e1.astype(F32), 0.0)
    route = jnp.where(lane == 1, e2.astype(F32), route)
    route = jnp.where(lane == 2, r1, route)
    route = jnp.where(lane == 3, r2, route)
    route = jnp.where(lane == 4, wa, route)
    route = jnp.where(lane == 5, wb, route)
    route_ref[...] = route
    route_t_ref[...] = route.T[0:SUBLANES, :]


def _post_mixer(y, w, xres, ln_g, ln_b, wr, br, name):
    t, kin = y.shape
    tm = min(TOKEN_TILE, t)
    return pl.pallas_call(
        functools.partial(_post_mixer_kernel, tm=tm),
        grid=(t // tm,),
        in_specs=[
            pl.BlockSpec((tm, kin), lambda i: (i, 0)),
            pl.BlockSpec((kin, D_MODEL), lambda i: (0, 0)),
            pl.BlockSpec((tm, D_MODEL), lambda i: (i, 0)),
            pl.BlockSpec((1, D_MODEL), lambda i: (0, 0)),
            pl.BlockSpec((1, D_MODEL), lambda i: (0, 0)),
            pl.BlockSpec((D_MODEL, 2 * LANES), lambda i: (0, 0)),
            pl.BlockSpec((1, LANES), lambda i: (0, 0)),
        ],
        out_specs=[
            pl.BlockSpec((tm, D_MODEL), lambda i: (i, 0)),
            pl.BlockSpec((tm, LANES), lambda i: (i, 0)),
            pl.BlockSpec((SUBLANES, tm), lambda i: (0, i)),
            pl.BlockSpec((1, LANES), lambda i: (0, 0)),
        ],
        out_shape=[
            jax.ShapeDtypeStruct((t, D_MODEL), F32),
            jax.ShapeDtypeStruct((t, LANES), F32),
            jax.ShapeDtypeStruct((SUBLANES, t), F32),
            jax.ShapeDtypeStruct((1, LANES), F32),
        ],
        scratch_shapes=[pltpu.VMEM((1, LANES), F32)],
        compiler_params=_cparams("arbitrary"),
        name=name,
    )(y, w, xres, ln_g, ln_b, wr, br)


def _row_copy(src_ref, src_row, dst_ref, dst_row, sem):
    return pltpu.make_async_copy(src_ref.at[pl.ds(src_row, 1)], dst_ref.at[pl.ds(dst_row, 1)], sem)


def _dispatch_kernel(dest_ref, x_ref, xs_ref, sem, *, tm):
    def start(r, carry):
        _row_copy(x_ref, r, xs_ref, dest_ref[0, 0, r], sem).start()
        _row_copy(x_ref, r, xs_ref, dest_ref[0, 0, tm + r], sem).start(priority=1)
        return carry

    lax.fori_loop(0, tm, start, 0, unroll=True)
    for _ in range(2):
        pltpu.make_async_copy(x_ref, xs_ref.at[pl.ds(0, tm)], sem).wait()


def _dispatch(x1, dest3, tm):
    t = x1.shape[0]
    return pl.pallas_call(
        functools.partial(_dispatch_kernel, tm=tm),
        grid=(t // tm,),
        in_specs=[
            pl.BlockSpec((1, 1, 2 * tm), lambda i: (i, 0, 0), memory_space=pltpu.SMEM),
            pl.BlockSpec((tm, D_MODEL), lambda i: (i, 0)),
        ],
        out_specs=pl.BlockSpec(memory_space=pl.ANY),
        out_shape=jax.ShapeDtypeStruct((2 * t, D_MODEL), F32),
        scratch_shapes=[pltpu.SemaphoreType.DMA(())],
        compiler_params=_cparams("arbitrary"),
        name="moe_dispatch",
    )(dest3, x1)


def _ffn_kernel(wt_ref, we_ref, wlo_ref, whi_ref, wfirst_ref,
                xs_ref, w1_ref, w3_ref, w2_ref, ys_ref, *, tf):
    w = pl.program_id(0)
    lo = wlo_ref[w]
    hi = whi_ref[w]

    @pl.when(wfirst_ref[w] == 1)
    def _():
        ys_ref[...] = jnp.zeros_like(ys_ref)

    @pl.when(hi > lo)
    def _():
        rows = wt_ref[w] * tf + lax.broadcasted_iota(jnp.int32, (tf, 1), 0)
        mask = (rows >= lo) & (rows < hi)
        xb = xs_ref[...].astype(BF16)
        a = jnp.dot(xb, w1_ref[0].astype(BF16), preferred_element_type=F32)
        b = jnp.dot(xb, w3_ref[0].astype(BF16), preferred_element_type=F32)
        hmid = (_silu(a) * b).astype(BF16)
        y = jnp.dot(hmid, w2_ref[0].astype(BF16), preferred_element_type=F32)
        ys_ref[...] = jnp.where(mask, y, ys_ref[...])


def _ffn(xs, w1, w3, w2, work, tf):
    n = xs.shape[0]
    n_work = work[0].shape[0]
    grid_spec = pltpu.PrefetchScalarGridSpec(
        num_scalar_prefetch=5,
        grid=(n_work,),
        in_specs=[
            pl.BlockSpec((tf, D_MODEL), lambda w, wt, we, wlo, whi, wf: (wt[w], 0)),
            pl.BlockSpec((1, D_MODEL, MOE_FF), lambda w, wt, we, wlo, whi, wf: (we[w], 0, 0)),
            pl.BlockSpec((1, D_MODEL, MOE_FF), lambda w, wt, we, wlo, whi, wf: (we[w], 0, 0)),
            pl.BlockSpec((1, MOE_FF, D_MODEL), lambda w, wt, we, wlo, whi, wf: (we[w], 0, 0)),
        ],
        out_specs=pl.BlockSpec((tf, D_MODEL), lambda w, wt, we, wlo, whi, wf: (wt[w], 0)),
    )
    return pl.pallas_call(
        functools.partial(_ffn_kernel, tf=tf),
        grid_spec=grid_spec,
        out_shape=jax.ShapeDtypeStruct((n, D_MODEL), F32),
        compiler_params=_cparams("arbitrary"),
        name="moe_ffn",
    )(*work, xs, w1, w3, w2)


def _combine_kernel(dest_ref, x1_ref, route_ref, g_ref, b_ref, ys_ref, o_ref, buf0, buf1, sem, *, tm):
    def start(r, carry):
        _row_copy(ys_ref, dest_ref[0, 0, r], buf0, r, sem).start()
        _row_copy(ys_ref, dest_ref[0, 0, tm + r], buf1, r, sem).start(priority=1)
        return carry

    lax.fori_loop(0, tm, start, 0, unroll=True)
    pltpu.make_async_copy(ys_ref.at[pl.ds(0, tm)], buf0, sem).wait()
    pltpu.make_async_copy(ys_ref.at[pl.ds(0, tm)], buf1, sem).wait()

    route = route_ref[...]
    y = route[:, 4:5] * buf0[...] + route[:, 5:6] * buf1[...]
    o_ref[...] = _layer_norm(DN_ALPHA * x1_ref[...] + y, g_ref[...], b_ref[...])


def _combine(x1, route, ys, dest3, ln_g, ln_b, tm, name):
    t = x1.shape[0]
    return pl.pallas_call(
        functools.partial(_combine_kernel, tm=tm),
        grid=(t // tm,),
        in_specs=[
            pl.BlockSpec((1, 1, 2 * tm), lambda i: (i, 0, 0), memory_space=pltpu.SMEM),
            pl.BlockSpec((tm, D_MODEL), lambda i: (i, 0)),
            pl.BlockSpec((tm, LANES), lambda i: (i, 0)),
            pl.BlockSpec((1, D_MODEL), lambda i: (0, 0)),
            pl.BlockSpec((1, D_MODEL), lambda i: (0, 0)),
            pl.BlockSpec(memory_space=pl.ANY),
        ],
        out_specs=pl.BlockSpec((tm, D_MODEL), lambda i: (i, 0)),
        out_shape=jax.ShapeDtypeStruct((t, D_MODEL), F32),
        scratch_shapes=[
            pltpu.VMEM((tm, D_MODEL), F32),
            pltpu.VMEM((tm, D_MODEL), F32),
            pltpu.SemaphoreType.DMA(()),
        ],
        compiler_params=_cparams("arbitrary"),
        name=name,
    )(dest3, x1, route, ln_g, ln_b, ys)


def _moe(x1, route, route_t, counts, w1, w3, w2, expert_base, ln_g, ln_b, name):
    t = x1.shape[0]
    n = 2 * t
    tm = min(TOKEN_TILE, t)
    tf = min(FFN_TILE, n)
    cnt = counts[0, :MOE_EXPERTS].astype(jnp.int32)
    offs = jnp.concatenate([jnp.zeros((1,), jnp.int32), jnp.cumsum(cnt)])
    e = route_t[0:2].astype(jnp.int32)
    r = route_t[2:4].astype(jnp.int32)
    dest = offs[e] + r
    dest3 = jnp.transpose(dest.reshape(2, t // tm, tm), (1, 0, 2)).reshape(t // tm, 1, 2 * tm)

    n_tiles = n // tf
    n_work = n_tiles + MOE_EXPERTS
    first_tile = offs[:-1] // tf
    last_tile = (offs[1:] - 1) // tf
    ntile_e = jnp.where(cnt > 0, last_tile - first_tile + 1, 0)
    wstart = jnp.concatenate([jnp.zeros((1,), jnp.int32), jnp.cumsum(ntile_e)])
    total = wstart[-1]
    widx = jnp.arange(n_work, dtype=jnp.int32)
    we = jnp.clip(jnp.searchsorted(wstart, widx, side="right") - 1, 0, MOE_EXPERTS - 1).astype(jnp.int32)
    wt = first_tile[we] + (widx - wstart[we])
    valid = widx < total
    wt = jnp.where(valid, wt, n_tiles - 1).astype(jnp.int32)
    wlo = jnp.where(valid, offs[we], 0).astype(jnp.int32)
    whi = jnp.where(valid, offs[we + 1], 0).astype(jnp.int32)
    last_e = we[jnp.maximum(total - 1, 0)]
    we = jnp.where(valid, we, last_e).astype(jnp.int32)
    wfirst = jnp.concatenate([jnp.ones((1,), jnp.int32), (wt[1:] != wt[:-1]).astype(jnp.int32)])

    xs = _dispatch(x1, dest3, tm)
    ys = _ffn(xs, w1, w3, w2, (wt, we + expert_base, wlo, whi, wfirst), tf)
    return _combine(x1, route, ys, dest3, ln_g, ln_b, tm, name)


def _bias_kernel(idx_ref, rbt_ref, o_ref, *, width):
    idx = idx_ref[...]
    bucket = lax.broadcasted_iota(jnp.int32, (NUM_BUCKETS, width), 0)
    onehot = (bucket == idx).astype(F32)
    vals = jnp.dot(rbt_ref[...], onehot, precision=HIGHEST, preferred_element_type=F32)
    o_ref[...] = jnp.where(idx < 0, -jnp.inf, vals)


def _bias_tiles(rel_bias, idx_np):
    n = idx_np.shape[0]
    width = 8192
    assert n % width == 0
    nh = rel_bias.shape[1]
    return pl.pallas_call(
        functools.partial(_bias_kernel, width=width),
        grid=(n // width,),
        in_specs=[
            pl.BlockSpec((1, width), lambda i: (0, i)),
            pl.BlockSpec((nh, NUM_BUCKETS), lambda i: (0, 0)),
        ],
        out_specs=pl.BlockSpec((nh, width), lambda i: (0, i)),
        out_shape=jax.ShapeDtypeStruct((nh, n), F32),
        compiler_params=_cparams("parallel"),
        name="rel_bias_tiles",
    )(jnp.asarray(idx_np.reshape(1, n)), rel_bias.T)


def _bucket_of_distance(dist):
    n = np.maximum(dist, 0)
    max_exact = NUM_BUCKETS // 2
    nf = np.maximum(n, 1).astype(np.float32)
    large = max_exact + (np.log(nf / np.float32(max_exact)) / np.float32(math.log(MAX_DISTANCE / max_exact))
                         * np.float32(NUM_BUCKETS - max_exact)).astype(np.int32)
    large = np.minimum(large, NUM_BUCKETS - 1)
    return np.where(n < max_exact, n, large).astype(np.int32)


def _bucket_tile(dist):
    return np.where(dist >= 0, _bucket_of_distance(dist), -1).astype(np.int32)


def _kvq_prompt_kernel(x_ref, wkt_ref, wv_ref, wq_ref, kt_ref, ktb_ref, v_ref, vb_ref, q_ref, *, chunk):
    xb = x_ref[...].astype(BF16)
    hd2 = 2 * ATT_HEAD_DIM
    for c0 in range(0, ATT_QK_DIM, chunk):
        kt = _dot_nt(wkt_ref[c0:c0 + chunk, :], xb)
        kt_ref[0, c0:c0 + chunk, :] = kt
        ktb_ref[0, 0, c0:c0 + chunk, :] = kt.astype(BF16)
    for c0 in range(0, ATT_V_DIM, chunk):
        v = jnp.dot(xb, wv_ref[:, c0:c0 + chunk], preferred_element_type=F32)
        vb_ref[:, c0:c0 + chunk] = v.astype(BF16)
        for h0 in range(0, chunk, hd2):
            v_ref[:, (c0 + h0) // hd2, :] = v[:, h0:h0 + hd2]
    for c0 in range(0, ATT_QK_DIM, chunk):
        q_ref[:, c0:c0 + chunk] = jnp.dot(xb, wq_ref[:, c0:c0 + chunk], preferred_element_type=F32).astype(BF16)


def _kvq_prompt(x, wkt, wv, wq, bsz, seqlen):
    t = bsz * seqlen
    tm = min(ATT_TILE, seqlen)
    nkb = seqlen // tm
    hd2 = 2 * ATT_HEAD_DIM
    full = lambda i: (0, 0)
    return pl.pallas_call(
        functools.partial(_kvq_prompt_kernel, chunk=512),
        grid=(t // tm,),
        in_specs=[
            pl.BlockSpec((tm, D_MODEL), lambda i: (i, 0)),
            pl.BlockSpec(wkt.shape, full),
            pl.BlockSpec(wv.shape, full),
            pl.BlockSpec(wq.shape, full),
        ],
        out_specs=[
            pl.BlockSpec((1, ATT_QK_DIM, tm), lambda i: (i // nkb, 0, i % nkb)),
            pl.BlockSpec((1, 1, ATT_QK_DIM, tm), lambda i: (i // nkb, i % nkb, 0, 0)),
            pl.BlockSpec((tm, ATT_HEADS, hd2), lambda i: (i, 0, 0)),
            pl.BlockSpec((tm, ATT_V_DIM), lambda i: (i, 0)),
            pl.BlockSpec((tm, ATT_QK_DIM), lambda i: (i, 0)),
        ],
        out_shape=[
            jax.ShapeDtypeStruct((bsz, ATT_QK_DIM, seqlen), F32),
            jax.ShapeDtypeStruct((bsz, nkb, ATT_QK_DIM, tm), BF16),
            jax.ShapeDtypeStruct((t, ATT_HEADS, hd2), F32),
            jax.ShapeDtypeStruct((t, ATT_V_DIM), BF16),
            jax.ShapeDtypeStruct((t, ATT_QK_DIM), BF16),
        ],
        compiler_params=_cparams("parallel"),
        name="kvq_proj_prompt",
    )(x, wkt, wv, wq)


def _lambda_value(lam_ref, layer_idx):
    lv = lam_ref[...]
    s1 = jnp.sum(lv[0:1, :] * lv[1:2, :], axis=1, keepdims=True)
    s2 = jnp.sum(lv[2:3, :] * lv[3:4, :], axis=1, keepdims=True)
    lam_init = 0.8 - 0.6 * math.exp(-0.3 * layer_idx)
    return jnp.exp(s1) - jnp.exp(s2) + lam_init, lam_init


def _attn_prompt_kernel(q_ref, kt_ref, v_ref, bias_ref, lam_ref, sw_ref, o_ref, *, tq, nq, layer_idx):
    lane = lax.broadcasted_iota(jnp.int32, (tq, 2 * ATT_HEAD_DIM), 1)
    lam, lam_init = _lambda_value(lam_ref, layer_idx)
    for qi in range(nq):
        qb = q_ref[0, qi * tq:(qi + 1) * tq, :]
        zero = jnp.zeros_like(qb)
        q2 = jnp.concatenate([jnp.where(lane < ATT_HEAD_DIM, qb, zero),
                              jnp.where(lane >= ATT_HEAD_DIM, qb, zero)], axis=0)
        m = l = acc = None
        for j in range(qi + 1):
            s = jnp.dot(q2, kt_ref[0, j], preferred_element_type=F32)
            if qi - j < 2:
                s = s + bias_ref[0, qi - j]
            vj = v_ref[0, j * tq:(j + 1) * tq, :]
            bm = jnp.max(s, axis=1, keepdims=True)
            if j == 0:
                m = bm
                p = jnp.exp2(s - m)
                l = jnp.sum(p, axis=1, keepdims=True)
                acc = jnp.dot(p.astype(BF16), vj, preferred_element_type=F32)
            else:
                m_new = jnp.maximum(m, bm)
                alpha = jnp.exp2(m - m_new)
                p = jnp.exp2(s - m_new)
                l = alpha * l + jnp.sum(p, axis=1, keepdims=True)
                acc = alpha * acc + jnp.dot(p.astype(BF16), vj, preferred_element_type=F32)
                m = m_new
        o = acc / l
        d = o[:tq] - lam * o[tq:]
        ms = jnp.mean(d * d, axis=-1, keepdims=True)
        o_ref[0, qi * tq:(qi + 1) * tq, :] = (
            d * lax.rsqrt(ms + LN_EPS) * sw_ref[...] * (1.0 - lam_init)).astype(o_ref.dtype)


def _attn_prompt(q, ktb, vb, bias, lam_vec, subln_w, bsz, seqlen, layer_idx):
    tq = min(ATT_TILE, seqlen)
    nq = seqlen // tq
    hd2 = 2 * ATT_HEAD_DIM
    q3 = q.reshape(bsz, seqlen, ATT_QK_DIM)
    v3 = vb.reshape(bsz, seqlen, ATT_V_DIM)
    out = pl.pallas_call(
        functools.partial(_attn_prompt_kernel, tq=tq, nq=nq, layer_idx=layer_idx),
        grid=(bsz, ATT_HEADS),
        in_specs=[
            pl.BlockSpec((1, seqlen, hd2), lambda b, h: (b, 0, h)),
            pl.BlockSpec((1, nq, hd2, tq), lambda b, h: (b, 0, h, 0)),
            pl.BlockSpec((1, seqlen, hd2), lambda b, h: (b, 0, h)),
            pl.BlockSpec((1, 2, 2 * tq, tq), lambda b, h: (h, 0, 0, 0)),
            pl.BlockSpec(lam_vec.shape, lambda b, h: (0, 0)),
            pl.BlockSpec((1, hd2), lambda b, h: (0, 0)),
        ],
        out_specs=pl.BlockSpec((1, seqlen, hd2), lambda b, h: (b, 0, h)),
        out_shape=jax.ShapeDtypeStruct((bsz, seqlen, ATT_V_DIM), BF16),
        compiler_params=_cparams("parallel", "parallel"),
        name="attn_prompt",
    )(q3, ktb, v3, bias, lam_vec, subln_w)
    return out.reshape(bsz * seqlen, ATT_V_DIM)


def _attn_sample_kernel(pt_ref, q_ref, *refs, nq, n_steps, pp, layer_idx):
    kt_refs = refs[:pp]
    vm_refs = refs[pp:2 * pp]
    (kn_ref, vn_ref, bias_ref, exp_ref, pmask_ref, lam_ref, sw_ref,
     o_ref, qbd, m_s, l_s, acc_s, kpad, vpad) = refs[2 * pp:]
    j = pl.program_id(1)
    maps = 2 * ATT_HEADS
    rows = maps * nq
    hd2 = 2 * ATT_HEAD_DIM
    page = PAGE_SIZE

    @pl.when(j == 0)
    def _():
        m_s[...] = jnp.full_like(m_s, -jnp.inf)
        l_s[...] = jnp.zeros_like(l_s)
        acc_s[...] = jnp.zeros_like(acc_s)
        qt = jnp.concatenate([q_ref[0].astype(F32)] * maps, axis=0)
        ri = lax.broadcasted_iota(jnp.int32, (rows, ATT_QK_DIM), 0)
        ci = lax.broadcasted_iota(jnp.int32, (rows, ATT_QK_DIM), 1)
        keep = (ri // nq) == (ci // ATT_HEAD_DIM)
        qbd[...] = jnp.where(keep, qt, 0.0).astype(qbd.dtype)

    def softmax_update(s):
        m = m_s[...]
        m_new = jnp.maximum(m, jnp.max(s, axis=1, keepdims=True))
        alpha = jnp.exp2(m - m_new)
        p = jnp.exp2(s - m_new)
        l_s[...] = alpha * l_s[...] + jnp.sum(p, axis=1, keepdims=True)
        m_s[...] = m_new
        return alpha, p.astype(BF16)

    def past_pages():
        q2 = qbd[...]
        s_parts = []
        for u in range(pp):
            kt = kt_refs[u][0].reshape(ATT_QK_DIM, page).astype(BF16)
            s_u = jnp.dot(q2, kt, preferred_element_type=F32)
            if u == pp - 1:
                s_u = s_u + bias_ref[jnp.where(j == n_steps - 1, 0, 2)]
            s_parts.append(s_u)
        alpha, pb = softmax_update(jnp.concatenate(s_parts, axis=1))
        pv = jnp.zeros((rows, hd2), F32)
        for u in range(pp):
            pe = jnp.dot(pb[:, page * u:page * (u + 1)], exp_ref[...], preferred_element_type=F32)
            pe = pe.astype(BF16) * pmask_ref[...]
            vm = vm_refs[u][0].reshape(page * ATT_HEADS, hd2).astype(BF16)
            pv = pv + jnp.dot(pe, vm, preferred_element_type=F32)
        acc_s[...] = alpha * acc_s[...] + pv

    past_pages()

    @pl.when(j == n_steps - 1)
    def _():
        kpad[...] = jnp.zeros_like(kpad)
        vpad[...] = jnp.zeros_like(vpad)
        kpad[0:nq, :] = kn_ref[0]
        vpad[0:nq, :] = vn_ref[0]
        s = _dot_nt(qbd[...], kpad[...].astype(BF16)) + bias_ref[1]
        alpha, pb = softmax_update(s)
        for h in range(ATT_HEADS):
            r0, r1 = 2 * nq * h, 2 * nq * (h + 1)
            pvh = jnp.dot(pb[r0:r1, :], vpad[:, hd2 * h:hd2 * (h + 1)].astype(BF16), preferred_element_type=F32)
            acc_s[r0:r1, :] = alpha[r0:r1, :] * acc_s[r0:r1, :] + pvh
        o = acc_s[...] / l_s[...]
        lam, lam_init = _lambda_value(lam_ref, layer_idx)
        for h in range(ATT_HEADS):
            r0 = 2 * nq * h
            d = o[r0:r0 + nq, :] - lam * o[r0 + nq:r0 + 2 * nq, :]
            ms = jnp.mean(d * d, axis=-1, keepdims=True)
            o_ref[0, :, hd2 * h:hd2 * (h + 1)] = (
                d * lax.rsqrt(ms + LN_EPS) * sw_ref[...] * (1.0 - lam_init)).astype(o_ref.dtype)


def _attn_sample(q, kv, cache_k, cache_v, page_table, bias, lam_vec, subln_w, bsz, nq, layer_idx):
    n_pages = page_table.shape[1]
    page = cache_k.shape[1]
    maps = 2 * ATT_HEADS
    rows = maps * nq
    hd2 = 2 * ATT_HEAD_DIM
    pp = math.gcd(n_pages, SAMPLE_PAGES_PER_STEP)
    n_steps = n_pages // pp
    q3 = q.reshape(bsz, nq, ATT_QK_DIM)
    kv3 = kv.reshape(bsz, nq, ATT_QK_DIM + ATT_V_DIM)
    cache_kt = jnp.transpose(cache_k, (0, 2, 3, 1))
    kk = np.arange(page)[:, None]
    cc = np.arange(page * ATT_HEADS)[None, :]
    expand = jnp.asarray((cc // ATT_HEADS == kk).astype(np.float32), dtype=BF16)
    rr = np.arange(rows)[:, None]
    pmask = jnp.asarray(((rr // (2 * nq)) == (cc % ATT_HEADS)).astype(np.float32), dtype=BF16)

    def page_map(u):
        return lambda b, j, pt: (pt[b, j * pp + u], 0, 0, 0)

    in_specs = [pl.BlockSpec((1, nq, ATT_QK_DIM), lambda b, j, pt: (b, 0, 0))]
    in_specs += [pl.BlockSpec((1, maps, ATT_HEAD_DIM, page), page_map(u)) for u in range(pp)]
    in_specs += [pl.BlockSpec((1, page, ATT_HEADS, hd2), page_map(u)) for u in range(pp)]
    in_specs += [
        pl.BlockSpec((1, nq, ATT_QK_DIM), lambda b, j, pt: (b, 0, 0)),
        pl.BlockSpec((1, nq, ATT_V_DIM), lambda b, j, pt: (b, 0, 1)),
        pl.BlockSpec((3, rows, page), lambda b, j, pt: (0, 0, 0)),
        pl.BlockSpec((page, page * ATT_HEADS), lambda b, j, pt: (0, 0)),
        pl.BlockSpec((rows, page * ATT_HEADS), lambda b, j, pt: (0, 0)),
        pl.BlockSpec(lam_vec.shape, lambda b, j, pt: (0, 0)),
        pl.BlockSpec((1, hd2), lambda b, j, pt: (0, 0)),
    ]
    grid_spec = pltpu.PrefetchScalarGridSpec(
        num_scalar_prefetch=1,
        grid=(bsz, n_steps),
        in_specs=in_specs,
        out_specs=pl.BlockSpec((1, nq, ATT_V_DIM), lambda b, j, pt: (b, 0, 0)),
        scratch_shapes=[
            pltpu.VMEM((rows, ATT_QK_DIM), BF16),
            pltpu.VMEM((rows, 1), F32),
            pltpu.VMEM((rows, 1), F32),
            pltpu.VMEM((rows, hd2), F32),
            pltpu.VMEM((page, ATT_QK_DIM), F32),
            pltpu.VMEM((page, ATT_V_DIM), F32),
        ],
    )
    out = pl.pallas_call(
        functools.partial(_attn_sample_kernel, nq=nq, n_steps=n_steps, pp=pp, layer_idx=layer_idx),
        grid_spec=grid_spec,
        out_shape=jax.ShapeDtypeStruct((bsz, nq, ATT_V_DIM), BF16),
        compiler_params=_cparams("parallel", "arbitrary"),
        name="attn_sample",
    )(page_table, q3, *([cache_kt] * pp), *([cache_v] * pp), kv3, kv3, bias, expand, pmask, lam_vec, subln_w)
    return out.reshape(bsz * nq, ATT_V_DIM)


def _router_params(wg, bg, we, be):
    wexp = jnp.transpose(we, (1, 0, 2)).reshape(D_MODEL, MOE_EXPERTS)
    wr = jnp.concatenate([wg, wexp], axis=1)
    wr = jnp.pad(wr, ((0, 0), (0, LANES - wr.shape[1])))
    w_hi = wr.astype(BF16)
    w_lo = (wr - w_hi.astype(F32)).astype(BF16)
    wr = jnp.concatenate([w_hi, w_lo], axis=1)
    br = jnp.concatenate([bg, be.reshape(MOE_EXPERTS)])
    br = jnp.pad(br, (0, LANES - br.shape[0])).reshape(1, LANES)
    return wr, br


def _head_tables(dt_bias, a_log, d_skip):
    rows = jnp.stack([dt_bias, a_log, d_skip])
    return jnp.pad(rows, ((0, SUBLANES - 3), (0, LANES - SSM_HEADS)))


def _prompt_bias_index(tq):
    r = np.arange(tq)[:, None]
    c = np.arange(tq)[None, :]
    tiles = [_bucket_tile(delta * tq + r - c) for delta in range(3)]
    return np.stack(tiles).reshape(-1)


def _sample_bias_index(nq, page, past_len):
    r = np.arange(nq)[:, None]
    c = np.arange(page)[None, :]
    last_page = _bucket_tile(past_len + r - (past_len - page) - c)
    own = np.where(c < nq, _bucket_tile(r - c), -1).astype(np.int32)
    far = _bucket_tile(np.full((nq, page), MAX_DISTANCE + page))
    return np.stack([last_page, own, far]).reshape(-1)


def kernel(x_prompt, x_sample, state_ssm, state_conv, cache_k, cache_v, page_table, ln_g, ln_b, m_w_in,
           m_conv_w, m_conv_b, m_dt_bias, m_a_log, m_d, m_norm_w, m_w_out, kv_w, a_w_q, a_lambda,
           a_subln_w, a_w_o, rel_bias, moe_wg, moe_bg, moe_we, moe_be, moe_w1, moe_w3, moe_w2):
    bp, lp, _ = x_prompt.shape
    bs, ls, _ = x_sample.shape
    n_pages = page_table.shape[1]
    page = cache_k.shape[1]
    past_len = n_pages * page
    assert page == PAGE_SIZE and page >= MAX_DISTANCE and min(ATT_TILE, lp) >= MAX_DISTANCE

    w_in = m_w_in[0]
    w_zx = w_in[:, :SSM_D_INNER + SSM_CONV_DIM].astype(BF16)
    w_dt = jnp.pad(w_in[:, SSM_D_INNER + SSM_CONV_DIM:], ((0, 0), (0, LANES - SSM_HEADS))).astype(BF16)
    ptab = _head_tables(m_dt_bias[0], m_a_log[0], m_d[0])
    conv_w = m_conv_w[0]
    conv_b = m_conv_b[0].reshape(1, SSM_CONV_DIM)
    norm_w = m_norm_w[0].reshape(1, SSM_D_INNER)
    w_out = m_w_out[0].astype(BF16)
    w_q = (a_w_q[0] * (LOG2E * ATT_HEAD_DIM ** -0.5)).astype(BF16)
    w_kv = kv_w.astype(BF16)
    w_kt = kv_w[:, :ATT_QK_DIM].T.astype(BF16)
    w_v = w_kv[:, ATT_QK_DIM:]
    w_o = a_w_o[0].astype(BF16)
    lam_vec = a_lambda[0]
    subln_w = a_subln_w[0].reshape(1, 2 * ATT_HEAD_DIM)
    routers = [_router_params(moe_wg[l], moe_bg[l], moe_we[l], moe_be[l]) for l in range(DEPTH)]
    experts = (moe_w1.reshape(DEPTH * MOE_EXPERTS, D_MODEL, MOE_FF),
               moe_w3.reshape(DEPTH * MOE_EXPERTS, D_MODEL, MOE_FF),
               moe_w2.reshape(DEPTH * MOE_EXPERTS, MOE_FF, D_MODEL))
    lng = ln_g.reshape(DEPTH, 2, 1, D_MODEL)
    lnb = ln_b.reshape(DEPTH, 2, 1, D_MODEL)

    tq = min(ATT_TILE, lp)
    maps = 2 * ATT_HEADS
    idx_p = _prompt_bias_index(tq)
    idx_s = _sample_bias_index(ls, page, past_len)
    n_p = idx_p.shape[0]
    n_s = idx_s.shape[0]
    n_tot = -(-(n_p + n_s) // 8192) * 8192
    idx_all = np.concatenate([idx_p, idx_s, np.zeros((n_tot - n_p - n_s,), np.int32)])
    tiles = _bias_tiles(rel_bias, idx_all)
    bias_p = tiles[:, :n_p].reshape(ATT_HEADS, 2, 3, tq, tq)
    bias_p = (bias_p[:, :, 0:2] - bias_p[:, :, 2:3]) * LOG2E
    bias_p = jnp.transpose(bias_p, (0, 2, 1, 3, 4)).reshape(ATT_HEADS, 2, 2 * tq, tq)
    bias_s = tiles[:, n_p:n_p + n_s].reshape(maps, 3, ls, page)
    bias_s = (bias_s - bias_s[:, 2:3]) * LOG2E
    bias_s = jnp.transpose(bias_s, (1, 0, 2, 3)).reshape(3, maps * ls, page)

    def trunk(x, ssm0, conv0, bsz, seqlen, past):
        t = bsz * seqlen
        xf = x.reshape(t, D_MODEL)
        zx, dt = _proj(xf, [w_zx, w_dt], [BF16, F32], "in_proj")
        y, new_conv, new_ssm = _mamba_core(zx, dt, conv0, ssm0, ptab, conv_w, conv_b, norm_w, bsz, seqlen)
        x1, route, route_t, counts = _post_mixer(y, w_out, xf, lng[0, 0], lnb[0, 0], *routers[0], "mamba_out_ln_router")
        x2 = _moe(x1, route, route_t, counts, *experts, 0, lng[0, 1], lnb[0, 1], "moe0_combine_ln")
        if past is None:
            kt, ktb, v4, vb, qs = _kvq_prompt(x2, w_kt, w_v, w_q, bsz, seqlen)
            k_new = jnp.transpose(kt.reshape(bsz, maps, ATT_HEAD_DIM, seqlen), (0, 3, 1, 2))
            v_new = v4.reshape(bsz, seqlen, ATT_HEADS, 2 * ATT_HEAD_DIM)
            o = _attn_prompt(qs, ktb, vb, bias_p, lam_vec, subln_w, bsz, seqlen, DEPTH - 1)
        else:
            kv, qs = _proj(x2, [w_kv, w_q], [F32, BF16], "kvq_proj")
            k_new = kv[:, :ATT_QK_DIM].reshape(bsz, seqlen, maps, ATT_HEAD_DIM)
            v_new = kv[:, ATT_QK_DIM:].reshape(bsz, seqlen, ATT_HEADS, 2 * ATT_HEAD_DIM)
            o = _attn_sample(qs, kv, cache_k, cache_v, page_table, bias_s, lam_vec, subln_w, bsz, seqlen, DEPTH - 1)
        x3, route, route_t, counts = _post_mixer(o, w_o, x2, lng[1, 0], lnb[1, 0], *routers[1], "attn_out_ln_router")
        x4 = _moe(x3, route, route_t, counts, *experts, MOE_EXPERTS, lng[1, 1], lnb[1, 1], "moe1_combine_ln")
        return (x4.reshape(bsz, seqlen, D_MODEL), new_ssm[None], new_conv[None], k_new, v_new)

    ssm0_p = jnp.zeros((bp, SSM_HEADS, SSM_HEAD_DIM, SSM_D_STATE), F32)
    conv0_p = jnp.zeros((bp, SSM_CONV - 1, SSM_CONV_DIM), F32)
    y_p, ssm_p, conv_p, k_p, v_p = trunk(x_prompt, ssm0_p, conv0_p, bp, lp, None)
    y_s, ssm_s, conv_s, k_s, v_s = trunk(x_sample, state_ssm[0], state_conv[0], bs, ls, True)
    return (y_p, y_s, ssm_p, conv_p, k_p, v_p, ssm_s, conv_s, k_s, v_s)
```

```python
import functools
import math

import numpy as np
import jax
import jax.numpy as jnp
from jax import lax
from jax.experimental import pallas as pl
from jax.experimental.pallas import tpu as pltpu

F32 = jnp.float32
BF16 = jnp.bfloat16
HIGHEST = lax.Precision.HIGHEST

D_MODEL = 1024
DEPTH = 2
DN_ALPHA = (2.0 * DEPTH) ** 0.25
LN_EPS = 1e-5
LOG2E = math.log2(math.e)
SSM_D_INNER = 2048
SSM_HEAD_DIM = 64
SSM_HEADS = 32
SSM_GROUPS = 8
SSM_D_STATE = 128
SSM_CONV = 4
SSM_CHUNK = 128
SSM_CONV_DIM = SSM_D_INNER + 2 * SSM_GROUPS * SSM_D_STATE
HEADS_PER_GROUP = SSM_HEADS // SSM_GROUPS
GROUP_WIDTH = HEADS_PER_GROUP * SSM_HEAD_DIM
ATT_HEADS = 8
ATT_HEAD_DIM = 64
ATT_QK_DIM = 1024
ATT_V_DIM = 1024
NUM_BUCKETS = 32
MAX_DISTANCE = 128
MOE_GROUPS = 4
MOE_EXPERTS_PER_GROUP = 8
MOE_EXPERTS = MOE_GROUPS * MOE_EXPERTS_PER_GROUP
MOE_FF = 512
PAGE_SIZE = 128

LANES = 128
SUBLANES = 8
VMEM_LIMIT = 56 * 1024 * 1024

TOKEN_TILE = 256
FFN_TILE = 256
ATT_TILE = 256
MAMBA_SEQS_PER_STEP = 2
SAMPLE_PAGES_PER_STEP = 8


def _cparams(*sem):
    return pltpu.CompilerParams(dimension_semantics=sem, vmem_limit_bytes=VMEM_LIMIT)


def _sigmoid(x):
    return 1.0 / (1.0 + jnp.exp(-x))


def _silu(x):
    return x * _sigmoid(x)


def _softplus(x):
    return jnp.maximum(x, 0.0) + jnp.log(1.0 + jnp.exp(-jnp.abs(x)))


def _layer_norm(u, g, b):
    mu = jnp.mean(u, axis=-1, keepdims=True)
    d = u - mu
    var = jnp.mean(d * d, axis=-1, keepdims=True)
    return d * lax.rsqrt(var + LN_EPS) * g + b


def _dot_nt(a, b):
    return lax.dot_general(a, b, (((1,), (1,)), ((), ())), preferred_element_type=F32)


def _dot_tn(a, b):
    return lax.dot_general(a, b, (((0,), (0,)), ((), ())), preferred_element_type=F32)


def _proj_kernel(x_ref, *refs, n_out, chunk):
    w_refs, o_refs = refs[:n_out], refs[n_out:]
    xb = x_ref[...].astype(BF16)
    for w_ref, o_ref in zip(w_refs, o_refs):
        n = w_ref.shape[1]
        for c0 in range(0, n, chunk):
            c1 = min(n, c0 + chunk)
            o_ref[:, c0:c1] = jnp.dot(xb, w_ref[:, c0:c1], preferred_element_type=F32).astype(o_ref.dtype)


def _proj(x, ws, out_dtypes, name):
    t, k = x.shape
    tm = min(TOKEN_TILE, t)
    in_specs = [pl.BlockSpec((tm, k), lambda i: (i, 0))]
    in_specs += [pl.BlockSpec(w.shape, lambda i: (0, 0)) for w in ws]
    out_specs = [pl.BlockSpec((tm, w.shape[1]), lambda i: (i, 0)) for w in ws]
    out_shape = [jax.ShapeDtypeStruct((t, w.shape[1]), dt) for w, dt in zip(ws, out_dtypes)]
    return pl.pallas_call(
        functools.partial(_proj_kernel, n_out=len(ws), chunk=512),
        grid=(t // tm,),
        in_specs=in_specs,
        out_specs=out_specs,
        out_shape=out_shape,
        compiler_params=_cparams("parallel"),
        name=name,
    )(x, *ws)


def _mamba_kernel(z_ref, x_ref, b_ref, c_ref, dt_ref, ptab_ref,
                  cwx_ref, cwb_ref, cwc_ref, cbx_ref, cbb_ref, cbc_ref,
                  c0x_ref, c0b_ref, c0c_ref, h0_ref, nw_ref,
                  y_ref, cox_ref, cob_ref, coc_ref, hout_ref,
                  sx, sb, sc, hs, *, q, nc, bb):
    c = pl.program_id(1)
    pad = SUBLANES
    hist = SSM_CONV - 1
    gw, ns, hpg = GROUP_WIDTH, SSM_D_STATE, HEADS_PER_GROUP

    @pl.when(c == 0)
    def _():
        sx[:, pad - hist:pad, :] = c0x_ref[...]
        sb[:, pad - hist:pad, :] = c0b_ref[...]
        sc[:, pad - hist:pad, :] = c0c_ref[...]
        hs[...] = h0_ref[...]

    ti = lax.broadcasted_iota(jnp.int32, (q, q), 0)
    si = lax.broadcasted_iota(jnp.int32, (q, q), 1)
    tri = ti >= si
    tri_f = tri.astype(F32)
    lane = lax.broadcasted_iota(jnp.int32, (1, gw), 1)
    shift = (jnp.concatenate([(ti - si == d).astype(BF16) for d in range(hist, 0, -1)], axis=0)
             if q > SUBLANES else None)
    ptab = ptab_ref[...]

    def expand(v, g):
        r = v.shape[0]
        h0 = hpg * g
        out = jnp.broadcast_to(v[:, h0 + hpg - 1:h0 + hpg], (r, gw))
        for j in range(hpg - 2, -1, -1):
            out = jnp.where(lane < SSM_HEAD_DIM * (j + 1), jnp.broadcast_to(v[:, h0 + j:h0 + j + 1], (r, gw)), out)
        return out

    def conv(cur, s, u, c0, c1, w_ref, bias_ref):
        cur_f = cur.astype(F32)
        s[u, pad:pad + q, c0:c1] = cur_f
        rows = q if shift is None else SUBLANES
        acc = bias_ref[:, c0:c1]
        for k in range(SSM_CONV):
            acc = acc + w_ref[k:k + 1, c0:c1] * s[u, pad - hist + k:pad - hist + k + rows, c0:c1]
        if shift is not None:
            rest = bias_ref[:, c0:c1] + w_ref[hist:hist + 1, c0:c1] * cur_f
            moved = jnp.dot(shift, cur, preferred_element_type=F32)
            for k in range(hist):
                rest = rest + w_ref[k:k + 1, c0:c1] * moved[k * q:(k + 1) * q]
            acc = jnp.concatenate([acc, rest[SUBLANES:]], axis=0)
        tail = s[u, pad + q - hist:pad + q, c0:c1]
        s[u, pad - hist:pad, c0:c1] = tail
        return _silu(acc), tail

    tails = []
    for u in range(bb):
        dt_all = _softplus(dt_ref[u] + ptab[0:1, :])
        da = dt_all * (-jnp.exp(ptab[1:2, :]))
        acum = jnp.dot(tri_f, da, precision=HIGHEST, preferred_element_type=F32)
        acum_t = acum.T
        ac_last = acum[q - 1:q, :]
        cd = jnp.exp(ac_last)
        for g in range(SSM_GROUPS):
            x0, x1 = gw * g, gw * (g + 1)
            n0, n1 = ns * g, ns * (g + 1)
            xc, tail_x = conv(x_ref[u, :, x0:x1], sx, u, x0, x1, cwx_ref, cbx_ref)
            bc, tail_b = conv(b_ref[u, :, n0:n1], sb, u, n0, n1, cwb_ref, cbb_ref)
            cc, tail_c = conv(c_ref[u, :, n0:n1], sc, u, n0, n1, cwc_ref, cbc_ref)

            dt_e = expand(dt_all, g)
            ac_e = expand(acum, g)
            ac_last_e = expand(ac_last, g)

            xdt = xc * dt_e
            xdt_b = xdt.astype(BF16)
            bb_ = bc.astype(BF16)
            cb_ = cc.astype(BF16)
            cbm = _dot_nt(cb_, bb_)
            hprev = hs[u, g]
            y = _dot_nt(cb_, hprev.astype(BF16)) * jnp.exp(ac_e)
            zero_b = jnp.zeros_like(xdt_b)
            for j in range(hpg):
                hd = hpg * g + j
                seg = acum[:, hd:hd + 1] - acum_t[hd:hd + 1, :]
                lm = jnp.exp(jnp.where(tri, seg, -jnp.inf))
                wj = (cbm * lm).astype(BF16)
                xm = jnp.where((lane >= SSM_HEAD_DIM * j) & (lane < SSM_HEAD_DIM * (j + 1)), xdt_b, zero_b)
                y = y + jnp.dot(wj, xm, preferred_element_type=F32)

            xw = (xdt * jnp.exp(ac_last_e - ac_e)).astype(BF16)
            st = _dot_tn(xw, bb_)
            for j in range(hpg):
                hd = hpg * g + j
                r0, r1 = SSM_HEAD_DIM * j, SSM_HEAD_DIM * (j + 1)
                hs[u, g, r0:r1, :] = (jnp.broadcast_to(cd[:, hd:hd + 1], (SSM_HEAD_DIM, ns)) * hprev[r0:r1, :]
                                      + st[r0:r1, :])

            y = y + expand(ptab[2:3, :], g) * xc
            zf = z_ref[u, :, x0:x1].astype(F32)
            y = y * _silu(zf)
            ms = jnp.mean(y * y, axis=-1, keepdims=True)
            y_ref[u, :, x0:x1] = (y * lax.rsqrt(ms + LN_EPS) * nw_ref[:, x0:x1]).astype(y_ref.dtype)
            tails.append((u, x0, x1, n0, n1, tail_x, tail_b, tail_c))

    @pl.when(c == nc - 1)
    def _():
        for u, x0, x1, n0, n1, tail_x, tail_b, tail_c in tails:
            cox_ref[u, :, x0:x1] = tail_x
            cob_ref[u, :, n0:n1] = tail_b
            coc_ref[u, :, n0:n1] = tail_c
        hout_ref[...] = hs[...]


def _mamba_core(zx, dt, conv0, ssm0, ptab, conv_w, conv_b, norm_w, bsz, seqlen):
    q = math.gcd(seqlen, SSM_CHUNK)
    nc = seqlen // q
    g = SSM_GROUPS
    bb = 1 if nc > 1 else math.gcd(bsz, MAMBA_SEQS_PER_STEP)
    di, gn = SSM_D_INNER, g * SSM_D_STATE
    zx3 = zx.reshape(bsz, seqlen, zx.shape[-1])
    dt3 = dt.reshape(bsz, seqlen, LANES)
    h0 = ssm0.reshape(bsz, g, GROUP_WIDTH, SSM_D_STATE)
    hist = SSM_CONV - 1
    in_specs = [
        pl.BlockSpec((bb, q, di), lambda b, c: (b, c, 0)),
        pl.BlockSpec((bb, q, di), lambda b, c: (b, c, 1)),
        pl.BlockSpec((bb, q, gn), lambda b, c: (b, c, 2 * di // gn)),
        pl.BlockSpec((bb, q, gn), lambda b, c: (b, c, 2 * di // gn + 1)),
        pl.BlockSpec((bb, q, LANES), lambda b, c: (b, c, 0)),
        pl.BlockSpec((SUBLANES, LANES), lambda b, c: (0, 0)),
        pl.BlockSpec((SSM_CONV, di), lambda b, c: (0, 0)),
        pl.BlockSpec((SSM_CONV, gn), lambda b, c: (0, di // gn)),
        pl.BlockSpec((SSM_CONV, gn), lambda b, c: (0, di // gn + 1)),
        pl.BlockSpec((1, di), lambda b, c: (0, 0)),
        pl.BlockSpec((1, gn), lambda b, c: (0, di // gn)),
        pl.BlockSpec((1, gn), lambda b, c: (0, di // gn + 1)),
        pl.BlockSpec((bb, hist, di), lambda b, c: (b, 0, 0)),
        pl.BlockSpec((bb, hist, gn), lambda b, c: (b, 0, di // gn)),
        pl.BlockSpec((bb, hist, gn), lambda b, c: (b, 0, di // gn + 1)),
        pl.BlockSpec((bb, g, GROUP_WIDTH, SSM_D_STATE), lambda b, c: (b, 0, 0, 0)),
        pl.BlockSpec((1, di), lambda b, c: (0, 0)),
    ]
    out_specs = [
        pl.BlockSpec((bb, q, di), lambda b, c: (b, c, 0)),
        pl.BlockSpec((bb, hist, di), lambda b, c: (b, 0, 0)),
        pl.BlockSpec((bb, hist, gn), lambda b, c: (b, 0, 0)),
        pl.BlockSpec((bb, hist, gn), lambda b, c: (b, 0, 0)),
        pl.BlockSpec((bb, g, GROUP_WIDTH, SSM_D_STATE), lambda b, c: (b, 0, 0, 0)),
    ]
    out_shape = [
        jax.ShapeDtypeStruct((bsz, seqlen, di), BF16),
        jax.ShapeDtypeStruct((bsz, hist, di), F32),
        jax.ShapeDtypeStruct((bsz, hist, gn), F32),
        jax.ShapeDtypeStruct((bsz, hist, gn), F32),
        jax.ShapeDtypeStruct((bsz, g, GROUP_WIDTH, SSM_D_STATE), F32),
    ]
    scratch = [
        pltpu.VMEM((bb, q + SUBLANES, di), F32),
        pltpu.VMEM((bb, q + SUBLANES, gn), F32),
        pltpu.VMEM((bb, q + SUBLANES, gn), F32),
        pltpu.VMEM((bb, g, GROUP_WIDTH, SSM_D_STATE), F32),
    ]
    y, cox, cob, coc, hout = pl.pallas_call(
        functools.partial(_mamba_kernel, q=q, nc=nc, bb=bb),
        grid=(bsz // bb, nc),
        in_specs=in_specs,
        out_specs=out_specs,
        out_shape=out_shape,
        scratch_shapes=scratch,
        compiler_params=_cparams("parallel", "arbitrary"),
        name="mamba_core",
    )(zx3, zx3, zx3, zx3, dt3, ptab, conv_w, conv_w, conv_w, conv_b, conv_b, conv_b,
      conv0, conv0, conv0, h0, norm_w)
    new_conv = jnp.concatenate([cox, cob, coc], axis=-1)
    return (y.reshape(bsz * seqlen, SSM_D_INNER), new_conv,
            hout.reshape(bsz, SSM_HEADS, SSM_HEAD_DIM, SSM_D_STATE))


def _post_mixer_kernel(y_ref, w_ref, xres_ref, g_ref, b_ref, wr_ref, br_ref,
                       x1_ref, route_ref, route_t_ref, cnt_ref, base, *, tm):
    i = pl.program_id(0)

    @pl.when(i == 0)
    def _():
        base[...] = jnp.zeros_like(base)

    h = jnp.dot(y_ref[...], w_ref[...], preferred_element_type=F32)
    x1 = _layer_norm(DN_ALPHA * xres_ref[...] + h, g_ref[...], b_ref[...])
    x1_ref[...] = x1

    x_hi = x1.astype(BF16)
    x_lo = (x1 - x_hi.astype(F32)).astype(BF16)
    both = jnp.dot(x_hi, wr_ref[...], preferred_element_type=F32)
    logits = (both[:, :LANES] + both[:, LANES:]
              + jnp.dot(x_lo, wr_ref[:, :LANES], preferred_element_type=F32) + br_ref[...])
    lane = lax.broadcasted_iota(jnp.int32, (tm, LANES), 1)
    neg = -jnp.inf
    gmask = lane < MOE_GROUPS
    gl = jnp.where(gmask, logits, neg)
    mg = jnp.max(gl, axis=1, keepdims=True)
    grp = jnp.min(jnp.where(gl == mg, lane, LANES), axis=1, keepdims=True)
    pg = 1.0 / jnp.sum(jnp.where(gmask, jnp.exp(logits - mg), 0.0), axis=1, keepdims=True)
    lo = MOE_GROUPS + MOE_EXPERTS_PER_GROUP * grp
    el = jnp.where((lane >= lo) & (lane < lo + MOE_EXPERTS_PER_GROUP), logits, neg)
    v1 = jnp.max(el, axis=1, keepdims=True)
    i1 = jnp.min(jnp.where(el == v1, lane, LANES), axis=1, keepdims=True)
    el2 = jnp.where(lane == i1, neg, el)
    v2 = jnp.max(el2, axis=1, keepdims=True)
    i2 = jnp.min(jnp.where(el2 == v2, lane, LANES), axis=1, keepdims=True)
    t = jnp.exp(v2 - v1)
    wa = pg / (1.0 + t)
    wb = pg * t / (1.0 + t)
    e1 = i1 - MOE_GROUPS
    e2 = i2 - MOE_GROUPS

    oh = ((lane == e1) | (lane == e2)).astype(BF16)
    ri = lax.broadcasted_iota(jnp.int32, (tm, tm), 0)
    ci = lax.broadcasted_iota(jnp.int32, (tm, tm), 1)
    before = jnp.dot((ri > ci).astype(BF16), oh, preferred_element_type=F32) + base[...]
    r1 = jnp.sum(jnp.where(lane == e1, before, 0.0), axis=1, keepdims=True)
    r2 = jnp.sum(jnp.where(lane == e2, before, 0.0), axis=1, keepdims=True)
    new_base = base[...] + jnp.sum(oh.astype(F32), axis=0, keepdims=True)
    base[...] = new_base
    cnt_ref[...] = new_base

    route = jnp.where(lane == 0, e1.astype(F32), 0.0)
    route = jnp.where(lane == 1, e2.astype(F32), route)
    route = jnp.where(lane == 2, r1, route)
    route = jnp.where(lane == 3, r2, route)
    route = jnp.where(lane == 4, wa, route)
    route = jnp.where(lane == 5, wb, route)
    route_ref[...] = route
    route_t_ref[...] = route.T[0:SUBLANES, :]


def _post_mixer(y, w, xres, ln_g, ln_b, wr, br, name):
    t, kin = y.shape
    tm = min(TOKEN_TILE, t)
    return pl.pallas_call(
        functools.partial(_post_mixer_kernel, tm=tm),
        grid=(t // tm,),
        in_specs=[
            pl.BlockSpec((tm, kin), lambda i: (i, 0)),
            pl.BlockSpec((kin, D_MODEL), lambda i: (0, 0)),
            pl.BlockSpec((tm, D_MODEL), lambda i: (i, 0)),
            pl.BlockSpec((1, D_MODEL), lambda i: (0, 0)),
            pl.BlockSpec((1, D_MODEL), lambda i: (0, 0)),
            pl.BlockSpec((D_MODEL, 2 * LANES), lambda i: (0, 0)),
            pl.BlockSpec((1, LANES), lambda i: (0, 0)),
        ],
        out_specs=[
            pl.BlockSpec((tm, D_MODEL), lambda i: (i, 0)),
            pl.BlockSpec((tm, LANES), lambda i: (i, 0)),
            pl.BlockSpec((SUBLANES, tm), lambda i: (0, i)),
            pl.BlockSpec((1, LANES), lambda i: (0, 0)),
        ],
        out_shape=[
            jax.ShapeDtypeStruct((t, D_MODEL), F32),
            jax.ShapeDtypeStruct((t, LANES), F32),
            jax.ShapeDtypeStruct((SUBLANES, t), F32),
            jax.ShapeDtypeStruct((1, LANES), F32),
        ],
        scratch_shapes=[pltpu.VMEM((1, LANES), F32)],
        compiler_params=_cparams("arbitrary"),
        name=name,
    )(y, w, xres, ln_g, ln_b, wr, br)


def _row_copy(src_ref, src_row, dst_ref, dst_row, sem):
    return pltpu.make_async_copy(src_ref.at[pl.ds(src_row, 1)], dst_ref.at[pl.ds(dst_row, 1)], sem)


def _dispatch_kernel(dest_ref, x_ref, xs_ref, sem, *, tm):
    def start(r, carry):
        _row_copy(x_ref, r, xs_ref, dest_ref[0, 0, r], sem).start()
        _row_copy(x_ref, r, xs_ref, dest_ref[0, 0, tm + r], sem).start(priority=1)
        return carry

    lax.fori_loop(0, tm, start, 0, unroll=True)
    for _ in range(2):
        pltpu.make_async_copy(x_ref, xs_ref.at[pl.ds(0, tm)], sem).wait()


def _dispatch(x1, dest3, tm):
    t = x1.shape[0]
    return pl.pallas_call(
        functools.partial(_dispatch_kernel, tm=tm),
        grid=(t // tm,),
        in_specs=[
            pl.BlockSpec((1, 1, 2 * tm), lambda i: (i, 0, 0), memory_space=pltpu.SMEM),
            pl.BlockSpec((tm, D_MODEL), lambda i: (i, 0)),
        ],
        out_specs=pl.BlockSpec(memory_space=pl.ANY),
        out_shape=jax.ShapeDtypeStruct((2 * t, D_MODEL), F32),
        scratch_shapes=[pltpu.SemaphoreType.DMA(())],
        compiler_params=_cparams("arbitrary"),
        name="moe_dispatch",
    )(dest3, x1)


def _ffn_kernel(wt_ref, we_ref, wlo_ref, whi_ref, wfirst_ref, wnew_ref,
                xs_ref, w1_ref, w3_ref, w2_ref, ys_ref, w1b, w3b, w2b, *, tf):
    w = pl.program_id(0)
    lo = wlo_ref[w]
    hi = whi_ref[w]

    @pl.when(wfirst_ref[w] == 1)
    def _():
        ys_ref[...] = jnp.zeros_like(ys_ref)

    @pl.when(wnew_ref[w] == 1)
    def _():
        w1b[...] = w1_ref[0].astype(BF16)
        w3b[...] = w3_ref[0].astype(BF16)
        w2b[...] = w2_ref[0].astype(BF16)

    @pl.when(hi > lo)
    def _():
        rows = wt_ref[w] * tf + lax.broadcasted_iota(jnp.int32, (tf, 1), 0)
        mask = (rows >= lo) & (rows < hi)
        xb = xs_ref[...].astype(BF16)
        a = jnp.dot(xb, w1b[...], preferred_element_type=F32)
        b = jnp.dot(xb, w3b[...], preferred_element_type=F32)
        hmid = (_silu(a) * b).astype(BF16)
        y = jnp.dot(hmid, w2b[...], preferred_element_type=F32)
        ys_ref[...] = jnp.where(mask, y, ys_ref[...])


def _ffn(xs, w1, w3, w2, work, tf):
    n = xs.shape[0]
    n_work = work[0].shape[0]
    grid_spec = pltpu.PrefetchScalarGridSpec(
        num_scalar_prefetch=6,
        grid=(n_work,),
        in_specs=[
            pl.BlockSpec((tf, D_MODEL), lambda w, wt, we, wlo, whi, wf, wn: (wt[w], 0)),
            pl.BlockSpec((1, D_MODEL, MOE_FF), lambda w, wt, we, wlo, whi, wf, wn: (we[w], 0, 0)),
            pl.BlockSpec((1, D_MODEL, MOE_FF), lambda w, wt, we, wlo, whi, wf, wn: (we[w], 0, 0)),
            pl.BlockSpec((1, MOE_FF, D_MODEL), lambda w, wt, we, wlo, whi, wf, wn: (we[w], 0, 0)),
        ],
        out_specs=pl.BlockSpec((tf, D_MODEL), lambda w, wt, we, wlo, whi, wf, wn: (wt[w], 0)),
        scratch_shapes=[
            pltpu.VMEM((D_MODEL, MOE_FF), BF16),
            pltpu.VMEM((D_MODEL, MOE_FF), BF16),
            pltpu.VMEM((MOE_FF, D_MODEL), BF16),
        ],
    )
    return pl.pallas_call(
        functools.partial(_ffn_kernel, tf=tf),
        grid_spec=grid_spec,
        out_shape=jax.ShapeDtypeStruct((n, D_MODEL), F32),
        compiler_params=_cparams("arbitrary"),
        name="moe_ffn",
    )(*work, xs, w1, w3, w2)


def _combine_kernel(dest_ref, x1_ref, route_ref, g_ref, b_ref, ys_ref, o_ref, buf0, buf1, sem, *, tm):
    def start(r, carry):
        _row_copy(ys_ref, dest_ref[0, 0, r], buf0, r, sem).start()
        _row_copy(ys_ref, dest_ref[0, 0, tm + r], buf1, r, sem).start(priority=1)
        return carry

    lax.fori_loop(0, tm, start, 0, unroll=True)
    pltpu.make_async_copy(ys_ref.at[pl.ds(0, tm)], buf0, sem).wait()
    pltpu.make_async_copy(ys_ref.at[pl.ds(0, tm)], buf1, sem).wait()

    route = route_ref[...]
    y = route[:, 4:5] * buf0[...] + route[:, 5:6] * buf1[...]
    o_ref[...] = _layer_norm(DN_ALPHA * x1_ref[...] + y, g_ref[...], b_ref[...])


def _combine(x1, route, ys, dest3, ln_g, ln_b, tm, name):
    t = x1.shape[0]
    return pl.pallas_call(
        functools.partial(_combine_kernel, tm=tm),
        grid=(t // tm,),
        in_specs=[
            pl.BlockSpec((1, 1, 2 * tm), lambda i: (i, 0, 0), memory_space=pltpu.SMEM),
            pl.BlockSpec((tm, D_MODEL), lambda i: (i, 0)),
            pl.BlockSpec((tm, LANES), lambda i: (i, 0)),
            pl.BlockSpec((1, D_MODEL), lambda i: (0, 0)),
            pl.BlockSpec((1, D_MODEL), lambda i: (0, 0)),
            pl.BlockSpec(memory_space=pl.ANY),
        ],
        out_specs=pl.BlockSpec((tm, D_MODEL), lambda i: (i, 0)),
        out_shape=jax.ShapeDtypeStruct((t, D_MODEL), F32),
        scratch_shapes=[
            pltpu.VMEM((tm, D_MODEL), F32),
            pltpu.VMEM((tm, D_MODEL), F32),
            pltpu.SemaphoreType.DMA(()),
        ],
        compiler_params=_cparams("arbitrary"),
        name=name,
    )(dest3, x1, route, ln_g, ln_b, ys)


def _moe(x1, route, route_t, counts, w1, w3, w2, expert_base, ln_g, ln_b, name):
    t = x1.shape[0]
    n = 2 * t
    tm = min(TOKEN_TILE, t)
    tf = min(FFN_TILE, n)
    cnt = counts[0, :MOE_EXPERTS].astype(jnp.int32)
    offs = jnp.concatenate([jnp.zeros((1,), jnp.int32), jnp.cumsum(cnt)])
    e = route_t[0:2].astype(jnp.int32)
    r = route_t[2:4].astype(jnp.int32)
    starts = jnp.where(e[None] == jnp.arange(MOE_EXPERTS, dtype=jnp.int32)[:, None, None], offs[:-1, None, None], 0)
    dest = jnp.sum(starts, axis=0) + r
    dest3 = jnp.transpose(dest.reshape(2, t // tm, tm), (1, 0, 2)).reshape(t // tm, 1, 2 * tm)

    n_tiles = n // tf
    n_work = n_tiles + MOE_EXPERTS
    first_tile = offs[:-1] // tf
    last_tile = (offs[1:] - 1) // tf
    ntile_e = jnp.where(cnt > 0, last_tile - first_tile + 1, 0)
    wstart = jnp.concatenate([jnp.zeros((1,), jnp.int32), jnp.cumsum(ntile_e)])
    total = wstart[-1]
    widx = jnp.arange(n_work, dtype=jnp.int32)
    we = jnp.clip(jnp.searchsorted(wstart, widx, side="right") - 1, 0, MOE_EXPERTS - 1).astype(jnp.int32)
    wt = first_tile[we] + (widx - wstart[we])
    valid = widx < total
    wt = jnp.where(valid, wt, n_tiles - 1).astype(jnp.int32)
    wlo = jnp.where(valid, offs[we], 0).astype(jnp.int32)
    whi = jnp.where(valid, offs[we + 1], 0).astype(jnp.int32)
    last_e = we[jnp.maximum(total - 1, 0)]
    we = jnp.where(valid, we, last_e).astype(jnp.int32)
    wfirst = jnp.concatenate([jnp.ones((1,), jnp.int32), (wt[1:] != wt[:-1]).astype(jnp.int32)])

    xs = _dispatch(x1, dest3, tm)
    wnew = jnp.concatenate([jnp.ones((1,), jnp.int32), (we[1:] != we[:-1]).astype(jnp.int32)])
    ys = _ffn(xs, w1, w3, w2, (wt, we + expert_base, wlo, whi, wfirst, wnew), tf)
    return _combine(x1, route, ys, dest3, ln_g, ln_b, tm, name)


def _bias_kernel(idx_ref, rbt_ref, o_ref, *, width):
    idx = idx_ref[...]
    bucket = lax.broadcasted_iota(jnp.int32, (NUM_BUCKETS, width), 0)
    onehot = (bucket == idx).astype(F32)
    vals = jnp.dot(rbt_ref[...], onehot, precision=HIGHEST, preferred_element_type=F32)
    o_ref[...] = jnp.where(idx < 0, -jnp.inf, vals)


def _bias_tiles(rel_bias, idx_np):
    n = idx_np.shape[0]
    width = 8192
    assert n % width == 0
    nh = rel_bias.shape[1]
    return pl.pallas_call(
        functools.partial(_bias_kernel, width=width),
        grid=(n // width,),
        in_specs=[
            pl.BlockSpec((1, width), lambda i: (0, i)),
            pl.BlockSpec((nh, NUM_BUCKETS), lambda i: (0, 0)),
        ],
        out_specs=pl.BlockSpec((nh, width), lambda i: (0, i)),
        out_shape=jax.ShapeDtypeStruct((nh, n), F32),
        compiler_params=_cparams("parallel"),
        name="rel_bias_tiles",
    )(jnp.asarray(idx_np.reshape(1, n)), rel_bias.T)


def _bucket_of_distance(dist):
    n = np.maximum(dist, 0)
    max_exact = NUM_BUCKETS // 2
    nf = np.maximum(n, 1).astype(np.float32)
    large = max_exact + (np.log(nf / np.float32(max_exact)) / np.float32(math.log(MAX_DISTANCE / max_exact))
                         * np.float32(NUM_BUCKETS - max_exact)).astype(np.int32)
    large = np.minimum(large, NUM_BUCKETS - 1)
    return np.where(n < max_exact, n, large).astype(np.int32)


def _bucket_tile(dist):
    return np.where(dist >= 0, _bucket_of_distance(dist), -1).astype(np.int32)


def _kvq_prompt_kernel(x_ref, wkt_ref, wv_ref, wq_ref, kt_ref, ktb_ref, v_ref, vb_ref, q_ref, *, chunk):
    xb = x_ref[...].astype(BF16)
    hd2 = 2 * ATT_HEAD_DIM
    for c0 in range(0, ATT_QK_DIM, chunk):
        kt = _dot_nt(wkt_ref[c0:c0 + chunk, :], xb)
        kt_ref[0, c0:c0 + chunk, :] = kt
        ktb_ref[0, 0, c0:c0 + chunk, :] = kt.astype(BF16)
    for c0 in range(0, ATT_V_DIM, chunk):
        v = jnp.dot(xb, wv_ref[:, c0:c0 + chunk], preferred_element_type=F32)
        vb_ref[:, c0:c0 + chunk] = v.astype(BF16)
        for h0 in range(0, chunk, hd2):
            v_ref[:, (c0 + h0) // hd2, :] = v[:, h0:h0 + hd2]
    for c0 in range(0, ATT_QK_DIM, chunk):
        q_ref[:, c0:c0 + chunk] = jnp.dot(xb, wq_ref[:, c0:c0 + chunk], preferred_element_type=F32).astype(BF16)


def _kvq_prompt(x, wkt, wv, wq, bsz, seqlen):
    t = bsz * seqlen
    tm = min(ATT_TILE, seqlen)
    nkb = seqlen // tm
    hd2 = 2 * ATT_HEAD_DIM
    full = lambda i: (0, 0)
    return pl.pallas_call(
        functools.partial(_kvq_prompt_kernel, chunk=512),
        grid=(t // tm,),
        in_specs=[
            pl.BlockSpec((tm, D_MODEL), lambda i: (i, 0)),
            pl.BlockSpec(wkt.shape, full),
            pl.BlockSpec(wv.shape, full),
            pl.BlockSpec(wq.shape, full),
        ],
        out_specs=[
            pl.BlockSpec((1, ATT_QK_DIM, tm), lambda i: (i // nkb, 0, i % nkb)),
            pl.BlockSpec((1, 1, ATT_QK_DIM, tm), lambda i: (i // nkb, i % nkb, 0, 0)),
            pl.BlockSpec((tm, ATT_HEADS, hd2), lambda i: (i, 0, 0)),
            pl.BlockSpec((tm, ATT_V_DIM), lambda i: (i, 0)),
            pl.BlockSpec((tm, ATT_QK_DIM), lambda i: (i, 0)),
        ],
        out_shape=[
            jax.ShapeDtypeStruct((bsz, ATT_QK_DIM, seqlen), F32),
            jax.ShapeDtypeStruct((bsz, nkb, ATT_QK_DIM, tm), BF16),
            jax.ShapeDtypeStruct((t, ATT_HEADS, hd2), F32),
            jax.ShapeDtypeStruct((t, ATT_V_DIM), BF16),
            jax.ShapeDtypeStruct((t, ATT_QK_DIM), BF16),
        ],
        compiler_params=_cparams("parallel"),
        name="kvq_proj_prompt",
    )(x, wkt, wv, wq)


def _lambda_value(lam_ref, layer_idx):
    lv = lam_ref[...]
    s1 = jnp.sum(lv[0:1, :] * lv[1:2, :], axis=1, keepdims=True)
    s2 = jnp.sum(lv[2:3, :] * lv[3:4, :], axis=1, keepdims=True)
    lam_init = 0.8 - 0.6 * math.exp(-0.3 * layer_idx)
    return jnp.exp(s1) - jnp.exp(s2) + lam_init, lam_init


def _attn_prompt_kernel(q_ref, kt_ref, v_ref, bias_ref, lam_ref, sw_ref, o_ref, *, tq, nq, layer_idx):
    lane = lax.broadcasted_iota(jnp.int32, (tq, 2 * ATT_HEAD_DIM), 1)
    lam, lam_init = _lambda_value(lam_ref, layer_idx)
    for qi in range(nq):
        qb = q_ref[0, qi * tq:(qi + 1) * tq, :]
        zero = jnp.zeros_like(qb)
        q2 = jnp.concatenate([jnp.where(lane < ATT_HEAD_DIM, qb, zero),
                              jnp.where(lane >= ATT_HEAD_DIM, qb, zero)], axis=0)
        m = l = acc = None
        for j in range(qi + 1):
            s = jnp.dot(q2, kt_ref[0, j], preferred_element_type=F32)
            if qi - j < 2:
                s = s + bias_ref[0, qi - j]
            vj = v_ref[0, j * tq:(j + 1) * tq, :]
            bm = jnp.max(s, axis=1, keepdims=True)
            if j == 0:
                m = bm
                p = jnp.exp2(s - m)
                l = jnp.sum(p, axis=1, keepdims=True)
                acc = jnp.dot(p.astype(BF16), vj, preferred_element_type=F32)
            else:
                m_new = jnp.maximum(m, bm)
                alpha = jnp.exp2(m - m_new)
                p = jnp.exp2(s - m_new)
                l = alpha * l + jnp.sum(p, axis=1, keepdims=True)
                acc = alpha * acc + jnp.dot(p.astype(BF16), vj, preferred_element_type=F32)
                m = m_new
        o = acc / l
        d = o[:tq] - lam * o[tq:]
        ms = jnp.mean(d * d, axis=-1, keepdims=True)
        o_ref[0, qi * tq:(qi + 1) * tq, :] = (
            d * lax.rsqrt(ms + LN_EPS) * sw_ref[...] * (1.0 - lam_init)).astype(o_ref.dtype)


def _attn_prompt(q, ktb, vb, bias, lam_vec, subln_w, bsz, seqlen, layer_idx):
    tq = min(ATT_TILE, seqlen)
    nq = seqlen // tq
    hd2 = 2 * ATT_HEAD_DIM
    q3 = q.reshape(bsz, seqlen, ATT_QK_DIM)
    v3 = vb.reshape(bsz, seqlen, ATT_V_DIM)
    out = pl.pallas_call(
        functools.partial(_attn_prompt_kernel, tq=tq, nq=nq, layer_idx=layer_idx),
        grid=(bsz, ATT_HEADS),
        in_specs=[
            pl.BlockSpec((1, seqlen, hd2), lambda b, h: (b, 0, h)),
            pl.BlockSpec((1, nq, hd2, tq), lambda b, h: (b, 0, h, 0)),
            pl.BlockSpec((1, seqlen, hd2), lambda b, h: (b, 0, h)),
            pl.BlockSpec((1, 2, 2 * tq, tq), lambda b, h: (h, 0, 0, 0)),
            pl.BlockSpec(lam_vec.shape, lambda b, h: (0, 0)),
            pl.BlockSpec((1, hd2), lambda b, h: (0, 0)),
        ],
        out_specs=pl.BlockSpec((1, seqlen, hd2), lambda b, h: (b, 0, h)),
        out_shape=jax.ShapeDtypeStruct((bsz, seqlen, ATT_V_DIM), BF16),
        compiler_params=_cparams("parallel", "parallel"),
        name="attn_prompt",
    )(q3, ktb, v3, bias, lam_vec, subln_w)
    return out.reshape(bsz * seqlen, ATT_V_DIM)


def _attn_sample_kernel(pt_ref, q_ref, *refs, nq, n_steps, pp, layer_idx):
    kt_refs = refs[:pp]
    vm_refs = refs[pp:2 * pp]
    (kn_ref, vn_ref, bias_ref, exp_ref, pmask_ref, lam_ref, sw_ref,
     o_ref, qbd, m_s, l_s, acc_s, kpad, vpad) = refs[2 * pp:]
    j = pl.program_id(1)
    maps = 2 * ATT_HEADS
    rows = maps * nq
    hd2 = 2 * ATT_HEAD_DIM
    page = PAGE_SIZE

    @pl.when(j == 0)
    def _():
        m_s[...] = jnp.full_like(m_s, -jnp.inf)
        l_s[...] = jnp.zeros_like(l_s)
        acc_s[...] = jnp.zeros_like(acc_s)
        qt = jnp.concatenate([q_ref[0].astype(F32)] * maps, axis=0)
        ri = lax.broadcasted_iota(jnp.int32, (rows, ATT_QK_DIM), 0)
        ci = lax.broadcasted_iota(jnp.int32, (rows, ATT_QK_DIM), 1)
        keep = (ri // nq) == (ci // ATT_HEAD_DIM)
        qbd[...] = jnp.where(keep, qt, 0.0).astype(qbd.dtype)

    def softmax_update(s):
        m = m_s[...]
        m_new = jnp.maximum(m, jnp.max(s, axis=1, keepdims=True))
        alpha = jnp.exp2(m - m_new)
        p = jnp.exp2(s - m_new)
        l_s[...] = alpha * l_s[...] + jnp.sum(p, axis=1, keepdims=True)
        m_s[...] = m_new
        return alpha, p.astype(BF16)

    def past_pages():
        q2 = qbd[...]
        s_parts = []
        for u in range(pp):
            kt = kt_refs[u][0].reshape(ATT_QK_DIM, page).astype(BF16)
            s_u = jnp.dot(q2, kt, preferred_element_type=F32)
            if u == pp - 1:
                s_u = s_u + bias_ref[jnp.where(j == n_steps - 1, 0, 2)]
            s_parts.append(s_u)
        alpha, pb = softmax_update(jnp.concatenate(s_parts, axis=1))
        pv = jnp.zeros((rows, hd2), F32)
        for u in range(pp):
            pe = jnp.dot(pb[:, page * u:page * (u + 1)], exp_ref[...], preferred_element_type=F32)
            pe = pe.astype(BF16) * pmask_ref[...]
            vm = vm_refs[u][0].reshape(page * ATT_HEADS, hd2).astype(BF16)
            pv = pv + jnp.dot(pe, vm, preferred_element_type=F32)
        acc_s[...] = alpha * acc_s[...] + pv

    past_pages()

    @pl.when(j == n_steps - 1)
    def _():
        kpad[...] = jnp.zeros_like(kpad)
        vpad[...] = jnp.zeros_like(vpad)
        kpad[0:nq, :] = kn_ref[0]
        vpad[0:nq, :] = vn_ref[0]
        s = _dot_nt(qbd[...], kpad[...].astype(BF16)) + bias_ref[1]
        alpha, pb = softmax_update(s)
        for h in range(ATT_HEADS):
            r0, r1 = 2 * nq * h, 2 * nq * (h + 1)
            pvh = jnp.dot(pb[r0:r1, :], vpad[:, hd2 * h:hd2 * (h + 1)].astype(BF16), preferred_element_type=F32)
            acc_s[r0:r1, :] = alpha[r0:r1, :] * acc_s[r0:r1, :] + pvh
        o = acc_s[...] / l_s[...]
        lam, lam_init = _lambda_value(lam_ref, layer_idx)
        for h in range(ATT_HEADS):
            r0 = 2 * nq * h
            d = o[r0:r0 + nq, :] - lam * o[r0 + nq:r0 + 2 * nq, :]
            ms = jnp.mean(d * d, axis=-1, keepdims=True)
            o_ref[0, :, hd2 * h:hd2 * (h + 1)] = (
                d * lax.rsqrt(ms + LN_EPS) * sw_ref[...] * (1.0 - lam_init)).astype(o_ref.dtype)


def _attn_sample(q, kv, cache_k, cache_v, page_table, bias, lam_vec, subln_w, bsz, nq, layer_idx):
    n_pages = page_table.shape[1]
    page = cache_k.shape[1]
    maps = 2 * ATT_HEADS
    rows = maps * nq
    hd2 = 2 * ATT_HEAD_DIM
    pp = math.gcd(n_pages, SAMPLE_PAGES_PER_STEP)
    n_steps = n_pages // pp
    q3 = q.reshape(bsz, nq, ATT_QK_DIM)
    kv3 = kv.reshape(bsz, nq, ATT_QK_DIM + ATT_V_DIM)
    cache_kt = jnp.transpose(cache_k, (0, 2, 3, 1))
    kk = np.arange(page)[:, None]
    cc = np.arange(page * ATT_HEADS)[None, :]
    expand = jnp.asarray((cc // ATT_HEADS == kk).astype(np.float32), dtype=BF16)
    rr = np.arange(rows)[:, None]
    pmask = jnp.asarray(((rr // (2 * nq)) == (cc % ATT_HEADS)).astype(np.float32), dtype=BF16)

    def page_map(u):
        return lambda b, j, pt: (pt[b, j * pp + u], 0, 0, 0)

    in_specs = [pl.BlockSpec((1, nq, ATT_QK_DIM), lambda b, j, pt: (b, 0, 0))]
    in_specs += [pl.BlockSpec((1, maps, ATT_HEAD_DIM, page), page_map(u)) for u in range(pp)]
    in_specs += [pl.BlockSpec((1, page, ATT_HEADS, hd2), page_map(u)) for u in range(pp)]
    in_specs += [
        pl.BlockSpec((1, nq, ATT_QK_DIM), lambda b, j, pt: (b, 0, 0)),
        pl.BlockSpec((1, nq, ATT_V_DIM), lambda b, j, pt: (b, 0, 1)),
        pl.BlockSpec((3, rows, page), lambda b, j, pt: (0, 0, 0)),
        pl.BlockSpec((page, page * ATT_HEADS), lambda b, j, pt: (0, 0)),
        pl.BlockSpec((rows, page * ATT_HEADS), lambda b, j, pt: (0, 0)),
        pl.BlockSpec(lam_vec.shape, lambda b, j, pt: (0, 0)),
        pl.BlockSpec((1, hd2), lambda b, j, pt: (0, 0)),
    ]
    grid_spec = pltpu.PrefetchScalarGridSpec(
        num_scalar_prefetch=1,
        grid=(bsz, n_steps),
        in_specs=in_specs,
        out_specs=pl.BlockSpec((1, nq, ATT_V_DIM), lambda b, j, pt: (b, 0, 0)),
        scratch_shapes=[
            pltpu.VMEM((rows, ATT_QK_DIM), BF16),
            pltpu.VMEM((rows, 1), F32),
            pltpu.VMEM((rows, 1), F32),
            pltpu.VMEM((rows, hd2), F32),
            pltpu.VMEM((page, ATT_QK_DIM), F32),
            pltpu.VMEM((page, ATT_V_DIM), F32),
        ],
    )
    out = pl.pallas_call(
        functools.partial(_attn_sample_kernel, nq=nq, n_steps=n_steps, pp=pp, layer_idx=layer_idx),
        grid_spec=grid_spec,
        out_shape=jax.ShapeDtypeStruct((bsz, nq, ATT_V_DIM), BF16),
        compiler_params=_cparams("parallel", "arbitrary"),
        name="attn_sample",
    )(page_table, q3, *([cache_kt] * pp), *([cache_v] * pp), kv3, kv3, bias, expand, pmask, lam_vec, subln_w)
    return out.reshape(bsz * nq, ATT_V_DIM)


def _router_params(wg, bg, we, be):
    wexp = jnp.transpose(we, (1, 0, 2)).reshape(D_MODEL, MOE_EXPERTS)
    wr = jnp.concatenate([wg, wexp], axis=1)
    wr = jnp.pad(wr, ((0, 0), (0, LANES - wr.shape[1])))
    w_hi = wr.astype(BF16)
    w_lo = (wr - w_hi.astype(F32)).astype(BF16)
    wr = jnp.concatenate([w_hi, w_lo], axis=1)
    br = jnp.concatenate([bg, be.reshape(MOE_EXPERTS)])
    br = jnp.pad(br, (0, LANES - br.shape[0])).reshape(1, LANES)
    return wr, br


def _head_tables(dt_bias, a_log, d_skip):
    rows = jnp.stack([dt_bias, a_log, d_skip])
    return jnp.pad(rows, ((0, SUBLANES - 3), (0, LANES - SSM_HEADS)))


def _prompt_bias_index(tq):
    r = np.arange(tq)[:, None]
    c = np.arange(tq)[None, :]
    tiles = [_bucket_tile(delta * tq + r - c) for delta in range(3)]
    return np.stack(tiles).reshape(-1)


def _sample_bias_index(nq, page, past_len):
    r = np.arange(nq)[:, None]
    c = np.arange(page)[None, :]
    last_page = _bucket_tile(past_len + r - (past_len - page) - c)
    own = np.where(c < nq, _bucket_tile(r - c), -1).astype(np.int32)
    far = _bucket_tile(np.full((nq, page), MAX_DISTANCE + page))
    return np.stack([last_page, own, far]).reshape(-1)


def kernel(x_prompt, x_sample, state_ssm, state_conv, cache_k, cache_v, page_table, ln_g, ln_b, m_w_in,
           m_conv_w, m_conv_b, m_dt_bias, m_a_log, m_d, m_norm_w, m_w_out, kv_w, a_w_q, a_lambda,
           a_subln_w, a_w_o, rel_bias, moe_wg, moe_bg, moe_we, moe_be, moe_w1, moe_w3, moe_w2):
    bp, lp, _ = x_prompt.shape
    bs, ls, _ = x_sample.shape
    n_pages = page_table.shape[1]
    page = cache_k.shape[1]
    past_len = n_pages * page
    assert page == PAGE_SIZE and page >= MAX_DISTANCE and min(ATT_TILE, lp) >= MAX_DISTANCE

    w_in = m_w_in[0]
    w_zx = w_in[:, :SSM_D_INNER + SSM_CONV_DIM].astype(BF16)
    w_dt = jnp.pad(w_in[:, SSM_D_INNER + SSM_CONV_DIM:], ((0, 0), (0, LANES - SSM_HEADS))).astype(BF16)
    ptab = _head_tables(m_dt_bias[0], m_a_log[0], m_d[0])
    conv_w = m_conv_w[0]
    conv_b = m_conv_b[0].reshape(1, SSM_CONV_DIM)
    norm_w = m_norm_w[0].reshape(1, SSM_D_INNER)
    w_out = m_w_out[0].astype(BF16)
    w_q = (a_w_q[0] * (LOG2E * ATT_HEAD_DIM ** -0.5)).astype(BF16)
    w_kv = kv_w.astype(BF16)
    w_kt = kv_w[:, :ATT_QK_DIM].T.astype(BF16)
    w_v = w_kv[:, ATT_QK_DIM:]
    w_o = a_w_o[0].astype(BF16)
    lam_vec = a_lambda[0]
    subln_w = a_subln_w[0].reshape(1, 2 * ATT_HEAD_DIM)
    routers = [_router_params(moe_wg[l], moe_bg[l], moe_we[l], moe_be[l]) for l in range(DEPTH)]
    experts = (moe_w1.reshape(DEPTH * MOE_EXPERTS, D_MODEL, MOE_FF),
               moe_w3.reshape(DEPTH * MOE_EXPERTS, D_MODEL, MOE_FF),
               moe_w2.reshape(DEPTH * MOE_EXPERTS, MOE_FF, D_MODEL))
    lng = ln_g.reshape(DEPTH, 2, 1, D_MODEL)
    lnb = ln_b.reshape(DEPTH, 2, 1, D_MODEL)

    tq = min(ATT_TILE, lp)
    maps = 2 * ATT_HEADS
    idx_p = _prompt_bias_index(tq)
    idx_s = _sample_bias_index(ls, page, past_len)
    n_p = idx_p.shape[0]
    n_s = idx_s.shape[0]
    n_tot = -(-(n_p + n_s) // 8192) * 8192
    idx_all = np.concatenate([idx_p, idx_s, np.zeros((n_tot - n_p - n_s,), np.int32)])
    tiles = _bias_tiles(rel_bias, idx_all)
    bias_p = tiles[:, :n_p].reshape(ATT_HEADS, 2, 3, tq, tq)
    bias_p = (bias_p[:, :, 0:2] - bias_p[:, :, 2:3]) * LOG2E
    bias_p = jnp.transpose(bias_p, (0, 2, 1, 3, 4)).reshape(ATT_HEADS, 2, 2 * tq, tq)
    bias_s = tiles[:, n_p:n_p + n_s].reshape(maps, 3, ls, page)
    bias_s = (bias_s - bias_s[:, 2:3]) * LOG2E
    bias_s = jnp.transpose(bias_s, (1, 0, 2, 3)).reshape(3, maps * ls, page)

    def trunk(x, ssm0, conv0, bsz, seqlen, past):
        t = bsz * seqlen
        xf = x.reshape(t, D_MODEL)
        zx, dt = _proj(xf, [w_zx, w_dt], [BF16, F32], "in_proj")
        y, new_conv, new_ssm = _mamba_core(zx, dt, conv0, ssm0, ptab, conv_w, conv_b, norm_w, bsz, seqlen)
        x1, route, route_t, counts = _post_mixer(y, w_out, xf, lng[0, 0], lnb[0, 0], *routers[0], "mamba_out_ln_router")
        x2 = _moe(x1, route, route_t, counts, *experts, 0, lng[0, 1], lnb[0, 1], "moe0_combine_ln")
        if past is None:
            kt, ktb, v4, vb, qs = _kvq_prompt(x2, w_kt, w_v, w_q, bsz, seqlen)
            k_new = jnp.transpose(kt.reshape(bsz, maps, ATT_HEAD_DIM, seqlen), (0, 3, 1, 2))
            v_new = v4.reshape(bsz, seqlen, ATT_HEADS, 2 * ATT_HEAD_DIM)
            o = _attn_prompt(qs, ktb, vb, bias_p, lam_vec, subln_w, bsz, seqlen, DEPTH - 1)
        else:
            kv, qs = _proj(x2, [w_kv, w_q], [F32, BF16], "kvq_proj")
            k_new = kv[:, :ATT_QK_DIM].reshape(bsz, seqlen, maps, ATT_HEAD_DIM)
            v_new = kv[:, ATT_QK_DIM:].reshape(bsz, seqlen, ATT_HEADS, 2 * ATT_HEAD_DIM)
            o = _attn_sample(qs, kv, cache_k, cache_v, page_table, bias_s, lam_vec, subln_w, bsz, seqlen, DEPTH - 1)
        x3, route, route_t, counts = _post_mixer(o, w_o, x2, lng[1, 0], lnb[1, 0], *routers[1], "attn_out_ln_router")
        x4 = _moe(x3, route, route_t, counts, *experts, MOE_EXPERTS, lng[1, 1], lnb[1, 1], "moe1_combine_ln")
        return (x4.reshape(bsz, seqlen, D_MODEL), new_ssm[None], new_conv[None], k_new, v_new)

    ssm0_p = jnp.zeros((bp, SSM_HEADS, SSM_HEAD_DIM, SSM_D_STATE), F32)
    conv0_p = jnp.zeros((bp, SSM_CONV - 1, SSM_CONV_DIM), F32)
    y_p, ssm_p, conv_p, k_p, v_p = trunk(x_prompt, ssm0_p, conv0_p, bp, lp, None)
    y_s, ssm_s, conv_s, k_s, v_s = trunk(x_sample, state_ssm[0], state_conv[0], bs, ls, True)
    return (y_p, y_s, ssm_p, conv_p, k_p, v_p, ssm_s, conv_s, k_s, v_s)
```

```python
import functools
import math

import numpy as np
import jax
import jax.numpy as jnp
from jax import lax
from jax.experimental import pallas as pl
from jax.experimental.pallas import tpu as pltpu

F32 = jnp.float32
BF16 = jnp.bfloat16
HIGHEST = lax.Precision.HIGHEST

D_MODEL = 1024
DEPTH = 2
DN_ALPHA = (2.0 * DEPTH) ** 0.25
LN_EPS = 1e-5
LOG2E = math.log2(math.e)
SSM_D_INNER = 2048
SSM_HEAD_DIM = 64
SSM_HEADS = 32
SSM_GROUPS = 8
SSM_D_STATE = 128
SSM_CONV = 4
SSM_CHUNK = 128
SSM_CONV_DIM = SSM_D_INNER + 2 * SSM_GROUPS * SSM_D_STATE
HEADS_PER_GROUP = SSM_HEADS // SSM_GROUPS
GROUP_WIDTH = HEADS_PER_GROUP * SSM_HEAD_DIM
ATT_HEADS = 8
ATT_HEAD_DIM = 64
ATT_QK_DIM = 1024
ATT_V_DIM = 1024
NUM_BUCKETS = 32
MAX_DISTANCE = 128
MOE_GROUPS = 4
MOE_EXPERTS_PER_GROUP = 8
MOE_EXPERTS = MOE_GROUPS * MOE_EXPERTS_PER_GROUP
MOE_FF = 512
PAGE_SIZE = 128

LANES = 128
SUBLANES = 8
VMEM_LIMIT = 56 * 1024 * 1024

TOKEN_TILE = 256
PROJ_TILE = 512
FFN_TILE = 256
ATT_TILE = 256
MAMBA_SEQS_PER_STEP = 2
SAMPLE_PAGES_PER_STEP = 8


def _cparams(*sem):
    return pltpu.CompilerParams(dimension_semantics=sem, vmem_limit_bytes=VMEM_LIMIT)


def _sigmoid(x):
    return 1.0 / (1.0 + jnp.exp(-x))


def _silu(x):
    return x * _sigmoid(x)


def _softplus(x):
    return jnp.maximum(x, 0.0) + jnp.log(1.0 + jnp.exp(-jnp.abs(x)))


def _layer_norm(u, g, b):
    mu = jnp.mean(u, axis=-1, keepdims=True)
    d = u - mu
    var = jnp.mean(d * d, axis=-1, keepdims=True)
    return d * lax.rsqrt(var + LN_EPS) * g + b


def _dot_nt(a, b):
    return lax.dot_general(a, b, (((1,), (1,)), ((), ())), preferred_element_type=F32)


def _dot_tn(a, b):
    return lax.dot_general(a, b, (((0,), (0,)), ((), ())), preferred_element_type=F32)


def _proj_kernel(x_ref, *refs, n_out, chunk):
    w_refs, o_refs = refs[:n_out], refs[n_out:]
    xb = x_ref[...].astype(BF16)
    for w_ref, o_ref in zip(w_refs, o_refs):
        n = w_ref.shape[1]
        for c0 in range(0, n, chunk):
            c1 = min(n, c0 + chunk)
            o_ref[:, c0:c1] = jnp.dot(xb, w_ref[:, c0:c1], preferred_element_type=F32).astype(o_ref.dtype)


def _proj(x, ws, out_dtypes, name):
    t, k = x.shape
    tm = min(PROJ_TILE, t)
    in_specs = [pl.BlockSpec((tm, k), lambda i: (i, 0))]
    in_specs += [pl.BlockSpec(w.shape, lambda i: (0, 0)) for w in ws]
    out_specs = [pl.BlockSpec((tm, w.shape[1]), lambda i: (i, 0)) for w in ws]
    out_shape = [jax.ShapeDtypeStruct((t, w.shape[1]), dt) for w, dt in zip(ws, out_dtypes)]
    return pl.pallas_call(
        functools.partial(_proj_kernel, n_out=len(ws), chunk=512),
        grid=(t // tm,),
        in_specs=in_specs,
        out_specs=out_specs,
        out_shape=out_shape,
        compiler_params=_cparams("parallel"),
        name=name,
    )(x, *ws)


def _mamba_kernel(z_ref, x_ref, b_ref, c_ref, dt_ref, ptab_ref,
                  cwx_ref, cwb_ref, cwc_ref, cbx_ref, cbb_ref, cbc_ref,
                  c0x_ref, c0b_ref, c0c_ref, h0_ref, nw_ref,
                  y_ref, cox_ref, cob_ref, coc_ref, hout_ref,
                  sx, sb, sc, hs, *, q, nc, bb):
    c = pl.program_id(1)
    pad = SUBLANES
    hist = SSM_CONV - 1
    gw, ns, hpg = GROUP_WIDTH, SSM_D_STATE, HEADS_PER_GROUP

    @pl.when(c == 0)
    def _():
        sx[:, pad - hist:pad, :] = c0x_ref[...]
        sb[:, pad - hist:pad, :] = c0b_ref[...]
        sc[:, pad - hist:pad, :] = c0c_ref[...]
        hs[...] = h0_ref[...]

    ti = lax.broadcasted_iota(jnp.int32, (q, q), 0)
    si = lax.broadcasted_iota(jnp.int32, (q, q), 1)
    tri = ti >= si
    tri_f = tri.astype(F32)
    lane = lax.broadcasted_iota(jnp.int32, (1, gw), 1)
    shift = (jnp.concatenate([(ti - si == d).astype(BF16) for d in range(hist, 0, -1)], axis=0)
             if q > SUBLANES else None)
    ptab = ptab_ref[...]

    def expand(v, g):
        r = v.shape[0]
        h0 = hpg * g
        out = jnp.broadcast_to(v[:, h0 + hpg - 1:h0 + hpg], (r, gw))
        for j in range(hpg - 2, -1, -1):
            out = jnp.where(lane < SSM_HEAD_DIM * (j + 1), jnp.broadcast_to(v[:, h0 + j:h0 + j + 1], (r, gw)), out)
        return out

    def conv(cur, s, u, c0, c1, w_ref, bias_ref):
        cur_f = cur.astype(F32)
        s[u, pad:pad + q, c0:c1] = cur_f
        rows = q if shift is None else SUBLANES
        acc = bias_ref[:, c0:c1]
        for k in range(SSM_CONV):
            acc = acc + w_ref[k:k + 1, c0:c1] * s[u, pad - hist + k:pad - hist + k + rows, c0:c1]
        if shift is not None:
            rest = bias_ref[:, c0:c1] + w_ref[hist:hist + 1, c0:c1] * cur_f
            moved = jnp.dot(shift, cur, preferred_element_type=F32)
            for k in range(hist):
                rest = rest + w_ref[k:k + 1, c0:c1] * moved[k * q:(k + 1) * q]
            acc = jnp.concatenate([acc, rest[SUBLANES:]], axis=0)
        tail = s[u, pad + q - hist:pad + q, c0:c1]
        s[u, pad - hist:pad, c0:c1] = tail
        return _silu(acc), tail

    tails = []
    for u in range(bb):
        dt_all = _softplus(dt_ref[u] + ptab[0:1, :])
        da = dt_all * (-jnp.exp(ptab[1:2, :]))
        acum = jnp.dot(tri_f, da, precision=HIGHEST, preferred_element_type=F32)
        acum_t = acum.T
        ac_last = acum[q - 1:q, :]
        cd = jnp.exp(ac_last)
        for g in range(SSM_GROUPS):
            x0, x1 = gw * g, gw * (g + 1)
            n0, n1 = ns * g, ns * (g + 1)
            xc, tail_x = conv(x_ref[u, :, x0:x1], sx, u, x0, x1, cwx_ref, cbx_ref)
            bc, tail_b = conv(b_ref[u, :, n0:n1], sb, u, n0, n1, cwb_ref, cbb_ref)
            cc, tail_c = conv(c_ref[u, :, n0:n1], sc, u, n0, n1, cwc_ref, cbc_ref)

            dt_e = expand(dt_all, g)
            ac_e = expand(acum, g)
            ac_last_e = expand(ac_last, g)

            xdt = xc * dt_e
            xdt_b = xdt.astype(BF16)
            bb_ = bc.astype(BF16)
            cb_ = cc.astype(BF16)
            cbm = _dot_nt(cb_, bb_)
            hprev = hs[u, g]
            y = _dot_nt(cb_, hprev.astype(BF16)) * jnp.exp(ac_e)
            zero_b = jnp.zeros_like(xdt_b)
            for j in range(hpg):
                hd = hpg * g + j
                seg = acum[:, hd:hd + 1] - acum_t[hd:hd + 1, :]
                lm = jnp.exp(jnp.where(tri, seg, -jnp.inf))
                wj = (cbm * lm).astype(BF16)
                xm = jnp.where((lane >= SSM_HEAD_DIM * j) & (lane < SSM_HEAD_DIM * (j + 1)), xdt_b, zero_b)
                y = y + jnp.dot(wj, xm, preferred_element_type=F32)

            xw = (xdt * jnp.exp(ac_last_e - ac_e)).astype(BF16)
            st = _dot_tn(xw, bb_)
            for j in range(hpg):
                hd = hpg * g + j
                r0, r1 = SSM_HEAD_DIM * j, SSM_HEAD_DIM * (j + 1)
                hs[u, g, r0:r1, :] = (jnp.broadcast_to(cd[:, hd:hd + 1], (SSM_HEAD_DIM, ns)) * hprev[r0:r1, :]
                                      + st[r0:r1, :])

            y = y + expand(ptab[2:3, :], g) * xc
            zf = z_ref[u, :, x0:x1].astype(F32)
            y = y * _silu(zf)
            ms = jnp.mean(y * y, axis=-1, keepdims=True)
            y_ref[u, :, x0:x1] = (y * lax.rsqrt(ms + LN_EPS) * nw_ref[:, x0:x1]).astype(y_ref.dtype)
            tails.append((u, x0, x1, n0, n1, tail_x, tail_b, tail_c))

    @pl.when(c == nc - 1)
    def _():
        for u, x0, x1, n0, n1, tail_x, tail_b, tail_c in tails:
            cox_ref[u, :, x0:x1] = tail_x
            cob_ref[u, :, n0:n1] = tail_b
            coc_ref[u, :, n0:n1] = tail_c
        hout_ref[...] = hs[...]


def _mamba_core(zx, dt, conv0, ssm0, ptab, conv_w, conv_b, norm_w, bsz, seqlen):
    q = math.gcd(seqlen, SSM_CHUNK)
    nc = seqlen // q
    g = SSM_GROUPS
    bb = 1 if nc > 1 else math.gcd(bsz, MAMBA_SEQS_PER_STEP)
    di, gn = SSM_D_INNER, g * SSM_D_STATE
    zx3 = zx.reshape(bsz, seqlen, zx.shape[-1])
    dt3 = dt.reshape(bsz, seqlen, LANES)
    h0 = ssm0.reshape(bsz, g, GROUP_WIDTH, SSM_D_STATE)
    hist = SSM_CONV - 1
    in_specs = [
        pl.BlockSpec((bb, q, di), lambda b, c: (b, c, 0)),
        pl.BlockSpec((bb, q, di), lambda b, c: (b, c, 1)),
        pl.BlockSpec((bb, q, gn), lambda b, c: (b, c, 2 * di // gn)),
        pl.BlockSpec((bb, q, gn), lambda b, c: (b, c, 2 * di // gn + 1)),
        pl.BlockSpec((bb, q, LANES), lambda b, c: (b, c, 0)),
        pl.BlockSpec((SUBLANES, LANES), lambda b, c: (0, 0)),
        pl.BlockSpec((SSM_CONV, di), lambda b, c: (0, 0)),
        pl.BlockSpec((SSM_CONV, gn), lambda b, c: (0, di // gn)),
        pl.BlockSpec((SSM_CONV, gn), lambda b, c: (0, di // gn + 1)),
        pl.BlockSpec((1, di), lambda b, c: (0, 0)),
        pl.BlockSpec((1, gn), lambda b, c: (0, di // gn)),
        pl.BlockSpec((1, gn), lambda b, c: (0, di // gn + 1)),
        pl.BlockSpec((bb, hist, di), lambda b, c: (b, 0, 0)),
        pl.BlockSpec((bb, hist, gn), lambda b, c: (b, 0, di // gn)),
        pl.BlockSpec((bb, hist, gn), lambda b, c: (b, 0, di // gn + 1)),
        pl.BlockSpec((bb, g, GROUP_WIDTH, SSM_D_STATE), lambda b, c: (b, 0, 0, 0)),
        pl.BlockSpec((1, di), lambda b, c: (0, 0)),
    ]
    out_specs = [
        pl.BlockSpec((bb, q, di), lambda b, c: (b, c, 0)),
        pl.BlockSpec((bb, hist, di), lambda b, c: (b, 0, 0)),
        pl.BlockSpec((bb, hist, gn), lambda b, c: (b, 0, 0)),
        pl.BlockSpec((bb, hist, gn), lambda b, c: (b, 0, 0)),
        pl.BlockSpec((bb, g, GROUP_WIDTH, SSM_D_STATE), lambda b, c: (b, 0, 0, 0)),
    ]
    out_shape = [
        jax.ShapeDtypeStruct((bsz, seqlen, di), BF16),
        jax.ShapeDtypeStruct((bsz, hist, di), F32),
        jax.ShapeDtypeStruct((bsz, hist, gn), F32),
        jax.ShapeDtypeStruct((bsz, hist, gn), F32),
        jax.ShapeDtypeStruct((bsz, g, GROUP_WIDTH, SSM_D_STATE), F32),
    ]
    scratch = [
        pltpu.VMEM((bb, q + SUBLANES, di), F32),
        pltpu.VMEM((bb, q + SUBLANES, gn), F32),
        pltpu.VMEM((bb, q + SUBLANES, gn), F32),
        pltpu.VMEM((bb, g, GROUP_WIDTH, SSM_D_STATE), F32),
    ]
    y, cox, cob, coc, hout = pl.pallas_call(
        functools.partial(_mamba_kernel, q=q, nc=nc, bb=bb),
        grid=(bsz // bb, nc),
        in_specs=in_specs,
        out_specs=out_specs,
        out_shape=out_shape,
        scratch_shapes=scratch,
        compiler_params=_cparams("parallel", "arbitrary"),
        name="mamba_core",
    )(zx3, zx3, zx3, zx3, dt3, ptab, conv_w, conv_w, conv_w, conv_b, conv_b, conv_b,
      conv0, conv0, conv0, h0, norm_w)
    new_conv = jnp.concatenate([cox, cob, coc], axis=-1)
    return (y.reshape(bsz * seqlen, SSM_D_INNER), new_conv,
            hout.reshape(bsz, SSM_HEADS, SSM_HEAD_DIM, SSM_D_STATE))


def _post_mixer_kernel(y_ref, w_ref, xres_ref, g_ref, b_ref, wr_ref, br_ref,
                       x1_ref, route_ref, route_t_ref, cnt_ref, base, *, tm):
    i = pl.program_id(0)

    @pl.when(i == 0)
    def _():
        base[...] = jnp.zeros_like(base)

    h = jnp.dot(y_ref[...], w_ref[...], preferred_element_type=F32)
    x1 = _layer_norm(DN_ALPHA * xres_ref[...] + h, g_ref[...], b_ref[...])
    x1_ref[...] = x1

    x_hi = x1.astype(BF16)
    x_lo = (x1 - x_hi.astype(F32)).astype(BF16)
    both = jnp.dot(x_hi, wr_ref[...], preferred_element_type=F32)
    logits = (both[:, :LANES] + both[:, LANES:]
              + jnp.dot(x_lo, wr_ref[:, :LANES], preferred_element_type=F32) + br_ref[...])
    lane = lax.broadcasted_iota(jnp.int32, (tm, LANES), 1)
    neg = -jnp.inf
    gmask = lane < MOE_GROUPS
    gl = jnp.where(gmask, logits, neg)
    mg = jnp.max(gl, axis=1, keepdims=True)
    grp = jnp.min(jnp.where(gl == mg, lane, LANES), axis=1, keepdims=True)
    pg = 1.0 / jnp.sum(jnp.where(gmask, jnp.exp(logits - mg), 0.0), axis=1, keepdims=True)
    lo = MOE_GROUPS + MOE_EXPERTS_PER_GROUP * grp
    el = jnp.where((lane >= lo) & (lane < lo + MOE_EXPERTS_PER_GROUP), logits, neg)
    v1 = jnp.max(el, axis=1, keepdims=True)
    i1 = jnp.min(jnp.where(el == v1, lane, LANES), axis=1, keepdims=True)
    el2 = jnp.where(lane == i1, neg, el)
    v2 = jnp.max(el2, axis=1, keepdims=True)
    i2 = jnp.min(jnp.where(el2 == v2, lane, LANES), axis=1, keepdims=True)
    t = jnp.exp(v2 - v1)
    wa = pg / (1.0 + t)
    wb = pg * t / (1.0 + t)
    e1 = i1 - MOE_GROUPS
    e2 = i2 - MOE_GROUPS

    oh = ((lane == e1) | (lane == e2)).astype(BF16)
    ri = lax.broadcasted_iota(jnp.int32, (tm, tm), 0)
    ci = lax.broadcasted_iota(jnp.int32, (tm, tm), 1)
    before = jnp.dot((ri > ci).astype(BF16), oh, preferred_element_type=F32) + base[...]
    r1 = jnp.sum(jnp.where(lane == e1, before, 0.0), axis=1, keepdims=True)
    r2 = jnp.sum(jnp.where(lane == e2, before, 0.0), axis=1, keepdims=True)
    new_base = base[...] + jnp.sum(oh.astype(F32), axis=0, keepdims=True)
    base[...] = new_base
    cnt_ref[...] = new_base

    route = jnp.where(lane == 0, e1.astype(F32), 0.0)
    route = jnp.where(lane == 1, e2.astype(F32), route)
    route = jnp.where(lane == 2, r1, route)
    route = jnp.where(lane == 3, r2, route)
    route = jnp.where(lane == 4, wa, route)
    route = jnp.where(lane == 5, wb, route)
    route_ref[...] = route
    route_t_ref[...] = route.T[0:SUBLANES, :]


def _post_mixer(y, w, xres, ln_g, ln_b, wr, br, name):
    t, kin = y.shape
    tm = min(PROJ_TILE, t)
    return pl.pallas_call(
        functools.partial(_post_mixer_kernel, tm=tm),
        grid=(t // tm,),
        in_specs=[
            pl.BlockSpec((tm, kin), lambda i: (i, 0)),
            pl.BlockSpec((kin, D_MODEL), lambda i: (0, 0)),
            pl.BlockSpec((tm, D_MODEL), lambda i: (i, 0)),
            pl.BlockSpec((1, D_MODEL), lambda i: (0, 0)),
            pl.BlockSpec((1, D_MODEL), lambda i: (0, 0)),
            pl.BlockSpec((D_MODEL, 2 * LANES), lambda i: (0, 0)),
            pl.BlockSpec((1, LANES), lambda i: (0, 0)),
        ],
        out_specs=[
            pl.BlockSpec((tm, D_MODEL), lambda i: (i, 0)),
            pl.BlockSpec((tm, LANES), lambda i: (i, 0)),
            pl.BlockSpec((SUBLANES, tm), lambda i: (0, i)),
            pl.BlockSpec((1, LANES), lambda i: (0, 0)),
        ],
        out_shape=[
            jax.ShapeDtypeStruct((t, D_MODEL), F32),
            jax.ShapeDtypeStruct((t, LANES), F32),
            jax.ShapeDtypeStruct((SUBLANES, t), F32),
            jax.ShapeDtypeStruct((1, LANES), F32),
        ],
        scratch_shapes=[pltpu.VMEM((1, LANES), F32)],
        compiler_params=_cparams("arbitrary"),
        name=name,
    )(y, w, xres, ln_g, ln_b, wr, br)


def _row_copy(src_ref, src_row, dst_ref, dst_row, sem):
    return pltpu.make_async_copy(src_ref.at[pl.ds(src_row, 1)], dst_ref.at[pl.ds(dst_row, 1)], sem)


def _dispatch_kernel(dest_ref, x_ref, xs_ref, sem, *, tm):
    def start(r, carry):
        _row_copy(x_ref, r, xs_ref, dest_ref[0, 0, r], sem).start()
        _row_copy(x_ref, r, xs_ref, dest_ref[0, 0, tm + r], sem).start(priority=1)
        return carry

    lax.fori_loop(0, tm, start, 0, unroll=True)
    for _ in range(2):
        pltpu.make_async_copy(x_ref, xs_ref.at[pl.ds(0, tm)], sem).wait()


def _dispatch(x1, dest3, tm):
    t = x1.shape[0]
    return pl.pallas_call(
        functools.partial(_dispatch_kernel, tm=tm),
        grid=(t // tm,),
        in_specs=[
            pl.BlockSpec((1, 1, 2 * tm), lambda i: (i, 0, 0), memory_space=pltpu.SMEM),
            pl.BlockSpec((tm, D_MODEL), lambda i: (i, 0)),
        ],
        out_specs=pl.BlockSpec(memory_space=pl.ANY),
        out_shape=jax.ShapeDtypeStruct((2 * t, D_MODEL), F32),
        scratch_shapes=[pltpu.SemaphoreType.DMA(())],
        compiler_params=_cparams("arbitrary"),
        name="moe_dispatch",
    )(dest3, x1)


def _ffn_kernel(wt_ref, we_ref, wlo_ref, whi_ref, wfirst_ref, wnew_ref,
                xs_ref, w1_ref, w3_ref, w2_ref, ys_ref, w1b, w3b, w2b, *, tf):
    w = pl.program_id(0)
    lo = wlo_ref[w]
    hi = whi_ref[w]

    @pl.when(wfirst_ref[w] == 1)
    def _():
        ys_ref[...] = jnp.zeros_like(ys_ref)

    @pl.when(wnew_ref[w] == 1)
    def _():
        w1b[...] = w1_ref[0].astype(BF16)
        w3b[...] = w3_ref[0].astype(BF16)
        w2b[...] = w2_ref[0].astype(BF16)

    @pl.when(hi > lo)
    def _():
        rows = wt_ref[w] * tf + lax.broadcasted_iota(jnp.int32, (tf, 1), 0)
        mask = (rows >= lo) & (rows < hi)
        xb = xs_ref[...].astype(BF16)
        a = jnp.dot(xb, w1b[...], preferred_element_type=F32)
        b = jnp.dot(xb, w3b[...], preferred_element_type=F32)
        hmid = (_silu(a) * b).astype(BF16)
        y = jnp.dot(hmid, w2b[...], preferred_element_type=F32)
        ys_ref[...] = jnp.where(mask, y, ys_ref[...])


def _ffn(xs, w1, w3, w2, work, tf):
    n = xs.shape[0]
    n_work = work[0].shape[0]
    grid_spec = pltpu.PrefetchScalarGridSpec(
        num_scalar_prefetch=6,
        grid=(n_work,),
        in_specs=[
            pl.BlockSpec((tf, D_MODEL), lambda w, wt, we, wlo, whi, wf, wn: (wt[w], 0)),
            pl.BlockSpec((1, D_MODEL, MOE_FF), lambda w, wt, we, wlo, whi, wf, wn: (we[w], 0, 0)),
            pl.BlockSpec((1, D_MODEL, MOE_FF), lambda w, wt, we, wlo, whi, wf, wn: (we[w], 0, 0)),
            pl.BlockSpec((1, MOE_FF, D_MODEL), lambda w, wt, we, wlo, whi, wf, wn: (we[w], 0, 0)),
        ],
        out_specs=pl.BlockSpec((tf, D_MODEL), lambda w, wt, we, wlo, whi, wf, wn: (wt[w], 0)),
        scratch_shapes=[
            pltpu.VMEM((D_MODEL, MOE_FF), BF16),
            pltpu.VMEM((D_MODEL, MOE_FF), BF16),
            pltpu.VMEM((MOE_FF, D_MODEL), BF16),
        ],
    )
    return pl.pallas_call(
        functools.partial(_ffn_kernel, tf=tf),
        grid_spec=grid_spec,
        out_shape=jax.ShapeDtypeStruct((n, D_MODEL), F32),
        compiler_params=_cparams("arbitrary"),
        name="moe_ffn",
    )(*work, xs, w1, w3, w2)


def _combine_kernel(dest_ref, x1_ref, route_ref, g_ref, b_ref, ys_ref, o_ref, buf0, buf1, sem, *, tm):
    def start(r, carry):
        _row_copy(ys_ref, dest_ref[0, 0, r], buf0, r, sem).start()
        _row_copy(ys_ref, dest_ref[0, 0, tm + r], buf1, r, sem).start(priority=1)
        return carry

    lax.fori_loop(0, tm, start, 0, unroll=True)
    pltpu.make_async_copy(ys_ref.at[pl.ds(0, tm)], buf0, sem).wait()
    pltpu.make_async_copy(ys_ref.at[pl.ds(0, tm)], buf1, sem).wait()

    route = route_ref[...]
    y = route[:, 4:5] * buf0[...] + route[:, 5:6] * buf1[...]
    o_ref[...] = _layer_norm(DN_ALPHA * x1_ref[...] + y, g_ref[...], b_ref[...])


def _combine(x1, route, ys, dest3, ln_g, ln_b, tm, name):
    t = x1.shape[0]
    return pl.pallas_call(
        functools.partial(_combine_kernel, tm=tm),
        grid=(t // tm,),
        in_specs=[
            pl.BlockSpec((1, 1, 2 * tm), lambda i: (i, 0, 0), memory_space=pltpu.SMEM),
            pl.BlockSpec((tm, D_MODEL), lambda i: (i, 0)),
            pl.BlockSpec((tm, LANES), lambda i: (i, 0)),
            pl.BlockSpec((1, D_MODEL), lambda i: (0, 0)),
            pl.BlockSpec((1, D_MODEL), lambda i: (0, 0)),
            pl.BlockSpec(memory_space=pl.ANY),
        ],
        out_specs=pl.BlockSpec((tm, D_MODEL), lambda i: (i, 0)),
        out_shape=jax.ShapeDtypeStruct((t, D_MODEL), F32),
        scratch_shapes=[
            pltpu.VMEM((tm, D_MODEL), F32),
            pltpu.VMEM((tm, D_MODEL), F32),
            pltpu.SemaphoreType.DMA(()),
        ],
        compiler_params=_cparams("arbitrary"),
        name=name,
    )(dest3, x1, route, ln_g, ln_b, ys)


def _moe(x1, route, route_t, counts, w1, w3, w2, expert_base, ln_g, ln_b, name):
    t = x1.shape[0]
    n = 2 * t
    tm = min(TOKEN_TILE, t)
    tf = min(FFN_TILE, n)
    cnt = counts[0, :MOE_EXPERTS].astype(jnp.int32)
    offs = jnp.concatenate([jnp.zeros((1,), jnp.int32), jnp.cumsum(cnt)])
    e = route_t[0:2].astype(jnp.int32)
    r = route_t[2:4].astype(jnp.int32)
    starts = jnp.where(e[None] == jnp.arange(MOE_EXPERTS, dtype=jnp.int32)[:, None, None], offs[:-1, None, None], 0)
    dest = jnp.sum(starts, axis=0) + r
    dest3 = jnp.transpose(dest.reshape(2, t // tm, tm), (1, 0, 2)).reshape(t // tm, 1, 2 * tm)

    n_tiles = n // tf
    n_work = n_tiles + MOE_EXPERTS
    first_tile = offs[:-1] // tf
    last_tile = (offs[1:] - 1) // tf
    ntile_e = jnp.where(cnt > 0, last_tile - first_tile + 1, 0)
    wstart = jnp.concatenate([jnp.zeros((1,), jnp.int32), jnp.cumsum(ntile_e)])
    total = wstart[-1]
    widx = jnp.arange(n_work, dtype=jnp.int32)
    we = jnp.clip(jnp.searchsorted(wstart, widx, side="right") - 1, 0, MOE_EXPERTS - 1).astype(jnp.int32)
    wt = first_tile[we] + (widx - wstart[we])
    valid = widx < total
    wt = jnp.where(valid, wt, n_tiles - 1).astype(jnp.int32)
    wlo = jnp.where(valid, offs[we], 0).astype(jnp.int32)
    whi = jnp.where(valid, offs[we + 1], 0).astype(jnp.int32)
    last_e = we[jnp.maximum(total - 1, 0)]
    we = jnp.where(valid, we, last_e).astype(jnp.int32)
    wfirst = jnp.concatenate([jnp.ones((1,), jnp.int32), (wt[1:] != wt[:-1]).astype(jnp.int32)])

    xs = _dispatch(x1, dest3, tm)
    wnew = jnp.concatenate([jnp.ones((1,), jnp.int32), (we[1:] != we[:-1]).astype(jnp.int32)])
    ys = _ffn(xs, w1, w3, w2, (wt, we + expert_base, wlo, whi, wfirst, wnew), tf)
    return _combine(x1, route, ys, dest3, ln_g, ln_b, tm, name)


def _bias_kernel(idx_ref, rbt_ref, o_ref, *, width):
    idx = idx_ref[...]
    bucket = lax.broadcasted_iota(jnp.int32, (NUM_BUCKETS, width), 0)
    onehot = (bucket == idx).astype(F32)
    vals = jnp.dot(rbt_ref[...], onehot, precision=HIGHEST, preferred_element_type=F32)
    o_ref[...] = jnp.where(idx < 0, -jnp.inf, vals)


def _bias_tiles(rel_bias, idx_np):
    n = idx_np.shape[0]
    width = 8192
    assert n % width == 0
    nh = rel_bias.shape[1]
    return pl.pallas_call(
        functools.partial(_bias_kernel, width=width),
        grid=(n // width,),
        in_specs=[
            pl.BlockSpec((1, width), lambda i: (0, i)),
            pl.BlockSpec((nh, NUM_BUCKETS), lambda i: (0, 0)),
        ],
        out_specs=pl.BlockSpec((nh, width), lambda i: (0, i)),
        out_shape=jax.ShapeDtypeStruct((nh, n), F32),
        compiler_params=_cparams("parallel"),
        name="rel_bias_tiles",
    )(jnp.asarray(idx_np.reshape(1, n)), rel_bias.T)


def _bucket_of_distance(dist):
    n = np.maximum(dist, 0)
    max_exact = NUM_BUCKETS // 2
    nf = np.maximum(n, 1).astype(np.float32)
    large = max_exact + (np.log(nf / np.float32(max_exact)) / np.float32(math.log(MAX_DISTANCE / max_exact))
                         * np.float32(NUM_BUCKETS - max_exact)).astype(np.int32)
    large = np.minimum(large, NUM_BUCKETS - 1)
    return np.where(n < max_exact, n, large).astype(np.int32)


def _bucket_tile(dist):
    return np.where(dist >= 0, _bucket_of_distance(dist), -1).astype(np.int32)


def _kvq_prompt_kernel(x_ref, wkt_ref, wv_ref, wq_ref, kt_ref, ktb_ref, v_ref, vb_ref, q_ref, *, chunk):
    xb = x_ref[...].astype(BF16)
    hd2 = 2 * ATT_HEAD_DIM
    for c0 in range(0, ATT_QK_DIM, chunk):
        kt = _dot_nt(wkt_ref[c0:c0 + chunk, :], xb)
        kt_ref[0, c0:c0 + chunk, :] = kt
        ktb_ref[0, 0, c0:c0 + chunk, :] = kt.astype(BF16)
    for c0 in range(0, ATT_V_DIM, chunk):
        v = jnp.dot(xb, wv_ref[:, c0:c0 + chunk], preferred_element_type=F32)
        vb_ref[:, c0:c0 + chunk] = v.astype(BF16)
        for h0 in range(0, chunk, hd2):
            v_ref[:, (c0 + h0) // hd2, :] = v[:, h0:h0 + hd2]
    for c0 in range(0, ATT_QK_DIM, chunk):
        q_ref[:, c0:c0 + chunk] = jnp.dot(xb, wq_ref[:, c0:c0 + chunk], preferred_element_type=F32).astype(BF16)


def _kvq_prompt(x, wkt, wv, wq, bsz, seqlen):
    t = bsz * seqlen
    tm = min(ATT_TILE, seqlen)
    nkb = seqlen // tm
    hd2 = 2 * ATT_HEAD_DIM
    full = lambda i: (0, 0)
    return pl.pallas_call(
        functools.partial(_kvq_prompt_kernel, chunk=512),
        grid=(t // tm,),
        in_specs=[
            pl.BlockSpec((tm, D_MODEL), lambda i: (i, 0)),
            pl.BlockSpec(wkt.shape, full),
            pl.BlockSpec(wv.shape, full),
            pl.BlockSpec(wq.shape, full),
        ],
        out_specs=[
            pl.BlockSpec((1, ATT_QK_DIM, tm), lambda i: (i // nkb, 0, i % nkb)),
            pl.BlockSpec((1, 1, ATT_QK_DIM, tm), lambda i: (i // nkb, i % nkb, 0, 0)),
            pl.BlockSpec((tm, ATT_HEADS, hd2), lambda i: (i, 0, 0)),
            pl.BlockSpec((tm, ATT_V_DIM), lambda i: (i, 0)),
            pl.BlockSpec((tm, ATT_QK_DIM), lambda i: (i, 0)),
        ],
        out_shape=[
            jax.ShapeDtypeStruct((bsz, ATT_QK_DIM, seqlen), F32),
            jax.ShapeDtypeStruct((bsz, nkb, ATT_QK_DIM, tm), BF16),
            jax.ShapeDtypeStruct((t, ATT_HEADS, hd2), F32),
            jax.ShapeDtypeStruct((t, ATT_V_DIM), BF16),
            jax.ShapeDtypeStruct((t, ATT_QK_DIM), BF16),
        ],
        compiler_params=_cparams("parallel"),
        name="kvq_proj_prompt",
    )(x, wkt, wv, wq)


def _lambda_value(lam_ref, layer_idx):
    lv = lam_ref[...]
    s1 = jnp.sum(lv[0:1, :] * lv[1:2, :], axis=1, keepdims=True)
    s2 = jnp.sum(lv[2:3, :] * lv[3:4, :], axis=1, keepdims=True)
    lam_init = 0.8 - 0.6 * math.exp(-0.3 * layer_idx)
    return jnp.exp(s1) - jnp.exp(s2) + lam_init, lam_init


def _attn_prompt_kernel(q_ref, kt_ref, v_ref, bias_ref, lam_ref, sw_ref, o_ref, *, tq, nq, layer_idx):
    lane = lax.broadcasted_iota(jnp.int32, (tq, 2 * ATT_HEAD_DIM), 1)
    lam, lam_init = _lambda_value(lam_ref, layer_idx)
    for qi in range(nq):
        qb = q_ref[0, qi * tq:(qi + 1) * tq, :]
        zero = jnp.zeros_like(qb)
        q2 = jnp.concatenate([jnp.where(lane < ATT_HEAD_DIM, qb, zero),
                              jnp.where(lane >= ATT_HEAD_DIM, qb, zero)], axis=0)
        m = l = acc = None
        for j in range(qi + 1):
            s = jnp.dot(q2, kt_ref[0, j], preferred_element_type=F32)
            if qi - j < 2:
                s = s + bias_ref[0, qi - j]
            vj = v_ref[0, j * tq:(j + 1) * tq, :]
            bm = jnp.max(s, axis=1, keepdims=True)
            if j == 0:
                m = bm
                p = jnp.exp2(s - m)
                l = jnp.sum(p, axis=1, keepdims=True)
                acc = jnp.dot(p.astype(BF16), vj, preferred_element_type=F32)
            else:
                m_new = jnp.maximum(m, bm)
                alpha = jnp.exp2(m - m_new)
                p = jnp.exp2(s - m_new)
                l = alpha * l + jnp.sum(p, axis=1, keepdims=True)
                acc = alpha * acc + jnp.dot(p.astype(BF16), vj, preferred_element_type=F32)
                m = m_new
        o = acc / l
        d = o[:tq] - lam * o[tq:]
        ms = jnp.mean(d * d, axis=-1, keepdims=True)
        o_ref[0, qi * tq:(qi + 1) * tq, :] = (
            d * lax.rsqrt(ms + LN_EPS) * sw_ref[...] * (1.0 - lam_init)).astype(o_ref.dtype)


def _attn_prompt(q, ktb, vb, bias, lam_vec, subln_w, bsz, seqlen, layer_idx):
    tq = min(ATT_TILE, seqlen)
    nq = seqlen // tq
    hd2 = 2 * ATT_HEAD_DIM
    q3 = q.reshape(bsz, seqlen, ATT_QK_DIM)
    v3 = vb.reshape(bsz, seqlen, ATT_V_DIM)
    out = pl.pallas_call(
        functools.partial(_attn_prompt_kernel, tq=tq, nq=nq, layer_idx=layer_idx),
        grid=(bsz, ATT_HEADS),
        in_specs=[
            pl.BlockSpec((1, seqlen, hd2), lambda b, h: (b, 0, h)),
            pl.BlockSpec((1, nq, hd2, tq), lambda b, h: (b, 0, h, 0)),
            pl.BlockSpec((1, seqlen, hd2), lambda b, h: (b, 0, h)),
            pl.BlockSpec((1, 2, 2 * tq, tq), lambda b, h: (h, 0, 0, 0)),
            pl.BlockSpec(lam_vec.shape, lambda b, h: (0, 0)),
            pl.BlockSpec((1, hd2), lambda b, h: (0, 0)),
        ],
        out_specs=pl.BlockSpec((1, seqlen, hd2), lambda b, h: (b, 0, h)),
        out_shape=jax.ShapeDtypeStruct((bsz, seqlen, ATT_V_DIM), BF16),
        compiler_params=_cparams("parallel", "parallel"),
        name="attn_prompt",
    )(q3, ktb, v3, bias, lam_vec, subln_w)
    return out.reshape(bsz * seqlen, ATT_V_DIM)


def _attn_sample_kernel(pt_ref, q_ref, *refs, nq, n_steps, pp, layer_idx):
    kt_refs = refs[:pp]
    vm_refs = refs[pp:2 * pp]
    (kn_ref, vn_ref, bias_ref, exp_ref, pmask_ref, lam_ref, sw_ref,
     o_ref, qbd, m_s, l_s, acc_s, kpad, vpad) = refs[2 * pp:]
    j = pl.program_id(1)
    maps = 2 * ATT_HEADS
    rows = maps * nq
    hd2 = 2 * ATT_HEAD_DIM
    page = PAGE_SIZE

    @pl.when(j == 0)
    def _():
        m_s[...] = jnp.full_like(m_s, -jnp.inf)
        l_s[...] = jnp.zeros_like(l_s)
        acc_s[...] = jnp.zeros_like(acc_s)
        qt = jnp.concatenate([q_ref[0].astype(F32)] * maps, axis=0)
        ri = lax.broadcasted_iota(jnp.int32, (rows, ATT_QK_DIM), 0)
        ci = lax.broadcasted_iota(jnp.int32, (rows, ATT_QK_DIM), 1)
        keep = (ri // nq) == (ci // ATT_HEAD_DIM)
        qbd[...] = jnp.where(keep, qt, 0.0).astype(qbd.dtype)

    def softmax_update(s):
        m = m_s[...]
        m_new = jnp.maximum(m, jnp.max(s, axis=1, keepdims=True))
        alpha = jnp.exp2(m - m_new)
        p = jnp.exp2(s - m_new)
        l_s[...] = alpha * l_s[...] + jnp.sum(p, axis=1, keepdims=True)
        m_s[...] = m_new
        return alpha, p.astype(BF16)

    def past_pages():
        kt_all = jnp.concatenate(
            [kt_refs[u][0].reshape(ATT_QK_DIM, page).astype(BF16) for u in range(pp)], axis=1)
        s = jnp.dot(qbd[...], kt_all, preferred_element_type=F32)
        last = bias_ref[jnp.where(j == n_steps - 1, 0, 2)]
        s = jnp.concatenate([s[:, :page * (pp - 1)], s[:, page * (pp - 1):] + last], axis=1)
        alpha, pb = softmax_update(s)
        p_rows = jnp.concatenate([pb[:, page * u:page * (u + 1)] for u in range(pp)], axis=0)
        pe = jnp.dot(p_rows, exp_ref[...], preferred_element_type=F32)
        pe_cat = jnp.concatenate(
            [pe[rows * u:rows * (u + 1)].astype(BF16) * pmask_ref[...] for u in range(pp)], axis=1)
        vm_all = jnp.concatenate(
            [vm_refs[u][0].reshape(page * ATT_HEADS, hd2).astype(BF16) for u in range(pp)], axis=0)
        acc_s[...] = alpha * acc_s[...] + jnp.dot(pe_cat, vm_all, preferred_element_type=F32)

    past_pages()

    @pl.when(j == n_steps - 1)
    def _():
        kpad[...] = jnp.zeros_like(kpad)
        vpad[...] = jnp.zeros_like(vpad)
        kpad[0:nq, :] = kn_ref[0]
        vpad[0:nq, :] = vn_ref[0]
        s = _dot_nt(qbd[...], kpad[...].astype(BF16)) + bias_ref[1]
        alpha, pb = softmax_update(s)
        for h in range(ATT_HEADS):
            r0, r1 = 2 * nq * h, 2 * nq * (h + 1)
            pvh = jnp.dot(pb[r0:r1, :], vpad[:, hd2 * h:hd2 * (h + 1)].astype(BF16), preferred_element_type=F32)
            acc_s[r0:r1, :] = alpha[r0:r1, :] * acc_s[r0:r1, :] + pvh
        o = acc_s[...] / l_s[...]
        lam, lam_init = _lambda_value(lam_ref, layer_idx)
        for h in range(ATT_HEADS):
            r0 = 2 * nq * h
            d = o[r0:r0 + nq, :] - lam * o[r0 + nq:r0 + 2 * nq, :]
            ms = jnp.mean(d * d, axis=-1, keepdims=True)
            o_ref[0, :, hd2 * h:hd2 * (h + 1)] = (
                d * lax.rsqrt(ms + LN_EPS) * sw_ref[...] * (1.0 - lam_init)).astype(o_ref.dtype)


def _attn_sample(q, kv, cache_k, cache_v, page_table, bias, lam_vec, subln_w, bsz, nq, layer_idx):
    n_pages = page_table.shape[1]
    page = cache_k.shape[1]
    maps = 2 * ATT_HEADS
    rows = maps * nq
    hd2 = 2 * ATT_HEAD_DIM
    pp = math.gcd(n_pages, SAMPLE_PAGES_PER_STEP)
    n_steps = n_pages // pp
    q3 = q.reshape(bsz, nq, ATT_QK_DIM)
    kv3 = kv.reshape(bsz, nq, ATT_QK_DIM + ATT_V_DIM)
    cache_kt = jnp.transpose(cache_k, (0, 2, 3, 1))
    kk = np.arange(page)[:, None]
    cc = np.arange(page * ATT_HEADS)[None, :]
    expand = jnp.asarray((cc // ATT_HEADS == kk).astype(np.float32), dtype=BF16)
    rr = np.arange(rows)[:, None]
    pmask = jnp.asarray(((rr // (2 * nq)) == (cc % ATT_HEADS)).astype(np.float32), dtype=BF16)

    def page_map(u):
        return lambda b, j, pt: (pt[b, j * pp + u], 0, 0, 0)

    in_specs = [pl.BlockSpec((1, nq, ATT_QK_DIM), lambda b, j, pt: (b, 0, 0))]
    in_specs += [pl.BlockSpec((1, maps, ATT_HEAD_DIM, page), page_map(u)) for u in range(pp)]
    in_specs += [pl.BlockSpec((1, page, ATT_HEADS, hd2), page_map(u)) for u in range(pp)]
    in_specs += [
        pl.BlockSpec((1, nq, ATT_QK_DIM), lambda b, j, pt: (b, 0, 0)),
        pl.BlockSpec((1, nq, ATT_V_DIM), lambda b, j, pt: (b, 0, 1)),
        pl.BlockSpec((3, rows, page), lambda b, j, pt: (0, 0, 0)),
        pl.BlockSpec((page, page * ATT_HEADS), lambda b, j, pt: (0, 0)),
        pl.BlockSpec((rows, page * ATT_HEADS), lambda b, j, pt: (0, 0)),
        pl.BlockSpec(lam_vec.shape, lambda b, j, pt: (0, 0)),
        pl.BlockSpec((1, hd2), lambda b, j, pt: (0, 0)),
    ]
    grid_spec = pltpu.PrefetchScalarGridSpec(
        num_scalar_prefetch=1,
        grid=(bsz, n_steps),
        in_specs=in_specs,
        out_specs=pl.BlockSpec((1, nq, ATT_V_DIM), lambda b, j, pt: (b, 0, 0)),
        scratch_shapes=[
            pltpu.VMEM((rows, ATT_QK_DIM), BF16),
            pltpu.VMEM((rows, 1), F32),
            pltpu.VMEM((rows, 1), F32),
            pltpu.VMEM((rows, hd2), F32),
            pltpu.VMEM((page, ATT_QK_DIM), F32),
            pltpu.VMEM((page, ATT_V_DIM), F32),
        ],
    )
    out = pl.pallas_call(
        functools.partial(_attn_sample_kernel, nq=nq, n_steps=n_steps, pp=pp, layer_idx=layer_idx),
        grid_spec=grid_spec,
        out_shape=jax.ShapeDtypeStruct((bsz, nq, ATT_V_DIM), BF16),
        compiler_params=_cparams("parallel", "arbitrary"),
        name="attn_sample",
    )(page_table, q3, *([cache_kt] * pp), *([cache_v] * pp), kv3, kv3, bias, expand, pmask, lam_vec, subln_w)
    return out.reshape(bsz * nq, ATT_V_DIM)


def _router_params(wg, bg, we, be):
    wexp = jnp.transpose(we, (1, 0, 2)).reshape(D_MODEL, MOE_EXPERTS)
    wr = jnp.concatenate([wg, wexp], axis=1)
    wr = jnp.pad(wr, ((0, 0), (0, LANES - wr.shape[1])))
    w_hi = wr.astype(BF16)
    w_lo = (wr - w_hi.astype(F32)).astype(BF16)
    wr = jnp.concatenate([w_hi, w_lo], axis=1)
    br = jnp.concatenate([bg, be.reshape(MOE_EXPERTS)])
    br = jnp.pad(br, (0, LANES - br.shape[0])).reshape(1, LANES)
    return wr, br


def _head_tables(dt_bias, a_log, d_skip):
    rows = jnp.stack([dt_bias, a_log, d_skip])
    return jnp.pad(rows, ((0, SUBLANES - 3), (0, LANES - SSM_HEADS)))


def _prompt_bias_index(tq):
    r = np.arange(tq)[:, None]
    c = np.arange(tq)[None, :]
    tiles = [_bucket_tile(delta * tq + r - c) for delta in range(3)]
    return np.stack(tiles).reshape(-1)


def _sample_bias_index(nq, page, past_len):
    r = np.arange(nq)[:, None]
    c = np.arange(page)[None, :]
    last_page = _bucket_tile(past_len + r - (past_len - page) - c)
    own = np.where(c < nq, _bucket_tile(r - c), -1).astype(np.int32)
    far = _bucket_tile(np.full((nq, page), MAX_DISTANCE + page))
    return np.stack([last_page, own, far]).reshape(-1)


def kernel(x_prompt, x_sample, state_ssm, state_conv, cache_k, cache_v, page_table, ln_g, ln_b, m_w_in,
           m_conv_w, m_conv_b, m_dt_bias, m_a_log, m_d, m_norm_w, m_w_out, kv_w, a_w_q, a_lambda,
           a_subln_w, a_w_o, rel_bias, moe_wg, moe_bg, moe_we, moe_be, moe_w1, moe_w3, moe_w2):
    bp, lp, _ = x_prompt.shape
    bs, ls, _ = x_sample.shape
    n_pages = page_table.shape[1]
    page = cache_k.shape[1]
    past_len = n_pages * page
    assert page == PAGE_SIZE and page >= MAX_DISTANCE and min(ATT_TILE, lp) >= MAX_DISTANCE

    w_in = m_w_in[0]
    w_zx = w_in[:, :SSM_D_INNER + SSM_CONV_DIM].astype(BF16)
    w_dt = jnp.pad(w_in[:, SSM_D_INNER + SSM_CONV_DIM:], ((0, 0), (0, LANES - SSM_HEADS))).astype(BF16)
    ptab = _head_tables(m_dt_bias[0], m_a_log[0], m_d[0])
    conv_w = m_conv_w[0]
    conv_b = m_conv_b[0].reshape(1, SSM_CONV_DIM)
    norm_w = m_norm_w[0].reshape(1, SSM_D_INNER)
    w_out = m_w_out[0].astype(BF16)
    w_q = (a_w_q[0] * (LOG2E * ATT_HEAD_DIM ** -0.5)).astype(BF16)
    w_kv = kv_w.astype(BF16)
    w_kt = kv_w[:, :ATT_QK_DIM].T.astype(BF16)
    w_v = w_kv[:, ATT_QK_DIM:]
    w_o = a_w_o[0].astype(BF16)
    lam_vec = a_lambda[0]
    subln_w = a_subln_w[0].reshape(1, 2 * ATT_HEAD_DIM)
    routers = [_router_params(moe_wg[l], moe_bg[l], moe_we[l], moe_be[l]) for l in range(DEPTH)]
    experts = (moe_w1.reshape(DEPTH * MOE_EXPERTS, D_MODEL, MOE_FF),
               moe_w3.reshape(DEPTH * MOE_EXPERTS, D_MODEL, MOE_FF),
               moe_w2.reshape(DEPTH * MOE_EXPERTS, MOE_FF, D_MODEL))
    lng = ln_g.reshape(DEPTH, 2, 1, D_MODEL)
    lnb = ln_b.reshape(DEPTH, 2, 1, D_MODEL)

    tq = min(ATT_TILE, lp)
    maps = 2 * ATT_HEADS
    idx_p = _prompt_bias_index(tq)
    idx_s = _sample_bias_index(ls, page, past_len)
    n_p = idx_p.shape[0]
    n_s = idx_s.shape[0]
    n_tot = -(-(n_p + n_s) // 8192) * 8192
    idx_all = np.concatenate([idx_p, idx_s, np.zeros((n_tot - n_p - n_s,), np.int32)])
    tiles = _bias_tiles(rel_bias, idx_all)
    bias_p = tiles[:, :n_p].reshape(ATT_HEADS, 2, 3, tq, tq)
    bias_p = (bias_p[:, :, 0:2] - bias_p[:, :, 2:3]) * LOG2E
    bias_p = jnp.transpose(bias_p, (0, 2, 1, 3, 4)).reshape(ATT_HEADS, 2, 2 * tq, tq)
    bias_s = tiles[:, n_p:n_p + n_s].reshape(maps, 3, ls, page)
    bias_s = (bias_s - bias_s[:, 2:3]) * LOG2E
    bias_s = jnp.transpose(bias_s, (1, 0, 2, 3)).reshape(3, maps * ls, page)

    def trunk(x, ssm0, conv0, bsz, seqlen, past):
        t = bsz * seqlen
        xf = x.reshape(t, D_MODEL)
        zx, dt = _proj(xf, [w_zx, w_dt], [BF16, F32], "in_proj")
        y, new_conv, new_ssm = _mamba_core(zx, dt, conv0, ssm0, ptab, conv_w, conv_b, norm_w, bsz, seqlen)
        x1, route, route_t, counts = _post_mixer(y, w_out, xf, lng[0, 0], lnb[0, 0], *routers[0], "mamba_out_ln_router")
        x2 = _moe(x1, route, route_t, counts, *experts, 0, lng[0, 1], lnb[0, 1], "moe0_combine_ln")
        if past is None:
            kt, ktb, v4, vb, qs = _kvq_prompt(x2, w_kt, w_v, w_q, bsz, seqlen)
            k_new = jnp.transpose(kt.reshape(bsz, maps, ATT_HEAD_DIM, seqlen), (0, 3, 1, 2))
            v_new = v4.reshape(bsz, seqlen, ATT_HEADS, 2 * ATT_HEAD_DIM)
            o = _attn_prompt(qs, ktb, vb, bias_p, lam_vec, subln_w, bsz, seqlen, DEPTH - 1)
        else:
            kv, qs = _proj(x2, [w_kv, w_q], [F32, BF16], "kvq_proj")
            k_new = kv[:, :ATT_QK_DIM].reshape(bsz, seqlen, maps, ATT_HEAD_DIM)
            v_new = kv[:, ATT_QK_DIM:].reshape(bsz, seqlen, ATT_HEADS, 2 * ATT_HEAD_DIM)
            o = _attn_sample(qs, kv, cache_k, cache_v, page_table, bias_s, lam_vec, subln_w, bsz, seqlen, DEPTH - 1)
        x3, route, route_t, counts = _post_mixer(o, w_o, x2, lng[1, 0], lnb[1, 0], *routers[1], "attn_out_ln_router")
        x4 = _moe(x3, route, route_t, counts, *experts, MOE_EXPERTS, lng[1, 1], lnb[1, 1], "moe1_combine_ln")
        return (x4.reshape(bsz, seqlen, D_MODEL), new_ssm[None], new_conv[None], k_new, v_new)

    ssm0_p = jnp.zeros((bp, SSM_HEADS, SSM_HEAD_DIM, SSM_D_STATE), F32)
    conv0_p = jnp.zeros((bp, SSM_CONV - 1, SSM_CONV_DIM), F32)
    y_p, ssm_p, conv_p, k_p, v_p = trunk(x_prompt, ssm0_p, conv0_p, bp, lp, None)
    y_s, ssm_s, conv_s, k_s, v_s = trunk(x_sample, state_ssm[0], state_conv[0], bs, ls, True)
    return (y_p, y_s, ssm_p, conv_p, k_p, v_p, ssm_s, conv_s, k_s, v_s)
```

```python
import functools
import math

import numpy as np
import jax
import jax.numpy as jnp
from jax import lax
from jax.experimental import pallas as pl
from jax.experimental.pallas import tpu as pltpu

F32 = jnp.float32
BF16 = jnp.bfloat16
HIGHEST = lax.Precision.HIGHEST

D_MODEL = 1024
DEPTH = 2
DN_ALPHA = (2.0 * DEPTH) ** 0.25
LN_EPS = 1e-5
LOG2E = math.log2(math.e)
SSM_D_INNER = 2048
SSM_HEAD_DIM = 64
SSM_HEADS = 32
SSM_GROUPS = 8
SSM_D_STATE = 128
SSM_CONV = 4
SSM_CHUNK = 128
SSM_CONV_DIM = SSM_D_INNER + 2 * SSM_GROUPS * SSM_D_STATE
HEADS_PER_GROUP = SSM_HEADS // SSM_GROUPS
GROUP_WIDTH = HEADS_PER_GROUP * SSM_HEAD_DIM
ATT_HEADS = 8
ATT_HEAD_DIM = 64
ATT_QK_DIM = 1024
ATT_V_DIM = 1024
NUM_BUCKETS = 32
MAX_DISTANCE = 128
MOE_GROUPS = 4
MOE_EXPERTS_PER_GROUP = 8
MOE_EXPERTS = MOE_GROUPS * MOE_EXPERTS_PER_GROUP
MOE_FF = 512
PAGE_SIZE = 128

LANES = 128
SUBLANES = 8
VMEM_LIMIT = 56 * 1024 * 1024

TOKEN_TILE = 256
PROJ_TILE = 512
FFN_TILE = 256
ATT_TILE = 256
MAMBA_SEQS_PER_STEP = 2
SAMPLE_PAGES_PER_STEP = 8


def _cparams(*sem):
    return pltpu.CompilerParams(dimension_semantics=sem, vmem_limit_bytes=VMEM_LIMIT)


def _sigmoid(x):
    return 1.0 / (1.0 + jnp.exp(-x))


def _silu(x):
    return x * _sigmoid(x)


def _softplus(x):
    return jnp.maximum(x, 0.0) + jnp.log(1.0 + jnp.exp(-jnp.abs(x)))


def _layer_norm(u, g, b):
    mu = jnp.mean(u, axis=-1, keepdims=True)
    d = u - mu
    var = jnp.mean(d * d, axis=-1, keepdims=True)
    return d * lax.rsqrt(var + LN_EPS) * g + b


def _dot_nt(a, b):
    return lax.dot_general(a, b, (((1,), (1,)), ((), ())), preferred_element_type=F32)


def _dot_tn(a, b):
    return lax.dot_general(a, b, (((0,), (0,)), ((), ())), preferred_element_type=F32)


def _proj_kernel(x_ref, *refs, n_out, chunk):
    w_refs, o_refs = refs[:n_out], refs[n_out:]
    xb = x_ref[...].astype(BF16)
    for w_ref, o_ref in zip(w_refs, o_refs):
        n = w_ref.shape[1]
        for c0 in range(0, n, chunk):
            c1 = min(n, c0 + chunk)
            o_ref[:, c0:c1] = jnp.dot(xb, w_ref[:, c0:c1], preferred_element_type=F32).astype(o_ref.dtype)


def _proj(x, ws, out_dtypes, name):
    t, k = x.shape
    tm = min(PROJ_TILE, t)
    assert t % tm == 0
    in_specs = [pl.BlockSpec((tm, k), lambda i: (i, 0))]
    in_specs += [pl.BlockSpec(w.shape, lambda i: (0, 0)) for w in ws]
    out_specs = [pl.BlockSpec((tm, w.shape[1]), lambda i: (i, 0)) for w in ws]
    out_shape = [jax.ShapeDtypeStruct((t, w.shape[1]), dt) for w, dt in zip(ws, out_dtypes)]
    return pl.pallas_call(
        functools.partial(_proj_kernel, n_out=len(ws), chunk=512),
        grid=(t // tm,),
        in_specs=in_specs,
        out_specs=out_specs,
        out_shape=out_shape,
        compiler_params=_cparams("parallel"),
        name=name,
    )(x, *ws)


def _mamba_kernel(z_ref, x_ref, b_ref, c_ref, dt_ref, ptab_ref,
                  cwx_ref, cwb_ref, cwc_ref, cbx_ref, cbb_ref, cbc_ref,
                  c0x_ref, c0b_ref, c0c_ref, h0_ref, nw_ref,
                  y_ref, cox_ref, cob_ref, coc_ref, hout_ref,
                  sx, sb, sc, hs, *, q, nc, bb):
    c = pl.program_id(1)
    pad = SUBLANES
    hist = SSM_CONV - 1
    gw, ns, hpg = GROUP_WIDTH, SSM_D_STATE, HEADS_PER_GROUP

    @pl.when(c == 0)
    def _():
        sx[:, pad - hist:pad, :] = c0x_ref[...]
        sb[:, pad - hist:pad, :] = c0b_ref[...]
        sc[:, pad - hist:pad, :] = c0c_ref[...]
        hs[...] = h0_ref[...]

    ti = lax.broadcasted_iota(jnp.int32, (q, q), 0)
    si = lax.broadcasted_iota(jnp.int32, (q, q), 1)
    tri = ti >= si
    tri_f = tri.astype(F32)
    lane = lax.broadcasted_iota(jnp.int32, (1, gw), 1)
    shift = (jnp.concatenate([(ti - si == d).astype(BF16) for d in range(hist, 0, -1)], axis=0)
             if q > SUBLANES else None)
    ptab = ptab_ref[...]

    def expand(v, g):
        r = v.shape[0]
        h0 = hpg * g
        out = jnp.broadcast_to(v[:, h0 + hpg - 1:h0 + hpg], (r, gw))
        for j in range(hpg - 2, -1, -1):
            out = jnp.where(lane < SSM_HEAD_DIM * (j + 1), jnp.broadcast_to(v[:, h0 + j:h0 + j + 1], (r, gw)), out)
        return out

    def conv(cur, s, u, c0, c1, w_ref, bias_ref):
        cur_f = cur.astype(F32)
        s[u, pad:pad + q, c0:c1] = cur_f
        rows = q if shift is None else SUBLANES
        acc = bias_ref[:, c0:c1]
        for k in range(SSM_CONV):
            acc = acc + w_ref[k:k + 1, c0:c1] * s[u, pad - hist + k:pad - hist + k + rows, c0:c1]
        if shift is not None:
            rest = bias_ref[:, c0:c1] + w_ref[hist:hist + 1, c0:c1] * cur_f
            moved = jnp.dot(shift, cur, preferred_element_type=F32)
            for k in range(hist):
                rest = rest + w_ref[k:k + 1, c0:c1] * moved[k * q:(k + 1) * q]
            acc = jnp.concatenate([acc, rest[SUBLANES:]], axis=0)
        tail = s[u, pad + q - hist:pad + q, c0:c1]
        s[u, pad - hist:pad, c0:c1] = tail
        return _silu(acc), tail

    tails = []
    for u in range(bb):
        dt_all = _softplus(dt_ref[u] + ptab[0:1, :])
        da = dt_all * (-jnp.exp(ptab[1:2, :]))
        acum = jnp.dot(tri_f, da, precision=HIGHEST, preferred_element_type=F32)
        acum_t = acum.T
        ac_last = acum[q - 1:q, :]
        cd = jnp.exp(ac_last)
        for g in range(SSM_GROUPS):
            x0, x1 = gw * g, gw * (g + 1)
            n0, n1 = ns * g, ns * (g + 1)
            xc, tail_x = conv(x_ref[u, :, x0:x1], sx, u, x0, x1, cwx_ref, cbx_ref)
            bc, tail_b = conv(b_ref[u, :, n0:n1], sb, u, n0, n1, cwb_ref, cbb_ref)
            cc, tail_c = conv(c_ref[u, :, n0:n1], sc, u, n0, n1, cwc_ref, cbc_ref)

            dt_e = expand(dt_all, g)
            ac_e = expand(acum, g)
            ac_last_e = expand(ac_last, g)

            xdt = xc * dt_e
            xdt_b = xdt.astype(BF16)
            bb_ = bc.astype(BF16)
            cb_ = cc.astype(BF16)
            cbm = _dot_nt(cb_, bb_)
            hprev = hs[u, g]
            y = _dot_nt(cb_, hprev.astype(BF16)) * jnp.exp(ac_e)
            zero_b = jnp.zeros_like(xdt_b)
            for j in range(hpg):
                hd = hpg * g + j
                seg = acum[:, hd:hd + 1] - acum_t[hd:hd + 1, :]
                lm = jnp.exp(jnp.where(tri, seg, -jnp.inf))
                wj = (cbm * lm).astype(BF16)
                xm = jnp.where((lane >= SSM_HEAD_DIM * j) & (lane < SSM_HEAD_DIM * (j + 1)), xdt_b, zero_b)
                y = y + jnp.dot(wj, xm, preferred_element_type=F32)

            xw = (xdt * jnp.exp(ac_last_e - ac_e)).astype(BF16)
            st = _dot_tn(xw, bb_)
            for j in range(hpg):
                hd = hpg * g + j
                r0, r1 = SSM_HEAD_DIM * j, SSM_HEAD_DIM * (j + 1)
                hs[u, g, r0:r1, :] = (jnp.broadcast_to(cd[:, hd:hd + 1], (SSM_HEAD_DIM, ns)) * hprev[r0:r1, :]
                                      + st[r0:r1, :])

            y = y + expand(ptab[2:3, :], g) * xc
            zf = z_ref[u, :, x0:x1].astype(F32)
            y = y * _silu(zf)
            ms = jnp.mean(y * y, axis=-1, keepdims=True)
            y_ref[u, :, x0:x1] = (y * lax.rsqrt(ms + LN_EPS) * nw_ref[:, x0:x1]).astype(y_ref.dtype)
            tails.append((u, x0, x1, n0, n1, tail_x, tail_b, tail_c))

    @pl.when(c == nc - 1)
    def _():
        for u, x0, x1, n0, n1, tail_x, tail_b, tail_c in tails:
            cox_ref[u, :, x0:x1] = tail_x
            cob_ref[u, :, n0:n1] = tail_b
            coc_ref[u, :, n0:n1] = tail_c
        hout_ref[...] = hs[...]


def _mamba_core(zx, dt, conv0, ssm0, ptab, conv_w, conv_b, norm_w, bsz, seqlen):
    q = math.gcd(seqlen, SSM_CHUNK)
    nc = seqlen // q
    g = SSM_GROUPS
    bb = 1 if nc > 1 else math.gcd(bsz, MAMBA_SEQS_PER_STEP)
    di, gn = SSM_D_INNER, g * SSM_D_STATE
    zx3 = zx.reshape(bsz, seqlen, zx.shape[-1])
    dt3 = dt.reshape(bsz, seqlen, LANES)
    h0 = ssm0.reshape(bsz, g, GROUP_WIDTH, SSM_D_STATE)
    hist = SSM_CONV - 1
    in_specs = [
        pl.BlockSpec((bb, q, di), lambda b, c: (b, c, 0)),
        pl.BlockSpec((bb, q, di), lambda b, c: (b, c, 1)),
        pl.BlockSpec((bb, q, gn), lambda b, c: (b, c, 2 * di // gn)),
        pl.BlockSpec((bb, q, gn), lambda b, c: (b, c, 2 * di // gn + 1)),
        pl.BlockSpec((bb, q, LANES), lambda b, c: (b, c, 0)),
        pl.BlockSpec((SUBLANES, LANES), lambda b, c: (0, 0)),
        pl.BlockSpec((SSM_CONV, di), lambda b, c: (0, 0)),
        pl.BlockSpec((SSM_CONV, gn), lambda b, c: (0, di // gn)),
        pl.BlockSpec((SSM_CONV, gn), lambda b, c: (0, di // gn + 1)),
        pl.BlockSpec((1, di), lambda b, c: (0, 0)),
        pl.BlockSpec((1, gn), lambda b, c: (0, di // gn)),
        pl.BlockSpec((1, gn), lambda b, c: (0, di // gn + 1)),
        pl.BlockSpec((bb, hist, di), lambda b, c: (b, 0, 0)),
        pl.BlockSpec((bb, hist, gn), lambda b, c: (b, 0, di // gn)),
        pl.BlockSpec((bb, hist, gn), lambda b, c: (b, 0, di // gn + 1)),
        pl.BlockSpec((bb, g, GROUP_WIDTH, SSM_D_STATE), lambda b, c: (b, 0, 0, 0)),
        pl.BlockSpec((1, di), lambda b, c: (0, 0)),
    ]
    out_specs = [
        pl.BlockSpec((bb, q, di), lambda b, c: (b, c, 0)),
        pl.BlockSpec((bb, hist, di), lambda b, c: (b, 0, 0)),
        pl.BlockSpec((bb, hist, gn), lambda b, c: (b, 0, 0)),
        pl.BlockSpec((bb, hist, gn), lambda b, c: (b, 0, 0)),
        pl.BlockSpec((bb, g, GROUP_WIDTH, SSM_D_STATE), lambda b, c: (b, 0, 0, 0)),
    ]
    out_shape = [
        jax.ShapeDtypeStruct((bsz, seqlen, di), BF16),
        jax.ShapeDtypeStruct((bsz, hist, di), F32),
        jax.ShapeDtypeStruct((bsz, hist, gn), F32),
        jax.ShapeDtypeStruct((bsz, hist, gn), F32),
        jax.ShapeDtypeStruct((bsz, g, GROUP_WIDTH, SSM_D_STATE), F32),
    ]
    scratch = [
        pltpu.VMEM((bb, q + SUBLANES, di), F32),
        pltpu.VMEM((bb, q + SUBLANES, gn), F32),
        pltpu.VMEM((bb, q + SUBLANES, gn), F32),
        pltpu.VMEM((bb, g, GROUP_WIDTH, SSM_D_STATE), F32),
    ]
    y, cox, cob, coc, hout = pl.pallas_call(
        functools.partial(_mamba_kernel, q=q, nc=nc, bb=bb),
        grid=(bsz // bb, nc),
        in_specs=in_specs,
        out_specs=out_specs,
        out_shape=out_shape,
        scratch_shapes=scratch,
        compiler_params=_cparams("parallel", "arbitrary"),
        name="mamba_core",
    )(zx3, zx3, zx3, zx3, dt3, ptab, conv_w, conv_w, conv_w, conv_b, conv_b, conv_b,
      conv0, conv0, conv0, h0, norm_w)
    new_conv = jnp.concatenate([cox, cob, coc], axis=-1)
    return (y.reshape(bsz * seqlen, SSM_D_INNER), new_conv,
            hout.reshape(bsz, SSM_HEADS, SSM_HEAD_DIM, SSM_D_STATE))


def _post_mixer_kernel(y_ref, w_ref, xres_ref, g_ref, b_ref, wr_ref, br_ref,
                       x1_ref, route_ref, route_t_ref, cnt_ref, base, *, tm):
    i = pl.program_id(0)

    @pl.when(i == 0)
    def _():
        base[...] = jnp.zeros_like(base)

    h = jnp.dot(y_ref[...], w_ref[...], preferred_element_type=F32)
    x1 = _layer_norm(DN_ALPHA * xres_ref[...] + h, g_ref[...], b_ref[...])
    x1_ref[...] = x1

    x_hi = x1.astype(BF16)
    x_lo = (x1 - x_hi.astype(F32)).astype(BF16)
    both = jnp.dot(x_hi, wr_ref[...], preferred_element_type=F32)
    logits = (both[:, :LANES] + both[:, LANES:]
              + jnp.dot(x_lo, wr_ref[:, :LANES], preferred_element_type=F32) + br_ref[...])
    lane = lax.broadcasted_iota(jnp.int32, (tm, LANES), 1)
    neg = -jnp.inf
    gmask = lane < MOE_GROUPS
    gl = jnp.where(gmask, logits, neg)
    mg = jnp.max(gl, axis=1, keepdims=True)
    grp = jnp.min(jnp.where(gl == mg, lane, LANES), axis=1, keepdims=True)
    pg = 1.0 / jnp.sum(jnp.where(gmask, jnp.exp(logits - mg), 0.0), axis=1, keepdims=True)
    lo = MOE_GROUPS + MOE_EXPERTS_PER_GROUP * grp
    el = jnp.where((lane >= lo) & (lane < lo + MOE_EXPERTS_PER_GROUP), logits, neg)
    v1 = jnp.max(el, axis=1, keepdims=True)
    i1 = jnp.min(jnp.where(el == v1, lane, LANES), axis=1, keepdims=True)
    el2 = jnp.where(lane == i1, neg, el)
    v2 = jnp.max(el2, axis=1, keepdims=True)
    i2 = jnp.min(jnp.where(el2 == v2, lane, LANES), axis=1, keepdims=True)
    t = jnp.exp(v2 - v1)
    wa = pg / (1.0 + t)
    wb = pg * t / (1.0 + t)
    e1 = i1 - MOE_GROUPS
    e2 = i2 - MOE_GROUPS

    oh = ((lane == e1) | (lane == e2)).astype(BF16)
    ri = lax.broadcasted_iota(jnp.int32, (tm, tm), 0)
    ci = lax.broadcasted_iota(jnp.int32, (tm, tm), 1)
    before = jnp.dot((ri > ci).astype(BF16), oh, preferred_element_type=F32) + base[...]
    r1 = jnp.sum(jnp.where(lane == e1, before, 0.0), axis=1, keepdims=True)
    r2 = jnp.sum(jnp.where(lane == e2, before, 0.0), axis=1, keepdims=True)
    new_base = base[...] + jnp.sum(oh.astype(F32), axis=0, keepdims=True)
    base[...] = new_base
    cnt_ref[...] = new_base

    route = jnp.where(lane == 0, e1.astype(F32), 0.0)
    route = jnp.where(lane == 1, e2.astype(F32), route)
    route = jnp.where(lane == 2, r1, route)
    route = jnp.where(lane == 3, r2, route)
    route = jnp.where(lane == 4, wa, route)
    route = jnp.where(lane == 5, wb, route)
    route_ref[...] = route
    route_t_ref[...] = route.T[0:SUBLANES, :]


def _post_mixer(y, w, xres, ln_g, ln_b, wr, br, name):
    t, kin = y.shape
    tm = min(PROJ_TILE, t)
    assert t % tm == 0
    return pl.pallas_call(
        functools.partial(_post_mixer_kernel, tm=tm),
        grid=(t // tm,),
        in_specs=[
            pl.BlockSpec((tm, kin), lambda i: (i, 0)),
            pl.BlockSpec((kin, D_MODEL), lambda i: (0, 0)),
            pl.BlockSpec((tm, D_MODEL), lambda i: (i, 0)),
            pl.BlockSpec((1, D_MODEL), lambda i: (0, 0)),
            pl.BlockSpec((1, D_MODEL), lambda i: (0, 0)),
            pl.BlockSpec((D_MODEL, 2 * LANES), lambda i: (0, 0)),
            pl.BlockSpec((1, LANES), lambda i: (0, 0)),
        ],
        out_specs=[
            pl.BlockSpec((tm, D_MODEL), lambda i: (i, 0)),
            pl.BlockSpec((tm, LANES), lambda i: (i, 0)),
            pl.BlockSpec((SUBLANES, tm), lambda i: (0, i)),
            pl.BlockSpec((1, LANES), lambda i: (0, 0)),
        ],
        out_shape=[
            jax.ShapeDtypeStruct((t, D_MODEL), F32),
            jax.ShapeDtypeStruct((t, LANES), F32),
            jax.ShapeDtypeStruct((SUBLANES, t), F32),
            jax.ShapeDtypeStruct((1, LANES), F32),
        ],
        scratch_shapes=[pltpu.VMEM((1, LANES), F32)],
        compiler_params=_cparams("arbitrary"),
        name=name,
    )(y, w, xres, ln_g, ln_b, wr, br)


def _row_copy(src_ref, src_row, dst_ref, dst_row, sem):
    return pltpu.make_async_copy(src_ref.at[pl.ds(src_row, 1)], dst_ref.at[pl.ds(dst_row, 1)], sem)


def _dispatch_kernel(dest_ref, x_ref, xs_ref, sem, *, tm):
    def start(r, carry):
        _row_copy(x_ref, r, xs_ref, dest_ref[0, 0, r], sem).start()
        _row_copy(x_ref, r, xs_ref, dest_ref[0, 0, tm + r], sem).start(priority=1)
        return carry

    lax.fori_loop(0, tm, start, 0, unroll=True)
    for _ in range(2):
        pltpu.make_async_copy(x_ref, xs_ref.at[pl.ds(0, tm)], sem).wait()


def _dispatch(x1, dest3, tm):
    t = x1.shape[0]
    return pl.pallas_call(
        functools.partial(_dispatch_kernel, tm=tm),
        grid=(t // tm,),
        in_specs=[
            pl.BlockSpec((1, 1, 2 * tm), lambda i: (i, 0, 0), memory_space=pltpu.SMEM),
            pl.BlockSpec((tm, D_MODEL), lambda i: (i, 0)),
        ],
        out_specs=pl.BlockSpec(memory_space=pl.ANY),
        out_shape=jax.ShapeDtypeStruct((2 * t, D_MODEL), F32),
        scratch_shapes=[pltpu.SemaphoreType.DMA(())],
        compiler_params=_cparams("arbitrary"),
        name="moe_dispatch",
    )(dest3, x1)


def _ffn_kernel(wt_ref, we_ref, wlo_ref, whi_ref, wfirst_ref, wnew_ref,
                xs_ref, w1_ref, w3_ref, w2_ref, ys_ref, w1b, w3b, w2b, *, tf):
    w = pl.program_id(0)
    lo = wlo_ref[w]
    hi = whi_ref[w]

    @pl.when(wfirst_ref[w] == 1)
    def _():
        ys_ref[...] = jnp.zeros_like(ys_ref)

    @pl.when(wnew_ref[w] == 1)
    def _():
        w1b[...] = w1_ref[0].astype(BF16)
        w3b[...] = w3_ref[0].astype(BF16)
        w2b[...] = w2_ref[0].astype(BF16)

    @pl.when(hi > lo)
    def _():
        rows = wt_ref[w] * tf + lax.broadcasted_iota(jnp.int32, (tf, 1), 0)
        mask = (rows >= lo) & (rows < hi)
        xb = xs_ref[...].astype(BF16)
        a = jnp.dot(xb, w1b[...], preferred_element_type=F32)
        b = jnp.dot(xb, w3b[...], preferred_element_type=F32)
        hmid = (_silu(a) * b).astype(BF16)
        y = jnp.dot(hmid, w2b[...], preferred_element_type=F32)
        ys_ref[...] = jnp.where(mask, y, ys_ref[...])


def _ffn(xs, w1, w3, w2, work, tf):
    n = xs.shape[0]
    n_work = work[0].shape[0]
    grid_spec = pltpu.PrefetchScalarGridSpec(
        num_scalar_prefetch=6,
        grid=(n_work,),
        in_specs=[
            pl.BlockSpec((tf, D_MODEL), lambda w, wt, we, wlo, whi, wf, wn: (wt[w], 0)),
            pl.BlockSpec((1, D_MODEL, MOE_FF), lambda w, wt, we, wlo, whi, wf, wn: (we[w], 0, 0)),
            pl.BlockSpec((1, D_MODEL, MOE_FF), lambda w, wt, we, wlo, whi, wf, wn: (we[w], 0, 0)),
            pl.BlockSpec((1, MOE_FF, D_MODEL), lambda w, wt, we, wlo, whi, wf, wn: (we[w], 0, 0)),
        ],
        out_specs=pl.BlockSpec((tf, D_MODEL), lambda w, wt, we, wlo, whi, wf, wn: (wt[w], 0)),
        scratch_shapes=[
            pltpu.VMEM((D_MODEL, MOE_FF), BF16),
            pltpu.VMEM((D_MODEL, MOE_FF), BF16),
            pltpu.VMEM((MOE_FF, D_MODEL), BF16),
        ],
    )
    return pl.pallas_call(
        functools.partial(_ffn_kernel, tf=tf),
        grid_spec=grid_spec,
        out_shape=jax.ShapeDtypeStruct((n, D_MODEL), F32),
        compiler_params=_cparams("arbitrary"),
        name="moe_ffn",
    )(*work, xs, w1, w3, w2)


def _combine_kernel(dcur_ref, dnext_ref, x1_ref, route_ref, g_ref, b_ref, ys_ref, o_ref, buf, sems, *, tm, n_tiles):
    i = pl.program_id(0)
    slot = lax.rem(i, 2)

    def issue(dref, s):
        def start(r, carry):
            _row_copy(ys_ref, dref[0, 0, r], buf.at[s, 0], r, sems.at[s]).start()
            _row_copy(ys_ref, dref[0, 0, tm + r], buf.at[s, 1], r, sems.at[s]).start(priority=1)
            return carry

        lax.fori_loop(0, tm, start, 0, unroll=True)

    @pl.when(i == 0)
    def _():
        issue(dcur_ref, 0)

    @pl.when(i + 1 < n_tiles)
    def _():
        issue(dnext_ref, 1 - slot)

    for k in range(2):
        pltpu.make_async_copy(ys_ref.at[pl.ds(0, tm)], buf.at[slot, k], sems.at[slot]).wait()

    route = route_ref[...]
    y = route[:, 4:5] * buf[slot, 0] + route[:, 5:6] * buf[slot, 1]
    o_ref[...] = _layer_norm(DN_ALPHA * x1_ref[...] + y, g_ref[...], b_ref[...])


def _combine(x1, route, ys, dest3, ln_g, ln_b, tm, name):
    t = x1.shape[0]
    n_tiles = t // tm
    return pl.pallas_call(
        functools.partial(_combine_kernel, tm=tm, n_tiles=n_tiles),
        grid=(n_tiles,),
        in_specs=[
            pl.BlockSpec((1, 1, 2 * tm), lambda i: (i, 0, 0), memory_space=pltpu.SMEM),
            pl.BlockSpec((1, 1, 2 * tm), lambda i: (jnp.minimum(i + 1, n_tiles - 1), 0, 0), memory_space=pltpu.SMEM),
            pl.BlockSpec((tm, D_MODEL), lambda i: (i, 0)),
            pl.BlockSpec((tm, LANES), lambda i: (i, 0)),
            pl.BlockSpec((1, D_MODEL), lambda i: (0, 0)),
            pl.BlockSpec((1, D_MODEL), lambda i: (0, 0)),
            pl.BlockSpec(memory_space=pl.ANY),
        ],
        out_specs=pl.BlockSpec((tm, D_MODEL), lambda i: (i, 0)),
        out_shape=jax.ShapeDtypeStruct((t, D_MODEL), F32),
        scratch_shapes=[
            pltpu.VMEM((2, 2, tm, D_MODEL), F32),
            pltpu.SemaphoreType.DMA((2,)),
        ],
        compiler_params=_cparams("arbitrary"),
        name=name,
    )(dest3, dest3, x1, route, ln_g, ln_b, ys)


def _moe(x1, route, route_t, counts, w1, w3, w2, expert_base, ln_g, ln_b, name):
    t = x1.shape[0]
    n = 2 * t
    tm = min(TOKEN_TILE, t)
    tf = min(FFN_TILE, n)
    cnt = counts[0, :MOE_EXPERTS].astype(jnp.int32)
    offs = jnp.concatenate([jnp.zeros((1,), jnp.int32), jnp.cumsum(cnt)])
    e = route_t[0:2].astype(jnp.int32)
    r = route_t[2:4].astype(jnp.int32)
    starts = jnp.where(e[None] == jnp.arange(MOE_EXPERTS, dtype=jnp.int32)[:, None, None], offs[:-1, None, None], 0)
    dest = jnp.sum(starts, axis=0) + r
    dest3 = jnp.transpose(dest.reshape(2, t // tm, tm), (1, 0, 2)).reshape(t // tm, 1, 2 * tm)

    n_tiles = n // tf
    n_work = n_tiles + MOE_EXPERTS
    first_tile = offs[:-1] // tf
    last_tile = (offs[1:] - 1) // tf
    ntile_e = jnp.where(cnt > 0, last_tile - first_tile + 1, 0)
    wstart = jnp.concatenate([jnp.zeros((1,), jnp.int32), jnp.cumsum(ntile_e)])
    total = wstart[-1]
    widx = jnp.arange(n_work, dtype=jnp.int32)
    we = jnp.sum((wstart[None, 1:] <= widx[:, None]).astype(jnp.int32), axis=1)
    we = jnp.minimum(we, MOE_EXPERTS - 1)
    onehot = we[:, None] == jnp.arange(MOE_EXPERTS, dtype=jnp.int32)[None, :]

    def pick(v):
        return jnp.sum(jnp.where(onehot, v[None, :], 0), axis=1)

    valid = widx < total
    wt = jnp.where(valid, pick(first_tile) + widx - pick(wstart[:-1]), n_tiles - 1).astype(jnp.int32)
    wlo = jnp.where(valid, pick(offs[:-1]), 0).astype(jnp.int32)
    whi = jnp.where(valid, pick(offs[1:]), 0).astype(jnp.int32)
    last_e = jnp.max(jnp.where(valid, we, 0))
    we = jnp.where(valid, we, last_e).astype(jnp.int32)
    wfirst = jnp.concatenate([jnp.ones((1,), jnp.int32), (wt[1:] != wt[:-1]).astype(jnp.int32)])

    xs = _dispatch(x1, dest3, tm)
    wnew = jnp.concatenate([jnp.ones((1,), jnp.int32), (we[1:] != we[:-1]).astype(jnp.int32)])
    ys = _ffn(xs, w1, w3, w2, (wt, we + expert_base, wlo, whi, wfirst, wnew), tf)
    return _combine(x1, route, ys, dest3, ln_g, ln_b, tm, name)


def _bias_kernel(idx_ref, rbt_ref, o_ref, *, width):
    idx = idx_ref[...]
    bucket = lax.broadcasted_iota(jnp.int32, (NUM_BUCKETS, width), 0)
    onehot = (bucket == idx).astype(F32)
    vals = jnp.dot(rbt_ref[...], onehot, precision=HIGHEST, preferred_element_type=F32)
    o_ref[...] = jnp.where(idx < 0, -jnp.inf, vals)


def _bias_tiles(rel_bias, idx_np):
    n = idx_np.shape[0]
    width = 8192
    assert n % width == 0
    nh = rel_bias.shape[1]
    return pl.pallas_call(
        functools.partial(_bias_kernel, width=width),
        grid=(n // width,),
        in_specs=[
            pl.BlockSpec((1, width), lambda i: (0, i)),
            pl.BlockSpec((nh, NUM_BUCKETS), lambda i: (0, 0)),
        ],
        out_specs=pl.BlockSpec((nh, width), lambda i: (0, i)),
        out_shape=jax.ShapeDtypeStruct((nh, n), F32),
        compiler_params=_cparams("parallel"),
        name="rel_bias_tiles",
    )(jnp.asarray(idx_np.reshape(1, n)), rel_bias.T)


def _bucket_of_distance(dist):
    n = np.maximum(dist, 0)
    max_exact = NUM_BUCKETS // 2
    nf = np.maximum(n, 1).astype(np.float32)
    large = max_exact + (np.log(nf / np.float32(max_exact)) / np.float32(math.log(MAX_DISTANCE / max_exact))
                         * np.float32(NUM_BUCKETS - max_exact)).astype(np.int32)
    large = np.minimum(large, NUM_BUCKETS - 1)
    return np.where(n < max_exact, n, large).astype(np.int32)


def _bucket_tile(dist):
    return np.where(dist >= 0, _bucket_of_distance(dist), -1).astype(np.int32)


def _kvq_prompt_kernel(x_ref, wkt_ref, wv_ref, wq_ref, kt_ref, ktb_ref, v_ref, vb_ref, q_ref, *, chunk):
    xb = x_ref[...].astype(BF16)
    hd2 = 2 * ATT_HEAD_DIM
    for c0 in range(0, ATT_QK_DIM, chunk):
        kt = _dot_nt(wkt_ref[c0:c0 + chunk, :], xb)
        kt_ref[0, c0:c0 + chunk, :] = kt
        ktb_ref[0, 0, c0:c0 + chunk, :] = kt.astype(BF16)
    for c0 in range(0, ATT_V_DIM, chunk):
        v = jnp.dot(xb, wv_ref[:, c0:c0 + chunk], preferred_element_type=F32)
        vb_ref[:, c0:c0 + chunk] = v.astype(BF16)
        for h0 in range(0, chunk, hd2):
            v_ref[:, (c0 + h0) // hd2, :] = v[:, h0:h0 + hd2]
    for c0 in range(0, ATT_QK_DIM, chunk):
        q_ref[:, c0:c0 + chunk] = jnp.dot(xb, wq_ref[:, c0:c0 + chunk], preferred_element_type=F32).astype(BF16)


def _kvq_prompt(x, wkt, wv, wq, bsz, seqlen):
    t = bsz * seqlen
    tm = min(ATT_TILE, seqlen)
    nkb = seqlen // tm
    hd2 = 2 * ATT_HEAD_DIM
    full = lambda i: (0, 0)
    return pl.pallas_call(
        functools.partial(_kvq_prompt_kernel, chunk=512),
        grid=(t // tm,),
        in_specs=[
            pl.BlockSpec((tm, D_MODEL), lambda i: (i, 0)),
            pl.BlockSpec(wkt.shape, full),
            pl.BlockSpec(wv.shape, full),
            pl.BlockSpec(wq.shape, full),
        ],
        out_specs=[
            pl.BlockSpec((1, ATT_QK_DIM, tm), lambda i: (i // nkb, 0, i % nkb)),
            pl.BlockSpec((1, 1, ATT_QK_DIM, tm), lambda i: (i // nkb, i % nkb, 0, 0)),
            pl.BlockSpec((tm, ATT_HEADS, hd2), lambda i: (i, 0, 0)),
            pl.BlockSpec((tm, ATT_V_DIM), lambda i: (i, 0)),
            pl.BlockSpec((tm, ATT_QK_DIM), lambda i: (i, 0)),
        ],
        out_shape=[
            jax.ShapeDtypeStruct((bsz, ATT_QK_DIM, seqlen), F32),
            jax.ShapeDtypeStruct((bsz, nkb, ATT_QK_DIM, tm), BF16),
            jax.ShapeDtypeStruct((t, ATT_HEADS, hd2), F32),
            jax.ShapeDtypeStruct((t, ATT_V_DIM), BF16),
            jax.ShapeDtypeStruct((t, ATT_QK_DIM), BF16),
        ],
        compiler_params=_cparams("parallel"),
        name="kvq_proj_prompt",
    )(x, wkt, wv, wq)


def _lambda_value(lam_ref, layer_idx):
    lv = lam_ref[...]
    s1 = jnp.sum(lv[0:1, :] * lv[1:2, :], axis=1, keepdims=True)
    s2 = jnp.sum(lv[2:3, :] * lv[3:4, :], axis=1, keepdims=True)
    lam_init = 0.8 - 0.6 * math.exp(-0.3 * layer_idx)
    return jnp.exp(s1) - jnp.exp(s2) + lam_init, lam_init


def _attn_prompt_kernel(q_ref, kt_ref, v_ref, bias_ref, lam_ref, sw_ref, o_ref, *, tq, nq, layer_idx):
    lane = lax.broadcasted_iota(jnp.int32, (tq, 2 * ATT_HEAD_DIM), 1)
    lam, lam_init = _lambda_value(lam_ref, layer_idx)
    for qi in range(nq):
        qb = q_ref[0, qi * tq:(qi + 1) * tq, :]
        zero = jnp.zeros_like(qb)
        q2 = jnp.concatenate([jnp.where(lane < ATT_HEAD_DIM, qb, zero),
                              jnp.where(lane >= ATT_HEAD_DIM, qb, zero)], axis=0)
        m = l = acc = None
        for j in range(qi + 1):
            s = jnp.dot(q2, kt_ref[0, j], preferred_element_type=F32)
            if qi - j < 2:
                s = s + bias_ref[0, qi - j]
            vj = v_ref[0, j * tq:(j + 1) * tq, :]
            bm = jnp.max(s, axis=1, keepdims=True)
            if j == 0:
                m = bm
                p = jnp.exp2(s - m)
                l = jnp.sum(p, axis=1, keepdims=True)
                acc = jnp.dot(p.astype(BF16), vj, preferred_element_type=F32)
            else:
                m_new = jnp.maximum(m, bm)
                alpha = jnp.exp2(m - m_new)
                p = jnp.exp2(s - m_new)
                l = alpha * l + jnp.sum(p, axis=1, keepdims=True)
                acc = alpha * acc + jnp.dot(p.astype(BF16), vj, preferred_element_type=F32)
                m = m_new
        o = acc / l
        d = o[:tq] - lam * o[tq:]
        ms = jnp.mean(d * d, axis=-1, keepdims=True)
        o_ref[0, qi * tq:(qi + 1) * tq, :] = (
            d * lax.rsqrt(ms + LN_EPS) * sw_ref[...] * (1.0 - lam_init)).astype(o_ref.dtype)


def _attn_prompt(q, ktb, vb, bias, lam_vec, subln_w, bsz, seqlen, layer_idx):
    tq = min(ATT_TILE, seqlen)
    nq = seqlen // tq
    hd2 = 2 * ATT_HEAD_DIM
    q3 = q.reshape(bsz, seqlen, ATT_QK_DIM)
    v3 = vb.reshape(bsz, seqlen, ATT_V_DIM)
    out = pl.pallas_call(
        functools.partial(_attn_prompt_kernel, tq=tq, nq=nq, layer_idx=layer_idx),
        grid=(bsz, ATT_HEADS),
        in_specs=[
            pl.BlockSpec((1, seqlen, hd2), lambda b, h: (b, 0, h)),
            pl.BlockSpec((1, nq, hd2, tq), lambda b, h: (b, 0, h, 0)),
            pl.BlockSpec((1, seqlen, hd2), lambda b, h: (b, 0, h)),
            pl.BlockSpec((1, 2, 2 * tq, tq), lambda b, h: (h, 0, 0, 0)),
            pl.BlockSpec(lam_vec.shape, lambda b, h: (0, 0)),
            pl.BlockSpec((1, hd2), lambda b, h: (0, 0)),
        ],
        out_specs=pl.BlockSpec((1, seqlen, hd2), lambda b, h: (b, 0, h)),
        out_shape=jax.ShapeDtypeStruct((bsz, seqlen, ATT_V_DIM), BF16),
        compiler_params=_cparams("parallel", "parallel"),
        name="attn_prompt",
    )(q3, ktb, v3, bias, lam_vec, subln_w)
    return out.reshape(bsz * seqlen, ATT_V_DIM)


def _attn_sample_kernel(pt_ref, q_ref, *refs, nq, n_steps, pp, layer_idx):
    kt_refs = refs[:pp]
    vm_refs = refs[pp:2 * pp]
    (kn_ref, vn_ref, bias_ref, exp_ref, pmask_ref, lam_ref, sw_ref,
     o_ref, qbd, m_s, l_s, acc_s, kpad, vpad) = refs[2 * pp:]
    j = pl.program_id(1)
    maps = 2 * ATT_HEADS
    rows = maps * nq
    hd2 = 2 * ATT_HEAD_DIM
    page = PAGE_SIZE

    @pl.when(j == 0)
    def _():
        m_s[...] = jnp.full_like(m_s, -jnp.inf)
        l_s[...] = jnp.zeros_like(l_s)
        acc_s[...] = jnp.zeros_like(acc_s)
        qt = jnp.concatenate([q_ref[0].astype(F32)] * maps, axis=0)
        ri = lax.broadcasted_iota(jnp.int32, (rows, ATT_QK_DIM), 0)
        ci = lax.broadcasted_iota(jnp.int32, (rows, ATT_QK_DIM), 1)
        keep = (ri // nq) == (ci // ATT_HEAD_DIM)
        qbd[...] = jnp.where(keep, qt, 0.0).astype(qbd.dtype)

    def softmax_update(s):
        m = m_s[...]
        m_new = jnp.maximum(m, jnp.max(s, axis=1, keepdims=True))
        alpha = jnp.exp2(m - m_new)
        p = jnp.exp2(s - m_new)
        l_s[...] = alpha * l_s[...] + jnp.sum(p, axis=1, keepdims=True)
        m_s[...] = m_new
        return alpha, p.astype(BF16)

    def past_pages():
        kt_all = jnp.concatenate(
            [kt_refs[u][0].reshape(ATT_QK_DIM, page).astype(BF16) for u in range(pp)], axis=1)
        s = jnp.dot(qbd[...], kt_all, preferred_element_type=F32)
        last = bias_ref[jnp.where(j == n_steps - 1, 0, 2)]
        s = jnp.concatenate([s[:, :page * (pp - 1)], s[:, page * (pp - 1):] + last], axis=1)
        alpha, pb = softmax_update(s)
        p_rows = jnp.concatenate([pb[:, page * u:page * (u + 1)] for u in range(pp)], axis=0)
        pe = jnp.dot(p_rows, exp_ref[...], preferred_element_type=F32)
        pe_cat = jnp.concatenate(
            [pe[rows * u:rows * (u + 1)].astype(BF16) * pmask_ref[...] for u in range(pp)], axis=1)
        vm_all = jnp.concatenate(
            [vm_refs[u][0].reshape(page * ATT_HEADS, hd2).astype(BF16) for u in range(pp)], axis=0)
        acc_s[...] = alpha * acc_s[...] + jnp.dot(pe_cat, vm_all, preferred_element_type=F32)

    past_pages()

    @pl.when(j == n_steps - 1)
    def _():
        kpad[...] = jnp.zeros_like(kpad)
        vpad[...] = jnp.zeros_like(vpad)
        kpad[0:nq, :] = kn_ref[0]
        vpad[0:nq, :] = vn_ref[0]
        s = _dot_nt(qbd[...], kpad[...].astype(BF16)) + bias_ref[1]
        alpha, pb = softmax_update(s)
        for h in range(ATT_HEADS):
            r0, r1 = 2 * nq * h, 2 * nq * (h + 1)
            pvh = jnp.dot(pb[r0:r1, :], vpad[:, hd2 * h:hd2 * (h + 1)].astype(BF16), preferred_element_type=F32)
            acc_s[r0:r1, :] = alpha[r0:r1, :] * acc_s[r0:r1, :] + pvh
        o = acc_s[...] / l_s[...]
        lam, lam_init = _lambda_value(lam_ref, layer_idx)
        for h in range(ATT_HEADS):
            r0 = 2 * nq * h
            d = o[r0:r0 + nq, :] - lam * o[r0 + nq:r0 + 2 * nq, :]
            ms = jnp.mean(d * d, axis=-1, keepdims=True)
            o_ref[0, :, hd2 * h:hd2 * (h + 1)] = (
                d * lax.rsqrt(ms + LN_EPS) * sw_ref[...] * (1.0 - lam_init)).astype(o_ref.dtype)


def _attn_sample(q, kv, cache_k, cache_v, page_table, bias, lam_vec, subln_w, bsz, nq, layer_idx):
    n_pages = page_table.shape[1]
    page = cache_k.shape[1]
    maps = 2 * ATT_HEADS
    rows = maps * nq
    hd2 = 2 * ATT_HEAD_DIM
    pp = math.gcd(n_pages, SAMPLE_PAGES_PER_STEP)
    n_steps = n_pages // pp
    q3 = q.reshape(bsz, nq, ATT_QK_DIM)
    kv3 = kv.reshape(bsz, nq, ATT_QK_DIM + ATT_V_DIM)
    cache_kt = jnp.transpose(cache_k, (0, 2, 3, 1))
    kk = np.arange(page)[:, None]
    cc = np.arange(page * ATT_HEADS)[None, :]
    expand = jnp.asarray((cc // ATT_HEADS == kk).astype(np.float32), dtype=BF16)
    rr = np.arange(rows)[:, None]
    pmask = jnp.asarray(((rr // (2 * nq)) == (cc % ATT_HEADS)).astype(np.float32), dtype=BF16)

    def page_map(u):
        return lambda b, j, pt: (pt[b, j * pp + u], 0, 0, 0)

    in_specs = [pl.BlockSpec((1, nq, ATT_QK_DIM), lambda b, j, pt: (b, 0, 0))]
    in_specs += [pl.BlockSpec((1, maps, ATT_HEAD_DIM, page), page_map(u)) for u in range(pp)]
    in_specs += [pl.BlockSpec((1, page, ATT_HEADS, hd2), page_map(u)) for u in range(pp)]
    in_specs += [
        pl.BlockSpec((1, nq, ATT_QK_DIM), lambda b, j, pt: (b, 0, 0)),
        pl.BlockSpec((1, nq, ATT_V_DIM), lambda b, j, pt: (b, 0, 1)),
        pl.BlockSpec((3, rows, page), lambda b, j, pt: (0, 0, 0)),
        pl.BlockSpec((page, page * ATT_HEADS), lambda b, j, pt: (0, 0)),
        pl.BlockSpec((rows, page * ATT_HEADS), lambda b, j, pt: (0, 0)),
        pl.BlockSpec(lam_vec.shape, lambda b, j, pt: (0, 0)),
        pl.BlockSpec((1, hd2), lambda b, j, pt: (0, 0)),
    ]
    grid_spec = pltpu.PrefetchScalarGridSpec(
        num_scalar_prefetch=1,
        grid=(bsz, n_steps),
        in_specs=in_specs,
        out_specs=pl.BlockSpec((1, nq, ATT_V_DIM), lambda b, j, pt: (b, 0, 0)),
        scratch_shapes=[
            pltpu.VMEM((rows, ATT_QK_DIM), BF16),
            pltpu.VMEM((rows, 1), F32),
            pltpu.VMEM((rows, 1), F32),
            pltpu.VMEM((rows, hd2), F32),
            pltpu.VMEM((page, ATT_QK_DIM), F32),
            pltpu.VMEM((page, ATT_V_DIM), F32),
        ],
    )
    out = pl.pallas_call(
        functools.partial(_attn_sample_kernel, nq=nq, n_steps=n_steps, pp=pp, layer_idx=layer_idx),
        grid_spec=grid_spec,
        out_shape=jax.ShapeDtypeStruct((bsz, nq, ATT_V_DIM), BF16),
        compiler_params=_cparams("parallel", "arbitrary"),
        name="attn_sample",
    )(page_table, q3, *([cache_kt] * pp), *([cache_v] * pp), kv3, kv3, bias, expand, pmask, lam_vec, subln_w)
    return out.reshape(bsz * nq, ATT_V_DIM)


def _router_params(wg, bg, we, be):
    wexp = jnp.transpose(we, (1, 0, 2)).reshape(D_MODEL, MOE_EXPERTS)
    wr = jnp.concatenate([wg, wexp], axis=1)
    wr = jnp.pad(wr, ((0, 0), (0, LANES - wr.shape[1])))
    w_hi = wr.astype(BF16)
    w_lo = (wr - w_hi.astype(F32)).astype(BF16)
    wr = jnp.concatenate([w_hi, w_lo], axis=1)
    br = jnp.concatenate([bg, be.reshape(MOE_EXPERTS)])
    br = jnp.pad(br, (0, LANES - br.shape[0])).reshape(1, LANES)
    return wr, br


def _head_tables(dt_bias, a_log, d_skip):
    rows = jnp.stack([dt_bias, a_log, d_skip])
    return jnp.pad(rows, ((0, SUBLANES - 3), (0, LANES - SSM_HEADS)))


def _prompt_bias_index(tq):
    r = np.arange(tq)[:, None]
    c = np.arange(tq)[None, :]
    tiles = [_bucket_tile(delta * tq + r - c) for delta in range(3)]
    return np.stack(tiles).reshape(-1)


def _sample_bias_index(nq, page, past_len):
    r = np.arange(nq)[:, None]
    c = np.arange(page)[None, :]
    last_page = _bucket_tile(past_len + r - (past_len - page) - c)
    own = np.where(c < nq, _bucket_tile(r - c), -1).astype(np.int32)
    far = _bucket_tile(np.full((nq, page), MAX_DISTANCE + page))
    return np.stack([last_page, own, far]).reshape(-1)


def kernel(x_prompt, x_sample, state_ssm, state_conv, cache_k, cache_v, page_table, ln_g, ln_b, m_w_in,
           m_conv_w, m_conv_b, m_dt_bias, m_a_log, m_d, m_norm_w, m_w_out, kv_w, a_w_q, a_lambda,
           a_subln_w, a_w_o, rel_bias, moe_wg, moe_bg, moe_we, moe_be, moe_w1, moe_w3, moe_w2):
    bp, lp, _ = x_prompt.shape
    bs, ls, _ = x_sample.shape
    n_pages = page_table.shape[1]
    page = cache_k.shape[1]
    past_len = n_pages * page
    assert page == PAGE_SIZE and page >= MAX_DISTANCE and min(ATT_TILE, lp) >= MAX_DISTANCE

    w_in = m_w_in[0]
    w_zx = w_in[:, :SSM_D_INNER + SSM_CONV_DIM].astype(BF16)
    w_dt = jnp.pad(w_in[:, SSM_D_INNER + SSM_CONV_DIM:], ((0, 0), (0, LANES - SSM_HEADS))).astype(BF16)
    ptab = _head_tables(m_dt_bias[0], m_a_log[0], m_d[0])
    conv_w = m_conv_w[0]
    conv_b = m_conv_b[0].reshape(1, SSM_CONV_DIM)
    norm_w = m_norm_w[0].reshape(1, SSM_D_INNER)
    w_out = m_w_out[0].astype(BF16)
    w_q = (a_w_q[0] * (LOG2E * ATT_HEAD_DIM ** -0.5)).astype(BF16)
    w_kv = kv_w.astype(BF16)
    w_kt = kv_w[:, :ATT_QK_DIM].T.astype(BF16)
    w_v = w_kv[:, ATT_QK_DIM:]
    w_o = a_w_o[0].astype(BF16)
    lam_vec = a_lambda[0]
    subln_w = a_subln_w[0].reshape(1, 2 * ATT_HEAD_DIM)
    routers = [_router_params(moe_wg[l], moe_bg[l], moe_we[l], moe_be[l]) for l in range(DEPTH)]
    experts = (moe_w1.reshape(DEPTH * MOE_EXPERTS, D_MODEL, MOE_FF),
               moe_w3.reshape(DEPTH * MOE_EXPERTS, D_MODEL, MOE_FF),
               moe_w2.reshape(DEPTH * MOE_EXPERTS, MOE_FF, D_MODEL))
    lng = ln_g.reshape(DEPTH, 2, 1, D_MODEL)
    lnb = ln_b.reshape(DEPTH, 2, 1, D_MODEL)

    tq = min(ATT_TILE, lp)
    maps = 2 * ATT_HEADS
    idx_p = _prompt_bias_index(tq)
    idx_s = _sample_bias_index(ls, page, past_len)
    n_p = idx_p.shape[0]
    n_s = idx_s.shape[0]
    n_tot = -(-(n_p + n_s) // 8192) * 8192
    idx_all = np.concatenate([idx_p, idx_s, np.zeros((n_tot - n_p - n_s,), np.int32)])
    tiles = _bias_tiles(rel_bias, idx_all)
    bias_p = tiles[:, :n_p].reshape(ATT_HEADS, 2, 3, tq, tq)
    bias_p = (bias_p[:, :, 0:2] - bias_p[:, :, 2:3]) * LOG2E
    bias_p = jnp.transpose(bias_p, (0, 2, 1, 3, 4)).reshape(ATT_HEADS, 2, 2 * tq, tq)
    bias_s = tiles[:, n_p:n_p + n_s].reshape(maps, 3, ls, page)
    bias_s = (bias_s - bias_s[:, 2:3]) * LOG2E
    bias_s = jnp.transpose(bias_s, (1, 0, 2, 3)).reshape(3, maps * ls, page)

    def trunk(x, ssm0, conv0, bsz, seqlen, past):
        t = bsz * seqlen
        xf = x.reshape(t, D_MODEL)
        zx, dt = _proj(xf, [w_zx, w_dt], [BF16, F32], "in_proj")
        y, new_conv, new_ssm = _mamba_core(zx, dt, conv0, ssm0, ptab, conv_w, conv_b, norm_w, bsz, seqlen)
        x1, route, route_t, counts = _post_mixer(y, w_out, xf, lng[0, 0], lnb[0, 0], *routers[0], "mamba_out_ln_router")
        x2 = _moe(x1, route, route_t, counts, *experts, 0, lng[0, 1], lnb[0, 1], "moe0_combine_ln")
        if past is None:
            kt, ktb, v4, vb, qs = _kvq_prompt(x2, w_kt, w_v, w_q, bsz, seqlen)
            k_new = jnp.transpose(kt.reshape(bsz, maps, ATT_HEAD_DIM, seqlen), (0, 3, 1, 2))
            v_new = v4.reshape(bsz, seqlen, ATT_HEADS, 2 * ATT_HEAD_DIM)
            o = _attn_prompt(qs, ktb, vb, bias_p, lam_vec, subln_w, bsz, seqlen, DEPTH - 1)
        else:
            kv, qs = _proj(x2, [w_kv, w_q], [F32, BF16], "kvq_proj")
            k_new = kv[:, :ATT_QK_DIM].reshape(bsz, seqlen, maps, ATT_HEAD_DIM)
            v_new = kv[:, ATT_QK_DIM:].reshape(bsz, seqlen, ATT_HEADS, 2 * ATT_HEAD_DIM)
            o = _attn_sample(qs, kv, cache_k, cache_v, page_table, bias_s, lam_vec, subln_w, bsz, seqlen, DEPTH - 1)
        x3, route, route_t, counts = _post_mixer(o, w_o, x2, lng[1, 0], lnb[1, 0], *routers[1], "attn_out_ln_router")
        x4 = _moe(x3, route, route_t, counts, *experts, MOE_EXPERTS, lng[1, 1], lnb[1, 1], "moe1_combine_ln")
        return (x4.reshape(bsz, seqlen, D_MODEL), new_ssm[None], new_conv[None], k_new, v_new)

    ssm0_p = jnp.zeros((bp, SSM_HEADS, SSM_HEAD_DIM, SSM_D_STATE), F32)
    conv0_p = jnp.zeros((bp, SSM_CONV - 1, SSM_CONV_DIM), F32)
    y_p, ssm_p, conv_p, k_p, v_p = trunk(x_prompt, ssm0_p, conv0_p, bp, lp, None)
    y_s, ssm_s, conv_s, k_s, v_s = trunk(x_sample, state_ssm[0], state_conv[0], bs, ls, True)
    return (y_p, y_s, ssm_p, conv_p, k_p, v_p, ssm_s, conv_s, k_s, v_s)
```

```python
import functools
import math

import numpy as np
import jax
import jax.numpy as jnp
from jax import lax
from jax.experimental import pallas as pl
from jax.experimental.pallas import tpu as pltpu

F32 = jnp.float32
BF16 = jnp.bfloat16
HIGHEST = lax.Precision.HIGHEST

D_MODEL = 1024
DEPTH = 2
DN_ALPHA = (2.0 * DEPTH) ** 0.25
LN_EPS = 1e-5
LOG2E = math.log2(math.e)
SSM_D_INNER = 2048
SSM_HEAD_DIM = 64
SSM_HEADS = 32
SSM_GROUPS = 8
SSM_D_STATE = 128
SSM_CONV = 4
SSM_CHUNK = 128
SSM_CONV_DIM = SSM_D_INNER + 2 * SSM_GROUPS * SSM_D_STATE
HEADS_PER_GROUP = SSM_HEADS // SSM_GROUPS
GROUP_WIDTH = HEADS_PER_GROUP * SSM_HEAD_DIM
ATT_HEADS = 8
ATT_HEAD_DIM = 64
ATT_QK_DIM = 1024
ATT_V_DIM = 1024
NUM_BUCKETS = 32
MAX_DISTANCE = 128
MOE_GROUPS = 4
MOE_EXPERTS_PER_GROUP = 8
MOE_EXPERTS = MOE_GROUPS * MOE_EXPERTS_PER_GROUP
MOE_FF = 512
PAGE_SIZE = 128

LANES = 128
SUBLANES = 8
VMEM_LIMIT = 56 * 1024 * 1024

TOKEN_TILE = 256
PROJ_TILE = 512
FFN_TILE = 256
ATT_TILE = 256
MAMBA_SEQS_PER_STEP = 2
SAMPLE_PAGES_PER_STEP = 8


def _cparams(*sem):
    return pltpu.CompilerParams(dimension_semantics=sem, vmem_limit_bytes=VMEM_LIMIT)


def _sigmoid(x):
    return 1.0 / (1.0 + jnp.exp(-x))


def _silu(x):
    return x * _sigmoid(x)


def _softplus(x):
    return jnp.maximum(x, 0.0) + jnp.log(1.0 + jnp.exp(-jnp.abs(x)))


def _layer_norm(u, g, b):
    mu = jnp.mean(u, axis=-1, keepdims=True)
    d = u - mu
    var = jnp.mean(d * d, axis=-1, keepdims=True)
    return d * lax.rsqrt(var + LN_EPS) * g + b


def _dot_nt(a, b):
    return lax.dot_general(a, b, (((1,), (1,)), ((), ())), preferred_element_type=F32)


def _dot_tn(a, b):
    return lax.dot_general(a, b, (((0,), (0,)), ((), ())), preferred_element_type=F32)


def _proj_kernel(x_ref, *refs, n_out, chunk):
    w_refs, o_refs = refs[:n_out], refs[n_out:]
    xb = x_ref[...].astype(BF16)
    for w_ref, o_ref in zip(w_refs, o_refs):
        n = w_ref.shape[1]
        for c0 in range(0, n, chunk):
            c1 = min(n, c0 + chunk)
            o_ref[:, c0:c1] = jnp.dot(xb, w_ref[:, c0:c1], preferred_element_type=F32).astype(o_ref.dtype)


def _proj(x, ws, out_dtypes, name):
    t, k = x.shape
    tm = min(PROJ_TILE, t)
    assert t % tm == 0
    in_specs = [pl.BlockSpec((tm, k), lambda i: (i, 0))]
    in_specs += [pl.BlockSpec(w.shape, lambda i: (0, 0)) for w in ws]
    out_specs = [pl.BlockSpec((tm, w.shape[1]), lambda i: (i, 0)) for w in ws]
    out_shape = [jax.ShapeDtypeStruct((t, w.shape[1]), dt) for w, dt in zip(ws, out_dtypes)]
    return pl.pallas_call(
        functools.partial(_proj_kernel, n_out=len(ws), chunk=512),
        grid=(t // tm,),
        in_specs=in_specs,
        out_specs=out_specs,
        out_shape=out_shape,
        compiler_params=_cparams("parallel"),
        name=name,
    )(x, *ws)


def _mamba_kernel(z_ref, x_ref, b_ref, c_ref, dt_ref, ptab_ref,
                  cwx_ref, cwb_ref, cwc_ref, cbx_ref, cbb_ref, cbc_ref,
                  c0x_ref, c0b_ref, c0c_ref, h0_ref, nw_ref,
                  y_ref, cox_ref, cob_ref, coc_ref, hout_ref,
                  sx, sb, sc, hs, *, q, nc, bb):
    c = pl.program_id(1)
    pad = SUBLANES
    hist = SSM_CONV - 1
    gw, ns, hpg = GROUP_WIDTH, SSM_D_STATE, HEADS_PER_GROUP

    @pl.when(c == 0)
    def _():
        sx[:, pad - hist:pad, :] = c0x_ref[...]
        sb[:, pad - hist:pad, :] = c0b_ref[...]
        sc[:, pad - hist:pad, :] = c0c_ref[...]
        hs[...] = h0_ref[...]

    ti = lax.broadcasted_iota(jnp.int32, (q, q), 0)
    si = lax.broadcasted_iota(jnp.int32, (q, q), 1)
    tri = ti >= si
    tri_f = tri.astype(F32)
    lane = lax.broadcasted_iota(jnp.int32, (1, gw), 1)
    shift = (jnp.concatenate([(ti - si == d).astype(BF16) for d in range(hist, 0, -1)], axis=0)
             if q > SUBLANES else None)
    ptab = ptab_ref[...]

    def expand(v, g):
        r = v.shape[0]
        h0 = hpg * g
        out = jnp.broadcast_to(v[:, h0 + hpg - 1:h0 + hpg], (r, gw))
        for j in range(hpg - 2, -1, -1):
            out = jnp.where(lane < SSM_HEAD_DIM * (j + 1), jnp.broadcast_to(v[:, h0 + j:h0 + j + 1], (r, gw)), out)
        return out

    def conv(cur, s, u, c0, c1, w_ref, bias_ref):
        cur_f = cur.astype(F32)
        s[u, pad:pad + q, c0:c1] = cur_f
        rows = q if shift is None else SUBLANES
        acc = bias_ref[:, c0:c1]
        for k in range(SSM_CONV):
            acc = acc + w_ref[k:k + 1, c0:c1] * s[u, pad - hist + k:pad - hist + k + rows, c0:c1]
        if shift is not None:
            rest = bias_ref[:, c0:c1] + w_ref[hist:hist + 1, c0:c1] * cur_f
            moved = jnp.dot(shift, cur, preferred_element_type=F32)
            for k in range(hist):
                rest = rest + w_ref[k:k + 1, c0:c1] * moved[k * q:(k + 1) * q]
            acc = jnp.concatenate([acc, rest[SUBLANES:]], axis=0)
        tail = s[u, pad + q - hist:pad + q, c0:c1]
        s[u, pad - hist:pad, c0:c1] = tail
        return _silu(acc), tail

    tails = []
    for u in range(bb):
        dt_all = _softplus(dt_ref[u] + ptab[0:1, :])
        da = dt_all * (-jnp.exp(ptab[1:2, :]))
        acum = jnp.dot(tri_f, da, precision=HIGHEST, preferred_element_type=F32)
        acum_t = acum.T
        ac_last = acum[q - 1:q, :]
        cd = jnp.exp(ac_last)
        for g in range(SSM_GROUPS):
            x0, x1 = gw * g, gw * (g + 1)
            n0, n1 = ns * g, ns * (g + 1)
            xc, tail_x = conv(x_ref[u, :, x0:x1], sx, u, x0, x1, cwx_ref, cbx_ref)
            bc, tail_b = conv(b_ref[u, :, n0:n1], sb, u, n0, n1, cwb_ref, cbb_ref)
            cc, tail_c = conv(c_ref[u, :, n0:n1], sc, u, n0, n1, cwc_ref, cbc_ref)

            dt_e = expand(dt_all, g)
            ac_e = expand(acum, g)
            ac_last_e = expand(ac_last, g)

            xdt = xc * dt_e
            xdt_b = xdt.astype(BF16)
            bb_ = bc.astype(BF16)
            cb_ = cc.astype(BF16)
            cbm = _dot_nt(cb_, bb_)
            hprev = hs[u, g]
            y = _dot_nt(cb_, hprev.astype(BF16)) * jnp.exp(ac_e)
            zero_b = jnp.zeros_like(xdt_b)
            for j in range(hpg):
                hd = hpg * g + j
                seg = acum[:, hd:hd + 1] - acum_t[hd:hd + 1, :]
                lm = jnp.exp(jnp.where(tri, seg, -jnp.inf))
                wj = (cbm * lm).astype(BF16)
                xm = jnp.where((lane >= SSM_HEAD_DIM * j) & (lane < SSM_HEAD_DIM * (j + 1)), xdt_b, zero_b)
                y = y + jnp.dot(wj, xm, preferred_element_type=F32)

            xw = (xdt * jnp.exp(ac_last_e - ac_e)).astype(BF16)
            st = _dot_tn(xw, bb_)
            for j in range(hpg):
                hd = hpg * g + j
                r0, r1 = SSM_HEAD_DIM * j, SSM_HEAD_DIM * (j + 1)
                hs[u, g, r0:r1, :] = (jnp.broadcast_to(cd[:, hd:hd + 1], (SSM_HEAD_DIM, ns)) * hprev[r0:r1, :]
                                      + st[r0:r1, :])

            y = y + expand(ptab[2:3, :], g) * xc
            zf = z_ref[u, :, x0:x1].astype(F32)
            y = y * _silu(zf)
            ms = jnp.mean(y * y, axis=-1, keepdims=True)
            y_ref[u, :, x0:x1] = (y * lax.rsqrt(ms + LN_EPS) * nw_ref[:, x0:x1]).astype(y_ref.dtype)
            tails.append((u, x0, x1, n0, n1, tail_x, tail_b, tail_c))

    @pl.when(c == nc - 1)
    def _():
        for u, x0, x1, n0, n1, tail_x, tail_b, tail_c in tails:
            cox_ref[u, :, x0:x1] = tail_x
            cob_ref[u, :, n0:n1] = tail_b
            coc_ref[u, :, n0:n1] = tail_c
        hout_ref[...] = hs[...]


def _mamba_core(zx, dt, conv0, ssm0, ptab, conv_w, conv_b, norm_w, bsz, seqlen):
    q = math.gcd(seqlen, SSM_CHUNK)
    nc = seqlen // q
    g = SSM_GROUPS
    bb = 1 if nc > 1 else math.gcd(bsz, MAMBA_SEQS_PER_STEP)
    di, gn = SSM_D_INNER, g * SSM_D_STATE
    zx3 = zx.reshape(bsz, seqlen, zx.shape[-1])
    dt3 = dt.reshape(bsz, seqlen, LANES)
    h0 = ssm0.reshape(bsz, g, GROUP_WIDTH, SSM_D_STATE)
    hist = SSM_CONV - 1
    in_specs = [
        pl.BlockSpec((bb, q, di), lambda b, c: (b, c, 0)),
        pl.BlockSpec((bb, q, di), lambda b, c: (b, c, 1)),
        pl.BlockSpec((bb, q, gn), lambda b, c: (b, c, 2 * di // gn)),
        pl.BlockSpec((bb, q, gn), lambda b, c: (b, c, 2 * di // gn + 1)),
        pl.BlockSpec((bb, q, LANES), lambda b, c: (b, c, 0)),
        pl.BlockSpec((SUBLANES, LANES), lambda b, c: (0, 0)),
        pl.BlockSpec((SSM_CONV, di), lambda b, c: (0, 0)),
        pl.BlockSpec((SSM_CONV, gn), lambda b, c: (0, di // gn)),
        pl.BlockSpec((SSM_CONV, gn), lambda b, c: (0, di // gn + 1)),
        pl.BlockSpec((1, di), lambda b, c: (0, 0)),
        pl.BlockSpec((1, gn), lambda b, c: (0, di // gn)),
        pl.BlockSpec((1, gn), lambda b, c: (0, di // gn + 1)),
        pl.BlockSpec((bb, hist, di), lambda b, c: (b, 0, 0)),
        pl.BlockSpec((bb, hist, gn), lambda b, c: (b, 0, di // gn)),
        pl.BlockSpec((bb, hist, gn), lambda b, c: (b, 0, di // gn + 1)),
        pl.BlockSpec((bb, g, GROUP_WIDTH, SSM_D_STATE), lambda b, c: (b, 0, 0, 0)),
        pl.BlockSpec((1, di), lambda b, c: (0, 0)),
    ]
    out_specs = [
        pl.BlockSpec((bb, q, di), lambda b, c: (b, c, 0)),
        pl.BlockSpec((bb, hist, di), lambda b, c: (b, 0, 0)),
        pl.BlockSpec((bb, hist, gn), lambda b, c: (b, 0, 0)),
        pl.BlockSpec((bb, hist, gn), lambda b, c: (b, 0, 0)),
        pl.BlockSpec((bb, g, GROUP_WIDTH, SSM_D_STATE), lambda b, c: (b, 0, 0, 0)),
    ]
    out_shape = [
        jax.ShapeDtypeStruct((bsz, seqlen, di), BF16),
        jax.ShapeDtypeStruct((bsz, hist, di), F32),
        jax.ShapeDtypeStruct((bsz, hist, gn), F32),
        jax.ShapeDtypeStruct((bsz, hist, gn), F32),
        jax.ShapeDtypeStruct((bsz, g, GROUP_WIDTH, SSM_D_STATE), F32),
    ]
    scratch = [
        pltpu.VMEM((bb, q + SUBLANES, di), F32),
        pltpu.VMEM((bb, q + SUBLANES, gn), F32),
        pltpu.VMEM((bb, q + SUBLANES, gn), F32),
        pltpu.VMEM((bb, g, GROUP_WIDTH, SSM_D_STATE), F32),
    ]
    y, cox, cob, coc, hout = pl.pallas_call(
        functools.partial(_mamba_kernel, q=q, nc=nc, bb=bb),
        grid=(bsz // bb, nc),
        in_specs=in_specs,
        out_specs=out_specs,
        out_shape=out_shape,
        scratch_shapes=scratch,
        compiler_params=_cparams("parallel", "arbitrary"),
        name="mamba_core",
    )(zx3, zx3, zx3, zx3, dt3, ptab, conv_w, conv_w, conv_w, conv_b, conv_b, conv_b,
      conv0, conv0, conv0, h0, norm_w)
    new_conv = jnp.concatenate([cox, cob, coc], axis=-1)
    return (y.reshape(bsz * seqlen, SSM_D_INNER), new_conv,
            hout.reshape(bsz, SSM_HEADS, SSM_HEAD_DIM, SSM_D_STATE))


def _post_mixer_kernel(y_ref, w_ref, xres_ref, g_ref, b_ref, wr_ref, br_ref,
                       x1_ref, route_ref, route_t_ref, cnt_ref, base, *, tm):
    i = pl.program_id(0)

    @pl.when(i == 0)
    def _():
        base[...] = jnp.zeros_like(base)

    h = jnp.dot(y_ref[...], w_ref[...], preferred_element_type=F32)
    x1 = _layer_norm(DN_ALPHA * xres_ref[...] + h, g_ref[...], b_ref[...])
    x1_ref[...] = x1

    x_hi = x1.astype(BF16)
    x_lo = (x1 - x_hi.astype(F32)).astype(BF16)
    both = jnp.dot(x_hi, wr_ref[...], preferred_element_type=F32)
    logits = (both[:, :LANES] + both[:, LANES:]
              + jnp.dot(x_lo, wr_ref[:, :LANES], preferred_element_type=F32) + br_ref[...])
    lane = lax.broadcasted_iota(jnp.int32, (tm, LANES), 1)
    neg = -jnp.inf
    gmask = lane < MOE_GROUPS
    gl = jnp.where(gmask, logits, neg)
    mg = jnp.max(gl, axis=1, keepdims=True)
    grp = jnp.min(jnp.where(gl == mg, lane, LANES), axis=1, keepdims=True)
    pg = 1.0 / jnp.sum(jnp.where(gmask, jnp.exp(logits - mg), 0.0), axis=1, keepdims=True)
    lo = MOE_GROUPS + MOE_EXPERTS_PER_GROUP * grp
    el = jnp.where((lane >= lo) & (lane < lo + MOE_EXPERTS_PER_GROUP), logits, neg)
    v1 = jnp.max(el, axis=1, keepdims=True)
    i1 = jnp.min(jnp.where(el == v1, lane, LANES), axis=1, keepdims=True)
    el2 = jnp.where(lane == i1, neg, el)
    v2 = jnp.max(el2, axis=1, keepdims=True)
    i2 = jnp.min(jnp.where(el2 == v2, lane, LANES), axis=1, keepdims=True)
    t = jnp.exp(v2 - v1)
    wa = pg / (1.0 + t)
    wb = pg * t / (1.0 + t)
    e1 = i1 - MOE_GROUPS
    e2 = i2 - MOE_GROUPS

    oh = ((lane == e1) | (lane == e2)).astype(BF16)
    ri = lax.broadcasted_iota(jnp.int32, (tm, tm), 0)
    ci = lax.broadcasted_iota(jnp.int32, (tm, tm), 1)
    before = jnp.dot((ri > ci).astype(BF16), oh, preferred_element_type=F32) + base[...]
    r1 = jnp.sum(jnp.where(lane == e1, before, 0.0), axis=1, keepdims=True)
    r2 = jnp.sum(jnp.where(lane == e2, before, 0.0), axis=1, keepdims=True)
    new_base = base[...] + jnp.sum(oh.astype(F32), axis=0, keepdims=True)
    base[...] = new_base
    cnt_ref[...] = new_base

    route = jnp.where(lane == 0, e1.astype(F32), 0.0)
    route = jnp.where(lane == 1, e2.astype(F32), route)
    route = jnp.where(lane == 2, r1, route)
    route = jnp.where(lane == 3, r2, route)
    route = jnp.where(lane == 4, wa, route)
    route = jnp.where(lane == 5, wb, route)
    route_ref[...] = route
    route_t_ref[...] = route.T[0:SUBLANES, :]


def _post_mixer(y, w, xres, ln_g, ln_b, wr, br, name):
    t, kin = y.shape
    tm = min(PROJ_TILE, t)
    assert t % tm == 0
    return pl.pallas_call(
        functools.partial(_post_mixer_kernel, tm=tm),
        grid=(t // tm,),
        in_specs=[
            pl.BlockSpec((tm, kin), lambda i: (i, 0)),
            pl.BlockSpec((kin, D_MODEL), lambda i: (0, 0)),
            pl.BlockSpec((tm, D_MODEL), lambda i: (i, 0)),
            pl.BlockSpec((1, D_MODEL), lambda i: (0, 0)),
            pl.BlockSpec((1, D_MODEL), lambda i: (0, 0)),
            pl.BlockSpec((D_MODEL, 2 * LANES), lambda i: (0, 0)),
            pl.BlockSpec((1, LANES), lambda i: (0, 0)),
        ],
        out_specs=[
            pl.BlockSpec((tm, D_MODEL), lambda i: (i, 0)),
            pl.BlockSpec((tm, LANES), lambda i: (i, 0)),
            pl.BlockSpec((SUBLANES, tm), lambda i: (0, i)),
            pl.BlockSpec((1, LANES), lambda i: (0, 0)),
        ],
        out_shape=[
            jax.ShapeDtypeStruct((t, D_MODEL), F32),
            jax.ShapeDtypeStruct((t, LANES), F32),
            jax.ShapeDtypeStruct((SUBLANES, t), F32),
            jax.ShapeDtypeStruct((1, LANES), F32),
        ],
        scratch_shapes=[pltpu.VMEM((1, LANES), F32)],
        compiler_params=_cparams("arbitrary"),
        name=name,
    )(y, w, xres, ln_g, ln_b, wr, br)


def _row_copy(src_ref, src_row, dst_ref, dst_row, sem):
    return pltpu.make_async_copy(src_ref.at[pl.ds(src_row, 1)], dst_ref.at[pl.ds(dst_row, 1)], sem)


def _dispatch_kernel(dest_ref, *refs, tm, tiles):
    x_refs, xs_ref, sem = refs[:len(tiles)], refs[-2], refs[-1]
    i = pl.program_id(0)
    first = 0
    for x_ref, n_tiles in zip(x_refs, tiles):
        @pl.when((i >= first) & (i < first + n_tiles))
        def _(x_ref=x_ref):
            def start(r, carry):
                _row_copy(x_ref, r, xs_ref, dest_ref[0, 0, r], sem).start()
                _row_copy(x_ref, r, xs_ref, dest_ref[0, 0, tm + r], sem).start(priority=1)
                return carry

            lax.fori_loop(0, tm, start, 0, unroll=True)
            for _ in range(2):
                pltpu.make_async_copy(x_ref, xs_ref.at[pl.ds(0, tm)], sem).wait()

        first += n_tiles


def _dispatch(xs_in, dest3, tm, n_rows):
    tiles = [x.shape[0] // tm for x in xs_in]
    starts = [sum(tiles[:k]) for k in range(len(tiles))]
    in_specs = [pl.BlockSpec((1, 1, 2 * tm), lambda i: (i, 0, 0), memory_space=pltpu.SMEM)]
    for first, n_tiles in zip(starts, tiles):
        in_specs.append(pl.BlockSpec(
            (tm, D_MODEL), lambda i, first=first, n_tiles=n_tiles: (jnp.clip(i - first, 0, n_tiles - 1), 0)))
    return pl.pallas_call(
        functools.partial(_dispatch_kernel, tm=tm, tiles=tuple(tiles)),
        grid=(sum(tiles),),
        in_specs=in_specs,
        out_specs=pl.BlockSpec(memory_space=pl.ANY),
        out_shape=jax.ShapeDtypeStruct((n_rows, D_MODEL), F32),
        scratch_shapes=[pltpu.SemaphoreType.DMA(())],
        compiler_params=_cparams("arbitrary"),
        name="moe_dispatch",
    )(dest3, *xs_in)


def _ffn_kernel(wt_ref, we_ref, wlo_ref, whi_ref, wfirst_ref, wnew_ref,
                xs_ref, w1_ref, w3_ref, w2_ref, ys_ref, w1b, w3b, w2b, *, tf):
    w = pl.program_id(0)
    lo = wlo_ref[w]
    hi = whi_ref[w]

    @pl.when(wfirst_ref[w] == 1)
    def _():
        ys_ref[...] = jnp.zeros_like(ys_ref)

    @pl.when(wnew_ref[w] == 1)
    def _():
        w1b[...] = w1_ref[0].astype(BF16)
        w3b[...] = w3_ref[0].astype(BF16)
        w2b[...] = w2_ref[0].astype(BF16)

    @pl.when(hi > lo)
    def _():
        rows = wt_ref[w] * tf + lax.broadcasted_iota(jnp.int32, (tf, 1), 0)
        mask = (rows >= lo) & (rows < hi)
        xb = xs_ref[...].astype(BF16)
        a = jnp.dot(xb, w1b[...], preferred_element_type=F32)
        b = jnp.dot(xb, w3b[...], preferred_element_type=F32)
        hmid = (_silu(a) * b).astype(BF16)
        y = jnp.dot(hmid, w2b[...], preferred_element_type=F32)
        ys_ref[...] = jnp.where(mask, y, ys_ref[...])


def _ffn(xs, w1, w3, w2, work, tf):
    n = xs.shape[0]
    n_work = work[0].shape[0]
    grid_spec = pltpu.PrefetchScalarGridSpec(
        num_scalar_prefetch=6,
        grid=(n_work,),
        in_specs=[
            pl.BlockSpec((tf, D_MODEL), lambda w, wt, we, wlo, whi, wf, wn: (wt[w], 0)),
            pl.BlockSpec((1, D_MODEL, MOE_FF), lambda w, wt, we, wlo, whi, wf, wn: (we[w], 0, 0)),
            pl.BlockSpec((1, D_MODEL, MOE_FF), lambda w, wt, we, wlo, whi, wf, wn: (we[w], 0, 0)),
            pl.BlockSpec((1, MOE_FF, D_MODEL), lambda w, wt, we, wlo, whi, wf, wn: (we[w], 0, 0)),
        ],
        out_specs=pl.BlockSpec((tf, D_MODEL), lambda w, wt, we, wlo, whi, wf, wn: (wt[w], 0)),
        scratch_shapes=[
            pltpu.VMEM((D_MODEL, MOE_FF), BF16),
            pltpu.VMEM((D_MODEL, MOE_FF), BF16),
            pltpu.VMEM((MOE_FF, D_MODEL), BF16),
        ],
    )
    return pl.pallas_call(
        functools.partial(_ffn_kernel, tf=tf),
        grid_spec=grid_spec,
        out_shape=jax.ShapeDtypeStruct((n, D_MODEL), F32),
        compiler_params=_cparams("arbitrary"),
        name="moe_ffn",
    )(*work, xs, w1, w3, w2)


def _combine_kernel(dcur_ref, dnext_ref, x1_ref, route_ref, g_ref, b_ref, ys_ref, o_ref, buf, sems, *, tm, n_tiles):
    i = pl.program_id(0)
    slot = lax.rem(i, 2)

    def issue(dref, s):
        def start(r, carry):
            _row_copy(ys_ref, dref[0, 0, r], buf.at[s, 0], r, sems.at[s]).start()
            _row_copy(ys_ref, dref[0, 0, tm + r], buf.at[s, 1], r, sems.at[s]).start(priority=1)
            return carry

        lax.fori_loop(0, tm, start, 0, unroll=True)

    @pl.when(i == 0)
    def _():
        issue(dcur_ref, 0)

    @pl.when(i + 1 < n_tiles)
    def _():
        issue(dnext_ref, 1 - slot)

    for k in range(2):
        pltpu.make_async_copy(ys_ref.at[pl.ds(0, tm)], buf.at[slot, k], sems.at[slot]).wait()

    route = route_ref[...]
    y = route[:, 4:5] * buf[slot, 0] + route[:, 5:6] * buf[slot, 1]
    o_ref[...] = _layer_norm(DN_ALPHA * x1_ref[...] + y, g_ref[...], b_ref[...])


def _combine(x1, route, ys, dest3, ln_g, ln_b, tm, name):
    t = x1.shape[0]
    n_tiles = t // tm
    return pl.pallas_call(
        functools.partial(_combine_kernel, tm=tm, n_tiles=n_tiles),
        grid=(n_tiles,),
        in_specs=[
            pl.BlockSpec((1, 1, 2 * tm), lambda i: (i, 0, 0), memory_space=pltpu.SMEM),
            pl.BlockSpec((1, 1, 2 * tm), lambda i: (jnp.minimum(i + 1, n_tiles - 1), 0, 0), memory_space=pltpu.SMEM),
            pl.BlockSpec((tm, D_MODEL), lambda i: (i, 0)),
            pl.BlockSpec((tm, LANES), lambda i: (i, 0)),
            pl.BlockSpec((1, D_MODEL), lambda i: (0, 0)),
            pl.BlockSpec((1, D_MODEL), lambda i: (0, 0)),
            pl.BlockSpec(memory_space=pl.ANY),
        ],
        out_specs=pl.BlockSpec((tm, D_MODEL), lambda i: (i, 0)),
        out_shape=jax.ShapeDtypeStruct((t, D_MODEL), F32),
        scratch_shapes=[
            pltpu.VMEM((2, 2, tm, D_MODEL), F32),
            pltpu.SemaphoreType.DMA((2,)),
        ],
        compiler_params=_cparams("arbitrary"),
        name=name,
    )(dest3, dest3, x1, route, ln_g, ln_b, ys)


def _moe(parts, w1, w3, w2, expert_base, ln_g, ln_b, name):
    tf = FFN_TILE
    lanes_e = jnp.arange(MOE_EXPERTS, dtype=jnp.int32)
    cnts = [p[3][0, :MOE_EXPERTS].astype(jnp.int32) for p in parts]
    cnt = functools.reduce(lambda a, b: a + b, cnts)
    offs = jnp.concatenate([jnp.zeros((1,), jnp.int32), jnp.cumsum(cnt)])
    n = 2 * sum(p[0].shape[0] for p in parts)
    assert n % tf == 0
    n_tiles = n // tf

    dests = []
    seg = offs[:-1]
    for (x1, route, route_t, counts), c in zip(parts, cnts):
        t = x1.shape[0]
        tm = min(TOKEN_TILE, t)
        e = route_t[0:2].astype(jnp.int32)
        r = route_t[2:4].astype(jnp.int32)
        starts = jnp.where(e[None] == lanes_e[:, None, None], seg[:, None, None], 0)
        dest = jnp.sum(starts, axis=0) + r
        dests.append(jnp.transpose(dest.reshape(2, t // tm, tm), (1, 0, 2)).reshape(t // tm, 1, 2 * tm))
        seg = seg + c

    n_work = n_tiles + MOE_EXPERTS
    first_tile = offs[:-1] // tf
    last_tile = (offs[1:] - 1) // tf
    ntile_e = jnp.where(cnt > 0, last_tile - first_tile + 1, 0)
    wstart = jnp.concatenate([jnp.zeros((1,), jnp.int32), jnp.cumsum(ntile_e)])
    total = wstart[-1]
    widx = jnp.arange(n_work, dtype=jnp.int32)
    we = jnp.sum((wstart[None, 1:] <= widx[:, None]).astype(jnp.int32), axis=1)
    we = jnp.minimum(we, MOE_EXPERTS - 1)
    onehot = we[:, None] == lanes_e[None, :]

    def pick(v):
        return jnp.sum(jnp.where(onehot, v[None, :], 0), axis=1)

    valid = widx < total
    wt = jnp.where(valid, pick(first_tile) + widx - pick(wstart[:-1]), n_tiles - 1).astype(jnp.int32)
    wlo = jnp.where(valid, pick(offs[:-1]), 0).astype(jnp.int32)
    whi = jnp.where(valid, pick(offs[1:]), 0).astype(jnp.int32)
    last_e = jnp.max(jnp.where(valid, we, 0))
    we = jnp.where(valid, we, last_e).astype(jnp.int32)
    wfirst = jnp.concatenate([jnp.ones((1,), jnp.int32), (wt[1:] != wt[:-1]).astype(jnp.int32)])
    wnew = jnp.concatenate([jnp.ones((1,), jnp.int32), (we[1:] != we[:-1]).astype(jnp.int32)])

    tm = min(TOKEN_TILE, min(p[0].shape[0] for p in parts))
    assert all(p[0].shape[0] % tm == 0 and min(TOKEN_TILE, p[0].shape[0]) == tm for p in parts)
    xs = _dispatch([p[0] for p in parts], jnp.concatenate(dests, axis=0), tm, n)
    ys = _ffn(xs, w1, w3, w2, (wt, we + expert_base, wlo, whi, wfirst, wnew), tf)
    return [_combine(x1, route, ys, dest3, ln_g, ln_b, min(TOKEN_TILE, x1.shape[0]), name)
            for (x1, route, route_t, counts), dest3 in zip(parts, dests)]


def _bias_kernel(idx_ref, rbt_ref, o_ref, *, width):
    idx = idx_ref[...]
    bucket = lax.broadcasted_iota(jnp.int32, (NUM_BUCKETS, width), 0)
    onehot = (bucket == idx).astype(F32)
    vals = jnp.dot(rbt_ref[...], onehot, precision=HIGHEST, preferred_element_type=F32)
    o_ref[...] = jnp.where(idx < 0, -jnp.inf, vals)


def _bias_tiles(rel_bias, idx_np):
    n = idx_np.shape[0]
    width = 8192
    assert n % width == 0
    nh = rel_bias.shape[1]
    return pl.pallas_call(
        functools.partial(_bias_kernel, width=width),
        grid=(n // width,),
        in_specs=[
            pl.BlockSpec((1, width), lambda i: (0, i)),
            pl.BlockSpec((nh, NUM_BUCKETS), lambda i: (0, 0)),
        ],
        out_specs=pl.BlockSpec((nh, width), lambda i: (0, i)),
        out_shape=jax.ShapeDtypeStruct((nh, n), F32),
        compiler_params=_cparams("parallel"),
        name="rel_bias_tiles",
    )(jnp.asarray(idx_np.reshape(1, n)), rel_bias.T)


def _bucket_of_distance(dist):
    n = np.maximum(dist, 0)
    max_exact = NUM_BUCKETS // 2
    nf = np.maximum(n, 1).astype(np.float32)
    large = max_exact + (np.log(nf / np.float32(max_exact)) / np.float32(math.log(MAX_DISTANCE / max_exact))
                         * np.float32(NUM_BUCKETS - max_exact)).astype(np.int32)
    large = np.minimum(large, NUM_BUCKETS - 1)
    return np.where(n < max_exact, n, large).astype(np.int32)


def _bucket_tile(dist):
    return np.where(dist >= 0, _bucket_of_distance(dist), -1).astype(np.int32)


def _kvq_prompt_kernel(x_ref, wkt_ref, wv_ref, wq_ref, kt_ref, ktb_ref, v_ref, vb_ref, q_ref, *, chunk):
    xb = x_ref[...].astype(BF16)
    hd2 = 2 * ATT_HEAD_DIM
    for c0 in range(0, ATT_QK_DIM, chunk):
        kt = _dot_nt(wkt_ref[c0:c0 + chunk, :], xb)
        kt_ref[0, c0:c0 + chunk, :] = kt
        ktb_ref[0, 0, c0:c0 + chunk, :] = kt.astype(BF16)
    for c0 in range(0, ATT_V_DIM, chunk):
        v = jnp.dot(xb, wv_ref[:, c0:c0 + chunk], preferred_element_type=F32)
        vb_ref[:, c0:c0 + chunk] = v.astype(BF16)
        for h0 in range(0, chunk, hd2):
            v_ref[:, (c0 + h0) // hd2, :] = v[:, h0:h0 + hd2]
    for c0 in range(0, ATT_QK_DIM, chunk):
        q_ref[:, c0:c0 + chunk] = jnp.dot(xb, wq_ref[:, c0:c0 + chunk], preferred_element_type=F32).astype(BF16)


def _kvq_prompt(x, wkt, wv, wq, bsz, seqlen):
    t = bsz * seqlen
    tm = min(ATT_TILE, seqlen)
    nkb = seqlen // tm
    hd2 = 2 * ATT_HEAD_DIM
    full = lambda i: (0, 0)
    return pl.pallas_call(
        functools.partial(_kvq_prompt_kernel, chunk=512),
        grid=(t // tm,),
        in_specs=[
            pl.BlockSpec((tm, D_MODEL), lambda i: (i, 0)),
            pl.BlockSpec(wkt.shape, full),
            pl.BlockSpec(wv.shape, full),
            pl.BlockSpec(wq.shape, full),
        ],
        out_specs=[
            pl.BlockSpec((1, ATT_QK_DIM, tm), lambda i: (i // nkb, 0, i % nkb)),
            pl.BlockSpec((1, 1, ATT_QK_DIM, tm), lambda i: (i // nkb, i % nkb, 0, 0)),
            pl.BlockSpec((tm, ATT_HEADS, hd2), lambda i: (i, 0, 0)),
            pl.BlockSpec((tm, ATT_V_DIM), lambda i: (i, 0)),
            pl.BlockSpec((tm, ATT_QK_DIM), lambda i: (i, 0)),
        ],
        out_shape=[
            jax.ShapeDtypeStruct((bsz, ATT_QK_DIM, seqlen), F32),
            jax.ShapeDtypeStruct((bsz, nkb, ATT_QK_DIM, tm), BF16),
            jax.ShapeDtypeStruct((t, ATT_HEADS, hd2), F32),
            jax.ShapeDtypeStruct((t, ATT_V_DIM), BF16),
            jax.ShapeDtypeStruct((t, ATT_QK_DIM), BF16),
        ],
        compiler_params=_cparams("parallel"),
        name="kvq_proj_prompt",
    )(x, wkt, wv, wq)


def _lambda_value(lam_ref, layer_idx):
    lv = lam_ref[...]
    s1 = jnp.sum(lv[0:1, :] * lv[1:2, :], axis=1, keepdims=True)
    s2 = jnp.sum(lv[2:3, :] * lv[3:4, :], axis=1, keepdims=True)
    lam_init = 0.8 - 0.6 * math.exp(-0.3 * layer_idx)
    return jnp.exp(s1) - jnp.exp(s2) + lam_init, lam_init


def _attn_prompt_kernel(q_ref, kt_ref, v_ref, bias_ref, lam_ref, sw_ref, o_ref, *, tq, nq, layer_idx):
    lane = lax.broadcasted_iota(jnp.int32, (tq, 2 * ATT_HEAD_DIM), 1)
    lam, lam_init = _lambda_value(lam_ref, layer_idx)
    for qi in range(nq):
        qb = q_ref[0, qi * tq:(qi + 1) * tq, :]
        zero = jnp.zeros_like(qb)
        q2 = jnp.concatenate([jnp.where(lane < ATT_HEAD_DIM, qb, zero),
                              jnp.where(lane >= ATT_HEAD_DIM, qb, zero)], axis=0)
        m = l = acc = None
        for j in range(qi + 1):
            s = jnp.dot(q2, kt_ref[0, j], preferred_element_type=F32)
            if qi - j < 2:
                s = s + bias_ref[0, qi - j]
            vj = v_ref[0, j * tq:(j + 1) * tq, :]
            bm = jnp.max(s, axis=1, keepdims=True)
            if j == 0:
                m = bm
                p = jnp.exp2(s - m)
                l = jnp.sum(p, axis=1, keepdims=True)
                acc = jnp.dot(p.astype(BF16), vj, preferred_element_type=F32)
            else:
                m_new = jnp.maximum(m, bm)
                alpha = jnp.exp2(m - m_new)
                p = jnp.exp2(s - m_new)
                l = alpha * l + jnp.sum(p, axis=1, keepdims=True)
                acc = alpha * acc + jnp.dot(p.astype(BF16), vj, preferred_element_type=F32)
                m = m_new
        o = acc / l
        d = o[:tq] - lam * o[tq:]
        ms = jnp.mean(d * d, axis=-1, keepdims=True)
        o_ref[0, qi * tq:(qi + 1) * tq, :] = (
            d * lax.rsqrt(ms + LN_EPS) * sw_ref[...] * (1.0 - lam_init)).astype(o_ref.dtype)


def _attn_prompt(q, ktb, vb, bias, lam_vec, subln_w, bsz, seqlen, layer_idx):
    tq = min(ATT_TILE, seqlen)
    nq = seqlen // tq
    hd2 = 2 * ATT_HEAD_DIM
    q3 = q.reshape(bsz, seqlen, ATT_QK_DIM)
    v3 = vb.reshape(bsz, seqlen, ATT_V_DIM)
    out = pl.pallas_call(
        functools.partial(_attn_prompt_kernel, tq=tq, nq=nq, layer_idx=layer_idx),
        grid=(bsz, ATT_HEADS),
        in_specs=[
            pl.BlockSpec((1, seqlen, hd2), lambda b, h: (b, 0, h)),
            pl.BlockSpec((1, nq, hd2, tq), lambda b, h: (b, 0, h, 0)),
            pl.BlockSpec((1, seqlen, hd2), lambda b, h: (b, 0, h)),
            pl.BlockSpec((1, 2, 2 * tq, tq), lambda b, h: (h, 0, 0, 0)),
            pl.BlockSpec(lam_vec.shape, lambda b, h: (0, 0)),
            pl.BlockSpec((1, hd2), lambda b, h: (0, 0)),
        ],
        out_specs=pl.BlockSpec((1, seqlen, hd2), lambda b, h: (b, 0, h)),
        out_shape=jax.ShapeDtypeStruct((bsz, seqlen, ATT_V_DIM), BF16),
        compiler_params=_cparams("parallel", "parallel"),
        name="attn_prompt",
    )(q3, ktb, v3, bias, lam_vec, subln_w)
    return out.reshape(bsz * seqlen, ATT_V_DIM)


def _attn_sample_kernel(pt_ref, q_ref, *refs, nq, n_steps, pp, layer_idx):
    kt_refs = refs[:pp]
    vm_refs = refs[pp:2 * pp]
    (kn_ref, vn_ref, bias_ref, exp_ref, pmask_ref, lam_ref, sw_ref,
     o_ref, qbd, m_s, l_s, acc_s, kpad, vpad) = refs[2 * pp:]
    j = pl.program_id(1)
    maps = 2 * ATT_HEADS
    rows = maps * nq
    hd2 = 2 * ATT_HEAD_DIM
    page = PAGE_SIZE

    @pl.when(j == 0)
    def _():
        m_s[...] = jnp.full_like(m_s, -jnp.inf)
        l_s[...] = jnp.zeros_like(l_s)
        acc_s[...] = jnp.zeros_like(acc_s)
        qt = jnp.concatenate([q_ref[0].astype(F32)] * maps, axis=0)
        ri = lax.broadcasted_iota(jnp.int32, (rows, ATT_QK_DIM), 0)
        ci = lax.broadcasted_iota(jnp.int32, (rows, ATT_QK_DIM), 1)
        keep = (ri // nq) == (ci // ATT_HEAD_DIM)
        qbd[...] = jnp.where(keep, qt, 0.0).astype(qbd.dtype)

    def softmax_update(s):
        m = m_s[...]
        m_new = jnp.maximum(m, jnp.max(s, axis=1, keepdims=True))
        alpha = jnp.exp2(m - m_new)
        p = jnp.exp2(s - m_new)
        l_s[...] = alpha * l_s[...] + jnp.sum(p, axis=1, keepdims=True)
        m_s[...] = m_new
        return alpha, p.astype(BF16)

    def past_pages():
        kt_all = jnp.concatenate(
            [kt_refs[u][0].reshape(ATT_QK_DIM, page).astype(BF16) for u in range(pp)], axis=1)
        s = jnp.dot(qbd[...], kt_all, preferred_element_type=F32)
        last = bias_ref[jnp.where(j == n_steps - 1, 0, 2)]
        s = jnp.concatenate([s[:, :page * (pp - 1)], s[:, page * (pp - 1):] + last], axis=1)
        alpha, pb = softmax_update(s)
        p_rows = jnp.concatenate([pb[:, page * u:page * (u + 1)] for u in range(pp)], axis=0)
        pe = jnp.dot(p_rows, exp_ref[...], preferred_element_type=F32)
        pe_cat = jnp.concatenate(
            [pe[rows * u:rows * (u + 1)].astype(BF16) * pmask_ref[...] for u in range(pp)], axis=1)
        vm_all = jnp.concatenate(
            [vm_refs[u][0].reshape(page * ATT_HEADS, hd2).astype(BF16) for u in range(pp)], axis=0)
        acc_s[...] = alpha * acc_s[...] + jnp.dot(pe_cat, vm_all, preferred_element_type=F32)

    past_pages()

    @pl.when(j == n_steps - 1)
    def _():
        kpad[...] = jnp.zeros_like(kpad)
        vpad[...] = jnp.zeros_like(vpad)
        kpad[0:nq, :] = kn_ref[0]
        vpad[0:nq, :] = vn_ref[0]
        s = _dot_nt(qbd[...], kpad[...].astype(BF16)) + bias_ref[1]
        alpha, pb = softmax_update(s)
        for h in range(ATT_HEADS):
            r0, r1 = 2 * nq * h, 2 * nq * (h + 1)
            pvh = jnp.dot(pb[r0:r1, :], vpad[:, hd2 * h:hd2 * (h + 1)].astype(BF16), preferred_element_type=F32)
            acc_s[r0:r1, :] = alpha[r0:r1, :] * acc_s[r0:r1, :] + pvh
        o = acc_s[...] / l_s[...]
        lam, lam_init = _lambda_value(lam_ref, layer_idx)
        for h in range(ATT_HEADS):
            r0 = 2 * nq * h
            d = o[r0:r0 + nq, :] - lam * o[r0 + nq:r0 + 2 * nq, :]
            ms = jnp.mean(d * d, axis=-1, keepdims=True)
            o_ref[0, :, hd2 * h:hd2 * (h + 1)] = (
                d * lax.rsqrt(ms + LN_EPS) * sw_ref[...] * (1.0 - lam_init)).astype(o_ref.dtype)


def _attn_sample(q, kv, cache_k, cache_v, page_table, bias, lam_vec, subln_w, bsz, nq, layer_idx):
    n_pages = page_table.shape[1]
    page = cache_k.shape[1]
    maps = 2 * ATT_HEADS
    rows = maps * nq
    hd2 = 2 * ATT_HEAD_DIM
    pp = math.gcd(n_pages, SAMPLE_PAGES_PER_STEP)
    n_steps = n_pages // pp
    q3 = q.reshape(bsz, nq, ATT_QK_DIM)
    kv3 = kv.reshape(bsz, nq, ATT_QK_DIM + ATT_V_DIM)
    cache_kt = jnp.transpose(cache_k, (0, 2, 3, 1))
    kk = np.arange(page)[:, None]
    cc = np.arange(page * ATT_HEADS)[None, :]
    expand = jnp.asarray((cc // ATT_HEADS == kk).astype(np.float32), dtype=BF16)
    rr = np.arange(rows)[:, None]
    pmask = jnp.asarray(((rr // (2 * nq)) == (cc % ATT_HEADS)).astype(np.float32), dtype=BF16)

    def page_map(u):
        return lambda b, j, pt: (pt[b, j * pp + u], 0, 0, 0)

    in_specs = [pl.BlockSpec((1, nq, ATT_QK_DIM), lambda b, j, pt: (b, 0, 0))]
    in_specs += [pl.BlockSpec((1, maps, ATT_HEAD_DIM, page), page_map(u)) for u in range(pp)]
    in_specs += [pl.BlockSpec((1, page, ATT_HEADS, hd2), page_map(u)) for u in range(pp)]
    in_specs += [
        pl.BlockSpec((1, nq, ATT_QK_DIM), lambda b, j, pt: (b, 0, 0)),
        pl.BlockSpec((1, nq, ATT_V_DIM), lambda b, j, pt: (b, 0, 1)),
        pl.BlockSpec((3, rows, page), lambda b, j, pt: (0, 0, 0)),
        pl.BlockSpec((page, page * ATT_HEADS), lambda b, j, pt: (0, 0)),
        pl.BlockSpec((rows, page * ATT_HEADS), lambda b, j, pt: (0, 0)),
        pl.BlockSpec(lam_vec.shape, lambda b, j, pt: (0, 0)),
        pl.BlockSpec((1, hd2), lambda b, j, pt: (0, 0)),
    ]
    grid_spec = pltpu.PrefetchScalarGridSpec(
        num_scalar_prefetch=1,
        grid=(bsz, n_steps),
        in_specs=in_specs,
        out_specs=pl.BlockSpec((1, nq, ATT_V_DIM), lambda b, j, pt: (b, 0, 0)),
        scratch_shapes=[
            pltpu.VMEM((rows, ATT_QK_DIM), BF16),
            pltpu.VMEM((rows, 1), F32),
            pltpu.VMEM((rows, 1), F32),
            pltpu.VMEM((rows, hd2), F32),
            pltpu.VMEM((page, ATT_QK_DIM), F32),
            pltpu.VMEM((page, ATT_V_DIM), F32),
        ],
    )
    out = pl.pallas_call(
        functools.partial(_attn_sample_kernel, nq=nq, n_steps=n_steps, pp=pp, layer_idx=layer_idx),
        grid_spec=grid_spec,
        out_shape=jax.ShapeDtypeStruct((bsz, nq, ATT_V_DIM), BF16),
        compiler_params=_cparams("parallel", "arbitrary"),
        name="attn_sample",
    )(page_table, q3, *([cache_kt] * pp), *([cache_v] * pp), kv3, kv3, bias, expand, pmask, lam_vec, subln_w)
    return out.reshape(bsz * nq, ATT_V_DIM)


def _router_params(wg, bg, we, be):
    wexp = jnp.transpose(we, (1, 0, 2)).reshape(D_MODEL, MOE_EXPERTS)
    wr = jnp.concatenate([wg, wexp], axis=1)
    wr = jnp.pad(wr, ((0, 0), (0, LANES - wr.shape[1])))
    w_hi = wr.astype(BF16)
    w_lo = (wr - w_hi.astype(F32)).astype(BF16)
    wr = jnp.concatenate([w_hi, w_lo], axis=1)
    br = jnp.concatenate([bg, be.reshape(MOE_EXPERTS)])
    br = jnp.pad(br, (0, LANES - br.shape[0])).reshape(1, LANES)
    return wr, br


def _head_tables(dt_bias, a_log, d_skip):
    rows = jnp.stack([dt_bias, a_log, d_skip])
    return jnp.pad(rows, ((0, SUBLANES - 3), (0, LANES - SSM_HEADS)))


def _prompt_bias_index(tq):
    r = np.arange(tq)[:, None]
    c = np.arange(tq)[None, :]
    tiles = [_bucket_tile(delta * tq + r - c) for delta in range(3)]
    return np.stack(tiles).reshape(-1)


def _sample_bias_index(nq, page, past_len):
    r = np.arange(nq)[:, None]
    c = np.arange(page)[None, :]
    last_page = _bucket_tile(past_len + r - (past_len - page) - c)
    own = np.where(c < nq, _bucket_tile(r - c), -1).astype(np.int32)
    far = _bucket_tile(np.full((nq, page), MAX_DISTANCE + page))
    return np.stack([last_page, own, far]).reshape(-1)


def kernel(x_prompt, x_sample, state_ssm, state_conv, cache_k, cache_v, page_table, ln_g, ln_b, m_w_in,
           m_conv_w, m_conv_b, m_dt_bias, m_a_log, m_d, m_norm_w, m_w_out, kv_w, a_w_q, a_lambda,
           a_subln_w, a_w_o, rel_bias, moe_wg, moe_bg, moe_we, moe_be, moe_w1, moe_w3, moe_w2):
    bp, lp, _ = x_prompt.shape
    bs, ls, _ = x_sample.shape
    n_pages = page_table.shape[1]
    page = cache_k.shape[1]
    past_len = n_pages * page
    assert page == PAGE_SIZE and page >= MAX_DISTANCE and min(ATT_TILE, lp) >= MAX_DISTANCE

    w_in = m_w_in[0]
    w_zx = w_in[:, :SSM_D_INNER + SSM_CONV_DIM].astype(BF16)
    w_dt = jnp.pad(w_in[:, SSM_D_INNER + SSM_CONV_DIM:], ((0, 0), (0, LANES - SSM_HEADS))).astype(BF16)
    ptab = _head_tables(m_dt_bias[0], m_a_log[0], m_d[0])
    conv_w = m_conv_w[0]
    conv_b = m_conv_b[0].reshape(1, SSM_CONV_DIM)
    norm_w = m_norm_w[0].reshape(1, SSM_D_INNER)
    w_out = m_w_out[0].astype(BF16)
    w_q = (a_w_q[0] * (LOG2E * ATT_HEAD_DIM ** -0.5)).astype(BF16)
    w_kv = kv_w.astype(BF16)
    w_kt = kv_w[:, :ATT_QK_DIM].T.astype(BF16)
    w_v = w_kv[:, ATT_QK_DIM:]
    w_o = a_w_o[0].astype(BF16)
    lam_vec = a_lambda[0]
    subln_w = a_subln_w[0].reshape(1, 2 * ATT_HEAD_DIM)
    routers = [_router_params(moe_wg[l], moe_bg[l], moe_we[l], moe_be[l]) for l in range(DEPTH)]
    experts = (moe_w1.reshape(DEPTH * MOE_EXPERTS, D_MODEL, MOE_FF),
               moe_w3.reshape(DEPTH * MOE_EXPERTS, D_MODEL, MOE_FF),
               moe_w2.reshape(DEPTH * MOE_EXPERTS, MOE_FF, D_MODEL))
    lng = ln_g.reshape(DEPTH, 2, 1, D_MODEL)
    lnb = ln_b.reshape(DEPTH, 2, 1, D_MODEL)

    tq = min(ATT_TILE, lp)
    maps = 2 * ATT_HEADS
    idx_p = _prompt_bias_index(tq)
    idx_s = _sample_bias_index(ls, page, past_len)
    n_p = idx_p.shape[0]
    n_s = idx_s.shape[0]
    n_tot = -(-(n_p + n_s) // 8192) * 8192
    idx_all = np.concatenate([idx_p, idx_s, np.zeros((n_tot - n_p - n_s,), np.int32)])
    tiles = _bias_tiles(rel_bias, idx_all)
    bias_p = tiles[:, :n_p].reshape(ATT_HEADS, 2, 3, tq, tq)
    bias_p = (bias_p[:, :, 0:2] - bias_p[:, :, 2:3]) * LOG2E
    bias_p = jnp.transpose(bias_p, (0, 2, 1, 3, 4)).reshape(ATT_HEADS, 2, 2 * tq, tq)
    bias_s = tiles[:, n_p:n_p + n_s].reshape(maps, 3, ls, page)
    bias_s = (bias_s - bias_s[:, 2:3]) * LOG2E
    bias_s = jnp.transpose(bias_s, (1, 0, 2, 3)).reshape(3, maps * ls, page)

    def mixer0(x, ssm0, conv0, bsz, seqlen):
        xf = x.reshape(bsz * seqlen, D_MODEL)
        zx, dt = _proj(xf, [w_zx, w_dt], [BF16, F32], "in_proj")
        y, new_conv, new_ssm = _mamba_core(zx, dt, conv0, ssm0, ptab, conv_w, conv_b, norm_w, bsz, seqlen)
        part = _post_mixer(y, w_out, xf, lng[0, 0], lnb[0, 0], *routers[0], "mamba_out_ln_router")
        return part, new_ssm[None], new_conv[None]

    def mixer1(x2, bsz, seqlen, past):
        if past is None:
            kt, ktb, v4, vb, qs = _kvq_prompt(x2, w_kt, w_v, w_q, bsz, seqlen)
            k_new = jnp.transpose(kt.reshape(bsz, maps, ATT_HEAD_DIM, seqlen), (0, 3, 1, 2))
            v_new = v4.reshape(bsz, seqlen, ATT_HEADS, 2 * ATT_HEAD_DIM)
            o = _attn_prompt(qs, ktb, vb, bias_p, lam_vec, subln_w, bsz, seqlen, DEPTH - 1)
        else:
            kv, qs = _proj(x2, [w_kv, w_q], [F32, BF16], "kvq_proj")
            k_new = kv[:, :ATT_QK_DIM].reshape(bsz, seqlen, maps, ATT_HEAD_DIM)
            v_new = kv[:, ATT_QK_DIM:].reshape(bsz, seqlen, ATT_HEADS, 2 * ATT_HEAD_DIM)
            o = _attn_sample(qs, kv, cache_k, cache_v, page_table, bias_s, lam_vec, subln_w, bsz, seqlen, DEPTH - 1)
        part = _post_mixer(o, w_o, x2, lng[1, 0], lnb[1, 0], *routers[1], "attn_out_ln_router")
        return part, k_new, v_new

    ssm0_p = jnp.zeros((bp, SSM_HEADS, SSM_HEAD_DIM, SSM_D_STATE), F32)
    conv0_p = jnp.zeros((bp, SSM_CONV - 1, SSM_CONV_DIM), F32)
    part_p, ssm_p, conv_p = mixer0(x_prompt, ssm0_p, conv0_p, bp, lp)
    part_s, ssm_s, conv_s = mixer0(x_sample, state_ssm[0], state_conv[0], bs, ls)
    x2_p, x2_s = _moe([part_p, part_s], *experts, 0, lng[0, 1], lnb[0, 1], "moe0_combine_ln")
    part_p, k_p, v_p = mixer1(x2_p, bp, lp, None)
    part_s, k_s, v_s = mixer1(x2_s, bs, ls, True)
    x4_p, x4_s = _moe([part_p, part_s], *experts, MOE_EXPERTS, lng[1, 1], lnb[1, 1], "moe1_combine_ln")
    y_p = x4_p.reshape(bp, lp, D_MODEL)
    y_s = x4_s.reshape(bs, ls, D_MODEL)
    return (y_p, y_s, ssm_p, conv_p, k_p, v_p, ssm_s, conv_s, k_s, v_s)
```

```python
import functools
import math

import numpy as np
import jax
import jax.numpy as jnp
from jax import lax
from jax.experimental import pallas as pl
from jax.experimental.pallas import tpu as pltpu

F32 = jnp.float32
BF16 = jnp.bfloat16
HIGHEST = lax.Precision.HIGHEST

D_MODEL = 1024
DEPTH = 2
DN_ALPHA = (2.0 * DEPTH) ** 0.25
LN_EPS = 1e-5
LOG2E = math.log2(math.e)
SSM_D_INNER = 2048
SSM_HEAD_DIM = 64
SSM_HEADS = 32
SSM_GROUPS = 8
SSM_D_STATE = 128
SSM_CONV = 4
SSM_CHUNK = 128
SSM_CONV_DIM = SSM_D_INNER + 2 * SSM_GROUPS * SSM_D_STATE
HEADS_PER_GROUP = SSM_HEADS // SSM_GROUPS
GROUP_WIDTH = HEADS_PER_GROUP * SSM_HEAD_DIM
ATT_HEADS = 8
ATT_HEAD_DIM = 64
ATT_QK_DIM = 1024
ATT_V_DIM = 1024
NUM_BUCKETS = 32
MAX_DISTANCE = 128
MOE_GROUPS = 4
MOE_EXPERTS_PER_GROUP = 8
MOE_EXPERTS = MOE_GROUPS * MOE_EXPERTS_PER_GROUP
MOE_FF = 512
PAGE_SIZE = 128

LANES = 128
SUBLANES = 8
VMEM_LIMIT = 56 * 1024 * 1024

TOKEN_TILE = 256
PROJ_TILE = 512
FFN_TILE = 256
ATT_TILE = 256
MAMBA_SEQS_PER_STEP = 4
SAMPLE_PAGES_PER_STEP = 8


def _cparams(*sem):
    return pltpu.CompilerParams(dimension_semantics=sem, vmem_limit_bytes=VMEM_LIMIT)


def _sigmoid(x):
    return 1.0 / (1.0 + jnp.exp(-x))


def _silu(x):
    return x * _sigmoid(x)


def _softplus(x):
    return jnp.maximum(x, 0.0) + jnp.log(1.0 + jnp.exp(-jnp.abs(x)))


def _layer_norm(u, g, b):
    mu = jnp.mean(u, axis=-1, keepdims=True)
    d = u - mu
    var = jnp.mean(d * d, axis=-1, keepdims=True)
    return d * lax.rsqrt(var + LN_EPS) * g + b


def _dot_nt(a, b):
    return lax.dot_general(a, b, (((1,), (1,)), ((), ())), preferred_element_type=F32)


def _dot_tn(a, b):
    return lax.dot_general(a, b, (((0,), (0,)), ((), ())), preferred_element_type=F32)


def _proj_kernel(x_ref, *refs, n_out, chunk):
    w_refs, o_refs = refs[:n_out], refs[n_out:]
    xb = x_ref[...].astype(BF16)
    for w_ref, o_ref in zip(w_refs, o_refs):
        n = w_ref.shape[1]
        for c0 in range(0, n, chunk):
            c1 = min(n, c0 + chunk)
            o_ref[:, c0:c1] = jnp.dot(xb, w_ref[:, c0:c1], preferred_element_type=F32).astype(o_ref.dtype)


def _proj(x, ws, out_dtypes, name):
    t, k = x.shape
    tm = min(PROJ_TILE, t)
    assert t % tm == 0
    in_specs = [pl.BlockSpec((tm, k), lambda i: (i, 0))]
    in_specs += [pl.BlockSpec(w.shape, lambda i: (0, 0)) for w in ws]
    out_specs = [pl.BlockSpec((tm, w.shape[1]), lambda i: (i, 0)) for w in ws]
    out_shape = [jax.ShapeDtypeStruct((t, w.shape[1]), dt) for w, dt in zip(ws, out_dtypes)]
    return pl.pallas_call(
        functools.partial(_proj_kernel, n_out=len(ws), chunk=512),
        grid=(t // tm,),
        in_specs=in_specs,
        out_specs=out_specs,
        out_shape=out_shape,
        compiler_params=_cparams("parallel"),
        name=name,
    )(x, *ws)


def _mamba_kernel(z_ref, x_ref, b_ref, c_ref, dt_ref, ptab_ref,
                  cwx_ref, cwb_ref, cwc_ref, cbx_ref, cbb_ref, cbc_ref,
                  c0x_ref, c0b_ref, c0c_ref, h0_ref, nw_ref,
                  y_ref, cox_ref, cob_ref, coc_ref, hout_ref,
                  sx, sb, sc, hs, *, q, nc, bb):
    c = pl.program_id(1)
    pad = SUBLANES
    hist = SSM_CONV - 1
    gw, ns, hpg = GROUP_WIDTH, SSM_D_STATE, HEADS_PER_GROUP

    @pl.when(c == 0)
    def _():
        sx[:, pad - hist:pad, :] = c0x_ref[...]
        sb[:, pad - hist:pad, :] = c0b_ref[...]
        sc[:, pad - hist:pad, :] = c0c_ref[...]
        hs[...] = h0_ref[...]

    ti = lax.broadcasted_iota(jnp.int32, (q, q), 0)
    si = lax.broadcasted_iota(jnp.int32, (q, q), 1)
    tri = ti >= si
    tri_f = tri.astype(F32)
    lane = lax.broadcasted_iota(jnp.int32, (1, gw), 1)
    shift = (jnp.concatenate([(ti - si == d).astype(BF16) for d in range(hist, 0, -1)], axis=0)
             if q > SUBLANES else None)
    ptab = ptab_ref[...]

    def expand(v, g):
        r = v.shape[0]
        h0 = hpg * g
        out = jnp.broadcast_to(v[:, h0 + hpg - 1:h0 + hpg], (r, gw))
        for j in range(hpg - 2, -1, -1):
            out = jnp.where(lane < SSM_HEAD_DIM * (j + 1), jnp.broadcast_to(v[:, h0 + j:h0 + j + 1], (r, gw)), out)
        return out

    def conv(cur, s, u, c0, c1, w_ref, bias_ref):
        cur_f = cur.astype(F32)
        s[u, pad:pad + q, c0:c1] = cur_f
        rows = q if shift is None else SUBLANES
        acc = bias_ref[:, c0:c1]
        for k in range(SSM_CONV):
            acc = acc + w_ref[k:k + 1, c0:c1] * s[u, pad - hist + k:pad - hist + k + rows, c0:c1]
        if shift is not None:
            rest = bias_ref[:, c0:c1] + w_ref[hist:hist + 1, c0:c1] * cur_f
            moved = jnp.dot(shift, cur, preferred_element_type=F32)
            for k in range(hist):
                rest = rest + w_ref[k:k + 1, c0:c1] * moved[k * q:(k + 1) * q]
            acc = jnp.concatenate([acc, rest[SUBLANES:]], axis=0)
        tail = s[u, pad + q - hist:pad + q, c0:c1]
        s[u, pad - hist:pad, c0:c1] = tail
        return _silu(acc), tail

    tails = []
    for u in range(bb):
        dt_all = _softplus(dt_ref[u] + ptab[0:1, :])
        da = dt_all * (-jnp.exp(ptab[1:2, :]))
        acum = jnp.dot(tri_f, da, precision=HIGHEST, preferred_element_type=F32)
        acum_t = acum.T
        ac_last = acum[q - 1:q, :]
        cd = jnp.exp(ac_last)
        for g in range(SSM_GROUPS):
            x0, x1 = gw * g, gw * (g + 1)
            n0, n1 = ns * g, ns * (g + 1)
            xc, tail_x = conv(x_ref[u, :, x0:x1], sx, u, x0, x1, cwx_ref, cbx_ref)
            bc, tail_b = conv(b_ref[u, :, n0:n1], sb, u, n0, n1, cwb_ref, cbb_ref)
            cc, tail_c = conv(c_ref[u, :, n0:n1], sc, u, n0, n1, cwc_ref, cbc_ref)

            dt_e = expand(dt_all, g)
            ac_e = expand(acum, g)
            ac_last_e = expand(ac_last, g)

            xdt = xc * dt_e
            xdt_b = xdt.astype(BF16)
            bb_ = bc.astype(BF16)
            cb_ = cc.astype(BF16)
            cbm = _dot_nt(cb_, bb_)
            hprev = hs[u, g]
            y = _dot_nt(cb_, hprev.astype(BF16)) * jnp.exp(ac_e)
            zero_b = jnp.zeros_like(xdt_b)
            for j in range(hpg):
                hd = hpg * g + j
                seg = acum[:, hd:hd + 1] - acum_t[hd:hd + 1, :]
                lm = jnp.exp(jnp.where(tri, seg, -jnp.inf))
                wj = (cbm * lm).astype(BF16)
                xm = jnp.where((lane >= SSM_HEAD_DIM * j) & (lane < SSM_HEAD_DIM * (j + 1)), xdt_b, zero_b)
                y = y + jnp.dot(wj, xm, preferred_element_type=F32)

            xw = (xdt * jnp.exp(ac_last_e - ac_e)).astype(BF16)
            st = _dot_tn(xw, bb_)
            for j in range(hpg):
                hd = hpg * g + j
                r0, r1 = SSM_HEAD_DIM * j, SSM_HEAD_DIM * (j + 1)
                hs[u, g, r0:r1, :] = (jnp.broadcast_to(cd[:, hd:hd + 1], (SSM_HEAD_DIM, ns)) * hprev[r0:r1, :]
                                      + st[r0:r1, :])

            y = y + expand(ptab[2:3, :], g) * xc
            zf = z_ref[u, :, x0:x1].astype(F32)
            y = y * _silu(zf)
            ms = jnp.mean(y * y, axis=-1, keepdims=True)
            y_ref[u, :, x0:x1] = (y * lax.rsqrt(ms + LN_EPS) * nw_ref[:, x0:x1]).astype(y_ref.dtype)
            tails.append((u, x0, x1, n0, n1, tail_x, tail_b, tail_c))

    @pl.when(c == nc - 1)
    def _():
        for u, x0, x1, n0, n1, tail_x, tail_b, tail_c in tails:
            cox_ref[u, :, x0:x1] = tail_x
            cob_ref[u, :, n0:n1] = tail_b
            coc_ref[u, :, n0:n1] = tail_c
        hout_ref[...] = hs[...]


def _mamba_core(zx, dt, conv0, ssm0, ptab, conv_w, conv_b, norm_w, bsz, seqlen):
    q = math.gcd(seqlen, SSM_CHUNK)
    nc = seqlen // q
    g = SSM_GROUPS
    bb = 1 if nc > 1 else math.gcd(bsz, MAMBA_SEQS_PER_STEP)
    di, gn = SSM_D_INNER, g * SSM_D_STATE
    zx3 = zx.reshape(bsz, seqlen, zx.shape[-1])
    dt3 = dt.reshape(bsz, seqlen, LANES)
    h0 = ssm0.reshape(bsz, g, GROUP_WIDTH, SSM_D_STATE)
    hist = SSM_CONV - 1
    in_specs = [
        pl.BlockSpec((bb, q, di), lambda b, c: (b, c, 0)),
        pl.BlockSpec((bb, q, di), lambda b, c: (b, c, 1)),
        pl.BlockSpec((bb, q, gn), lambda b, c: (b, c, 2 * di // gn)),
        pl.BlockSpec((bb, q, gn), lambda b, c: (b, c, 2 * di // gn + 1)),
        pl.BlockSpec((bb, q, LANES), lambda b, c: (b, c, 0)),
        pl.BlockSpec((SUBLANES, LANES), lambda b, c: (0, 0)),
        pl.BlockSpec((SSM_CONV, di), lambda b, c: (0, 0)),
        pl.BlockSpec((SSM_CONV, gn), lambda b, c: (0, di // gn)),
        pl.BlockSpec((SSM_CONV, gn), lambda b, c: (0, di // gn + 1)),
        pl.BlockSpec((1, di), lambda b, c: (0, 0)),
        pl.BlockSpec((1, gn), lambda b, c: (0, di // gn)),
        pl.BlockSpec((1, gn), lambda b, c: (0, di // gn + 1)),
        pl.BlockSpec((bb, hist, di), lambda b, c: (b, 0, 0)),
        pl.BlockSpec((bb, hist, gn), lambda b, c: (b, 0, di // gn)),
        pl.BlockSpec((bb, hist, gn), lambda b, c: (b, 0, di // gn + 1)),
        pl.BlockSpec((bb, g, GROUP_WIDTH, SSM_D_STATE), lambda b, c: (b, 0, 0, 0)),
        pl.BlockSpec((1, di), lambda b, c: (0, 0)),
    ]
    out_specs = [
        pl.BlockSpec((bb, q, di), lambda b, c: (b, c, 0)),
        pl.BlockSpec((bb, hist, di), lambda b, c: (b, 0, 0)),
        pl.BlockSpec((bb, hist, gn), lambda b, c: (b, 0, 0)),
        pl.BlockSpec((bb, hist, gn), lambda b, c: (b, 0, 0)),
        pl.BlockSpec((bb, g, GROUP_WIDTH, SSM_D_STATE), lambda b, c: (b, 0, 0, 0)),
    ]
    out_shape = [
        jax.ShapeDtypeStruct((bsz, seqlen, di), BF16),
        jax.ShapeDtypeStruct((bsz, hist, di), F32),
        jax.ShapeDtypeStruct((bsz, hist, gn), F32),
        jax.ShapeDtypeStruct((bsz, hist, gn), F32),
        jax.ShapeDtypeStruct((bsz, g, GROUP_WIDTH, SSM_D_STATE), F32),
    ]
    scratch = [
        pltpu.VMEM((bb, q + SUBLANES, di), F32),
        pltpu.VMEM((bb, q + SUBLANES, gn), F32),
        pltpu.VMEM((bb, q + SUBLANES, gn), F32),
        pltpu.VMEM((bb, g, GROUP_WIDTH, SSM_D_STATE), F32),
    ]
    y, cox, cob, coc, hout = pl.pallas_call(
        functools.partial(_mamba_kernel, q=q, nc=nc, bb=bb),
        grid=(bsz // bb, nc),
        in_specs=in_specs,
        out_specs=out_specs,
        out_shape=out_shape,
        scratch_shapes=scratch,
        compiler_params=_cparams("parallel", "arbitrary"),
        name="mamba_core",
    )(zx3, zx3, zx3, zx3, dt3, ptab, conv_w, conv_w, conv_w, conv_b, conv_b, conv_b,
      conv0, conv0, conv0, h0, norm_w)
    new_conv = jnp.concatenate([cox, cob, coc], axis=-1)
    return (y.reshape(bsz * seqlen, SSM_D_INNER), new_conv,
            hout.reshape(bsz, SSM_HEADS, SSM_HEAD_DIM, SSM_D_STATE))


def _post_mixer_kernel(y_ref, w_ref, xres_ref, g_ref, b_ref, wr_ref, br_ref,
                       x1_ref, route_ref, route_t_ref, cnt_ref, base, *, tm):
    i = pl.program_id(0)

    @pl.when(i == 0)
    def _():
        base[...] = jnp.zeros_like(base)

    h = jnp.dot(y_ref[...], w_ref[...], preferred_element_type=F32)
    x1 = _layer_norm(DN_ALPHA * xres_ref[...] + h, g_ref[...], b_ref[...])
    x1_ref[...] = x1

    x_hi = x1.astype(BF16)
    x_lo = (x1 - x_hi.astype(F32)).astype(BF16)
    both = jnp.dot(x_hi, wr_ref[...], preferred_element_type=F32)
    logits = (both[:, :LANES] + both[:, LANES:]
              + jnp.dot(x_lo, wr_ref[:, :LANES], preferred_element_type=F32) + br_ref[...])
    lane = lax.broadcasted_iota(jnp.int32, (tm, LANES), 1)
    neg = -jnp.inf
    gmask = lane < MOE_GROUPS
    gl = jnp.where(gmask, logits, neg)
    mg = jnp.max(gl, axis=1, keepdims=True)
    grp = jnp.min(jnp.where(gl == mg, lane, LANES), axis=1, keepdims=True)
    pg = 1.0 / jnp.sum(jnp.where(gmask, jnp.exp(logits - mg), 0.0), axis=1, keepdims=True)
    lo = MOE_GROUPS + MOE_EXPERTS_PER_GROUP * grp
    el = jnp.where((lane >= lo) & (lane < lo + MOE_EXPERTS_PER_GROUP), logits, neg)
    v1 = jnp.max(el, axis=1, keepdims=True)
    i1 = jnp.min(jnp.where(el == v1, lane, LANES), axis=1, keepdims=True)
    el2 = jnp.where(lane == i1, neg, el)
    v2 = jnp.max(el2, axis=1, keepdims=True)
    i2 = jnp.min(jnp.where(el2 == v2, lane, LANES), axis=1, keepdims=True)
    t = jnp.exp(v2 - v1)
    wa = pg / (1.0 + t)
    wb = pg * t / (1.0 + t)
    e1 = i1 - MOE_GROUPS
    e2 = i2 - MOE_GROUPS

    oh = ((lane == e1) | (lane == e2)).astype(BF16)
    ri = lax.broadcasted_iota(jnp.int32, (tm, tm), 0)
    ci = lax.broadcasted_iota(jnp.int32, (tm, tm), 1)
    before = jnp.dot((ri > ci).astype(BF16), oh, preferred_element_type=F32) + base[...]
    r1 = jnp.sum(jnp.where(lane == e1, before, 0.0), axis=1, keepdims=True)
    r2 = jnp.sum(jnp.where(lane == e2, before, 0.0), axis=1, keepdims=True)
    new_base = base[...] + jnp.sum(oh.astype(F32), axis=0, keepdims=True)
    base[...] = new_base
    cnt_ref[...] = new_base

    route = jnp.where(lane == 0, e1.astype(F32), 0.0)
    route = jnp.where(lane == 1, e2.astype(F32), route)
    route = jnp.where(lane == 2, r1, route)
    route = jnp.where(lane == 3, r2, route)
    route = jnp.where(lane == 4, wa, route)
    route = jnp.where(lane == 5, wb, route)
    route_ref[...] = route
    route_t_ref[...] = route.T[0:SUBLANES, :]


def _post_mixer(y, w, xres, ln_g, ln_b, wr, br, name):
    t, kin = y.shape
    tm = min(PROJ_TILE, t)
    assert t % tm == 0
    return pl.pallas_call(
        functools.partial(_post_mixer_kernel, tm=tm),
        grid=(t // tm,),
        in_specs=[
            pl.BlockSpec((tm, kin), lambda i: (i, 0)),
            pl.BlockSpec((kin, D_MODEL), lambda i: (0, 0)),
            pl.BlockSpec((tm, D_MODEL), lambda i: (i, 0)),
            pl.BlockSpec((1, D_MODEL), lambda i: (0, 0)),
            pl.BlockSpec((1, D_MODEL), lambda i: (0, 0)),
            pl.BlockSpec((D_MODEL, 2 * LANES), lambda i: (0, 0)),
            pl.BlockSpec((1, LANES), lambda i: (0, 0)),
        ],
        out_specs=[
            pl.BlockSpec((tm, D_MODEL), lambda i: (i, 0)),
            pl.BlockSpec((tm, LANES), lambda i: (i, 0)),
            pl.BlockSpec((SUBLANES, tm), lambda i: (0, i)),
            pl.BlockSpec((1, LANES), lambda i: (0, 0)),
        ],
        out_shape=[
            jax.ShapeDtypeStruct((t, D_MODEL), F32),
            jax.ShapeDtypeStruct((t, LANES), F32),
            jax.ShapeDtypeStruct((SUBLANES, t), F32),
            jax.ShapeDtypeStruct((1, LANES), F32),
        ],
        scratch_shapes=[pltpu.VMEM((1, LANES), F32)],
        compiler_params=_cparams("arbitrary"),
        name=name,
    )(y, w, xres, ln_g, ln_b, wr, br)


def _row_copy(src_ref, src_row, dst_ref, dst_row, sem):
    return pltpu.make_async_copy(src_ref.at[pl.ds(src_row, 1)], dst_ref.at[pl.ds(dst_row, 1)], sem)


def _dispatch_kernel(dest_ref, *refs, tm, tiles):
    x_refs, xs_ref, sem = refs[:len(tiles)], refs[-2], refs[-1]
    i = pl.program_id(0)
    first = 0
    for x_ref, n_tiles in zip(x_refs, tiles):
        @pl.when((i >= first) & (i < first + n_tiles))
        def _(x_ref=x_ref):
            def start(r, carry):
                _row_copy(x_ref, r, xs_ref, dest_ref[0, 0, r], sem).start()
                _row_copy(x_ref, r, xs_ref, dest_ref[0, 0, tm + r], sem).start(priority=1)
                return carry

            lax.fori_loop(0, tm, start, 0, unroll=True)
            for _ in range(2):
                pltpu.make_async_copy(x_ref, xs_ref.at[pl.ds(0, tm)], sem).wait()

        first += n_tiles


def _dispatch(xs_in, dest3, tm, n_rows):
    tiles = [x.shape[0] // tm for x in xs_in]
    starts = [sum(tiles[:k]) for k in range(len(tiles))]
    in_specs = [pl.BlockSpec((1, 1, 2 * tm), lambda i: (i, 0, 0), memory_space=pltpu.SMEM)]
    for first, n_tiles in zip(starts, tiles):
        in_specs.append(pl.BlockSpec(
            (tm, D_MODEL), lambda i, first=first, n_tiles=n_tiles: (jnp.clip(i - first, 0, n_tiles - 1), 0)))
    return pl.pallas_call(
        functools.partial(_dispatch_kernel, tm=tm, tiles=tuple(tiles)),
        grid=(sum(tiles),),
        in_specs=in_specs,
        out_specs=pl.BlockSpec(memory_space=pl.ANY),
        out_shape=jax.ShapeDtypeStruct((n_rows, D_MODEL), F32),
        scratch_shapes=[pltpu.SemaphoreType.DMA(())],
        compiler_params=_cparams("arbitrary"),
        name="moe_dispatch",
    )(dest3, *xs_in)


def _ffn_kernel(wt_ref, we_ref, wlo_ref, whi_ref, wfirst_ref, wnew_ref,
                xs_ref, w1_ref, w3_ref, w2_ref, ys_ref, w1b, w3b, w2b, *, tf):
    w = pl.program_id(0)
    lo = wlo_ref[w]
    hi = whi_ref[w]

    @pl.when(wfirst_ref[w] == 1)
    def _():
        ys_ref[...] = jnp.zeros_like(ys_ref)

    @pl.when(wnew_ref[w] == 1)
    def _():
        w1b[...] = w1_ref[0].astype(BF16)
        w3b[...] = w3_ref[0].astype(BF16)
        w2b[...] = w2_ref[0].astype(BF16)

    @pl.when(hi > lo)
    def _():
        rows = wt_ref[w] * tf + lax.broadcasted_iota(jnp.int32, (tf, 1), 0)
        mask = (rows >= lo) & (rows < hi)
        xb = xs_ref[...].astype(BF16)
        a = jnp.dot(xb, w1b[...], preferred_element_type=F32)
        b = jnp.dot(xb, w3b[...], preferred_element_type=F32)
        hmid = (_silu(a) * b).astype(BF16)
        y = jnp.dot(hmid, w2b[...], preferred_element_type=F32)
        ys_ref[...] = jnp.where(mask, y, ys_ref[...])


def _ffn(xs, w1, w3, w2, work, tf):
    n = xs.shape[0]
    n_work = work[0].shape[0]
    grid_spec = pltpu.PrefetchScalarGridSpec(
        num_scalar_prefetch=6,
        grid=(n_work,),
        in_specs=[
            pl.BlockSpec((tf, D_MODEL), lambda w, wt, we, wlo, whi, wf, wn: (wt[w], 0)),
            pl.BlockSpec((1, D_MODEL, MOE_FF), lambda w, wt, we, wlo, whi, wf, wn: (we[w], 0, 0)),
            pl.BlockSpec((1, D_MODEL, MOE_FF), lambda w, wt, we, wlo, whi, wf, wn: (we[w], 0, 0)),
            pl.BlockSpec((1, MOE_FF, D_MODEL), lambda w, wt, we, wlo, whi, wf, wn: (we[w], 0, 0)),
        ],
        out_specs=pl.BlockSpec((tf, D_MODEL), lambda w, wt, we, wlo, whi, wf, wn: (wt[w], 0)),
        scratch_shapes=[
            pltpu.VMEM((D_MODEL, MOE_FF), BF16),
            pltpu.VMEM((D_MODEL, MOE_FF), BF16),
            pltpu.VMEM((MOE_FF, D_MODEL), BF16),
        ],
    )
    return pl.pallas_call(
        functools.partial(_ffn_kernel, tf=tf),
        grid_spec=grid_spec,
        out_shape=jax.ShapeDtypeStruct((n, D_MODEL), F32),
        compiler_params=_cparams("arbitrary"),
        name="moe_ffn",
    )(*work, xs, w1, w3, w2)


def _combine_kernel(dcur_ref, dnext_ref, x1_ref, route_ref, g_ref, b_ref, ys_ref, o_ref, buf, sems, *, tm, n_tiles):
    i = pl.program_id(0)
    slot = lax.rem(i, 2)

    def issue(dref, s):
        def start(r, carry):
            _row_copy(ys_ref, dref[0, 0, r], buf.at[s, 0], r, sems.at[s]).start()
            _row_copy(ys_ref, dref[0, 0, tm + r], buf.at[s, 1], r, sems.at[s]).start(priority=1)
            return carry

        lax.fori_loop(0, tm, start, 0, unroll=True)

    @pl.when(i == 0)
    def _():
        issue(dcur_ref, 0)

    @pl.when(i + 1 < n_tiles)
    def _():
        issue(dnext_ref, 1 - slot)

    for k in range(2):
        pltpu.make_async_copy(ys_ref.at[pl.ds(0, tm)], buf.at[slot, k], sems.at[slot]).wait()

    route = route_ref[...]
    y = route[:, 4:5] * buf[slot, 0] + route[:, 5:6] * buf[slot, 1]
    o_ref[...] = _layer_norm(DN_ALPHA * x1_ref[...] + y, g_ref[...], b_ref[...])


def _combine(x1, route, ys, dest3, ln_g, ln_b, tm, name):
    t = x1.shape[0]
    n_tiles = t // tm
    return pl.pallas_call(
        functools.partial(_combine_kernel, tm=tm, n_tiles=n_tiles),
        grid=(n_tiles,),
        in_specs=[
            pl.BlockSpec((1, 1, 2 * tm), lambda i: (i, 0, 0), memory_space=pltpu.SMEM),
            pl.BlockSpec((1, 1, 2 * tm), lambda i: (jnp.minimum(i + 1, n_tiles - 1), 0, 0), memory_space=pltpu.SMEM),
            pl.BlockSpec((tm, D_MODEL), lambda i: (i, 0)),
            pl.BlockSpec((tm, LANES), lambda i: (i, 0)),
            pl.BlockSpec((1, D_MODEL), lambda i: (0, 0)),
            pl.BlockSpec((1, D_MODEL), lambda i: (0, 0)),
            pl.BlockSpec(memory_space=pl.ANY),
        ],
        out_specs=pl.BlockSpec((tm, D_MODEL), lambda i: (i, 0)),
        out_shape=jax.ShapeDtypeStruct((t, D_MODEL), F32),
        scratch_shapes=[
            pltpu.VMEM((2, 2, tm, D_MODEL), F32),
            pltpu.SemaphoreType.DMA((2,)),
        ],
        compiler_params=_cparams("arbitrary"),
        name=name,
    )(dest3, dest3, x1, route, ln_g, ln_b, ys)


def _moe(parts, w1, w3, w2, expert_base, ln_g, ln_b, name):
    tf = FFN_TILE
    lanes_e = jnp.arange(MOE_EXPERTS, dtype=jnp.int32)
    cnts = [p[3][0, :MOE_EXPERTS].astype(jnp.int32) for p in parts]
    cnt = functools.reduce(lambda a, b: a + b, cnts)
    offs = jnp.concatenate([jnp.zeros((1,), jnp.int32), jnp.cumsum(cnt)])
    n = 2 * sum(p[0].shape[0] for p in parts)
    assert n % tf == 0
    n_tiles = n // tf

    dests = []
    seg = offs[:-1]
    for (x1, route, route_t, counts), c in zip(parts, cnts):
        t = x1.shape[0]
        tm = min(TOKEN_TILE, t)
        e = route_t[0:2].astype(jnp.int32)
        r = route_t[2:4].astype(jnp.int32)
        starts = jnp.where(e[None] == lanes_e[:, None, None], seg[:, None, None], 0)
        dest = jnp.sum(starts, axis=0) + r
        dests.append(jnp.transpose(dest.reshape(2, t // tm, tm), (1, 0, 2)).reshape(t // tm, 1, 2 * tm))
        seg = seg + c

    n_work = n_tiles + MOE_EXPERTS
    first_tile = offs[:-1] // tf
    last_tile = (offs[1:] - 1) // tf
    ntile_e = jnp.where(cnt > 0, last_tile - first_tile + 1, 0)
    wstart = jnp.concatenate([jnp.zeros((1,), jnp.int32), jnp.cumsum(ntile_e)])
    total = wstart[-1]
    widx = jnp.arange(n_work, dtype=jnp.int32)
    we = jnp.sum((wstart[None, 1:] <= widx[:, None]).astype(jnp.int32), axis=1)
    we = jnp.minimum(we, MOE_EXPERTS - 1)
    onehot = we[:, None] == lanes_e[None, :]

    def pick(v):
        return jnp.sum(jnp.where(onehot, v[None, :], 0), axis=1)

    valid = widx < total
    wt = jnp.where(valid, pick(first_tile) + widx - pick(wstart[:-1]), n_tiles - 1).astype(jnp.int32)
    wlo = jnp.where(valid, pick(offs[:-1]), 0).astype(jnp.int32)
    whi = jnp.where(valid, pick(offs[1:]), 0).astype(jnp.int32)
    last_e = jnp.max(jnp.where(valid, we, 0))
    we = jnp.where(valid, we, last_e).astype(jnp.int32)
    wfirst = jnp.concatenate([jnp.ones((1,), jnp.int32), (wt[1:] != wt[:-1]).astype(jnp.int32)])
    wnew = jnp.concatenate([jnp.ones((1,), jnp.int32), (we[1:] != we[:-1]).astype(jnp.int32)])

    tm = min(TOKEN_TILE, min(p[0].shape[0] for p in parts))
    assert all(p[0].shape[0] % tm == 0 and min(TOKEN_TILE, p[0].shape[0]) == tm for p in parts)
    xs = _dispatch([p[0] for p in parts], jnp.concatenate(dests, axis=0), tm, n)
    ys = _ffn(xs, w1, w3, w2, (wt, we + expert_base, wlo, whi, wfirst, wnew), tf)
    return [_combine(x1, route, ys, dest3, ln_g, ln_b, min(TOKEN_TILE, x1.shape[0]), name)
            for (x1, route, route_t, counts), dest3 in zip(parts, dests)]


def _bias_kernel(idx_ref, rbt_ref, o_ref, *, width):
    idx = idx_ref[...]
    bucket = lax.broadcasted_iota(jnp.int32, (NUM_BUCKETS, width), 0)
    onehot = (bucket == idx).astype(F32)
    vals = jnp.dot(rbt_ref[...], onehot, precision=HIGHEST, preferred_element_type=F32)
    o_ref[...] = jnp.where(idx < 0, -jnp.inf, vals)


def _bias_tiles(rel_bias, idx_np):
    n = idx_np.shape[0]
    width = 8192
    assert n % width == 0
    nh = rel_bias.shape[1]
    return pl.pallas_call(
        functools.partial(_bias_kernel, width=width),
        grid=(n // width,),
        in_specs=[
            pl.BlockSpec((1, width), lambda i: (0, i)),
            pl.BlockSpec((nh, NUM_BUCKETS), lambda i: (0, 0)),
        ],
        out_specs=pl.BlockSpec((nh, width), lambda i: (0, i)),
        out_shape=jax.ShapeDtypeStruct((nh, n), F32),
        compiler_params=_cparams("parallel"),
        name="rel_bias_tiles",
    )(jnp.asarray(idx_np.reshape(1, n)), rel_bias.T)


def _bucket_of_distance(dist):
    n = np.maximum(dist, 0)
    max_exact = NUM_BUCKETS // 2
    nf = np.maximum(n, 1).astype(np.float32)
    large = max_exact + (np.log(nf / np.float32(max_exact)) / np.float32(math.log(MAX_DISTANCE / max_exact))
                         * np.float32(NUM_BUCKETS - max_exact)).astype(np.int32)
    large = np.minimum(large, NUM_BUCKETS - 1)
    return np.where(n < max_exact, n, large).astype(np.int32)


def _bucket_tile(dist):
    return np.where(dist >= 0, _bucket_of_distance(dist), -1).astype(np.int32)


def _kvq_prompt_kernel(x_ref, wkt_ref, wv_ref, wq_ref, kt_ref, ktb_ref, v_ref, vb_ref, q_ref, *, chunk):
    xb = x_ref[...].astype(BF16)
    hd2 = 2 * ATT_HEAD_DIM
    for c0 in range(0, ATT_QK_DIM, chunk):
        kt = _dot_nt(wkt_ref[c0:c0 + chunk, :], xb)
        kt_ref[0, c0:c0 + chunk, :] = kt
        ktb_ref[0, 0, c0:c0 + chunk, :] = kt.astype(BF16)
    for c0 in range(0, ATT_V_DIM, chunk):
        v = jnp.dot(xb, wv_ref[:, c0:c0 + chunk], preferred_element_type=F32)
        vb_ref[:, c0:c0 + chunk] = v.astype(BF16)
        for h0 in range(0, chunk, hd2):
            v_ref[:, (c0 + h0) // hd2, :] = v[:, h0:h0 + hd2]
    for c0 in range(0, ATT_QK_DIM, chunk):
        q_ref[:, c0:c0 + chunk] = jnp.dot(xb, wq_ref[:, c0:c0 + chunk], preferred_element_type=F32).astype(BF16)


def _kvq_prompt(x, wkt, wv, wq, bsz, seqlen):
    t = bsz * seqlen
    tm = min(ATT_TILE, seqlen)
    nkb = seqlen // tm
    hd2 = 2 * ATT_HEAD_DIM
    full = lambda i: (0, 0)
    return pl.pallas_call(
        functools.partial(_kvq_prompt_kernel, chunk=512),
        grid=(t // tm,),
        in_specs=[
            pl.BlockSpec((tm, D_MODEL), lambda i: (i, 0)),
            pl.BlockSpec(wkt.shape, full),
            pl.BlockSpec(wv.shape, full),
            pl.BlockSpec(wq.shape, full),
        ],
        out_specs=[
            pl.BlockSpec((1, ATT_QK_DIM, tm), lambda i: (i // nkb, 0, i % nkb)),
            pl.BlockSpec((1, 1, ATT_QK_DIM, tm), lambda i: (i // nkb, i % nkb, 0, 0)),
            pl.BlockSpec((tm, ATT_HEADS, hd2), lambda i: (i, 0, 0)),
            pl.BlockSpec((tm, ATT_V_DIM), lambda i: (i, 0)),
            pl.BlockSpec((tm, ATT_QK_DIM), lambda i: (i, 0)),
        ],
        out_shape=[
            jax.ShapeDtypeStruct((bsz, ATT_QK_DIM, seqlen), F32),
            jax.ShapeDtypeStruct((bsz, nkb, ATT_QK_DIM, tm), BF16),
            jax.ShapeDtypeStruct((t, ATT_HEADS, hd2), F32),
            jax.ShapeDtypeStruct((t, ATT_V_DIM), BF16),
            jax.ShapeDtypeStruct((t, ATT_QK_DIM), BF16),
        ],
        compiler_params=_cparams("parallel"),
        name="kvq_proj_prompt",
    )(x, wkt, wv, wq)


def _lambda_value(lam_ref, layer_idx):
    lv = lam_ref[...]
    s1 = jnp.sum(lv[0:1, :] * lv[1:2, :], axis=1, keepdims=True)
    s2 = jnp.sum(lv[2:3, :] * lv[3:4, :], axis=1, keepdims=True)
    lam_init = 0.8 - 0.6 * math.exp(-0.3 * layer_idx)
    return jnp.exp(s1) - jnp.exp(s2) + lam_init, lam_init


def _attn_prompt_kernel(q_ref, kt_ref, v_ref, bias_ref, lam_ref, sw_ref, o_ref, *, tq, nq, layer_idx):
    lane = lax.broadcasted_iota(jnp.int32, (tq, 2 * ATT_HEAD_DIM), 1)
    lam, lam_init = _lambda_value(lam_ref, layer_idx)
    for qi in range(nq):
        qb = q_ref[0, qi * tq:(qi + 1) * tq, :]
        zero = jnp.zeros_like(qb)
        q2 = jnp.concatenate([jnp.where(lane < ATT_HEAD_DIM, qb, zero),
                              jnp.where(lane >= ATT_HEAD_DIM, qb, zero)], axis=0)
        m = l = acc = None
        for j in range(qi + 1):
            s = jnp.dot(q2, kt_ref[0, j], preferred_element_type=F32)
            if qi - j < 2:
                s = s + bias_ref[0, qi - j]
            vj = v_ref[0, j * tq:(j + 1) * tq, :]
            bm = jnp.max(s, axis=1, keepdims=True)
            if j == 0:
                m = bm
                p = jnp.exp2(s - m)
                l = jnp.sum(p, axis=1, keepdims=True)
                acc = jnp.dot(p.astype(BF16), vj, preferred_element_type=F32)
            else:
                m_new = jnp.maximum(m, bm)
                alpha = jnp.exp2(m - m_new)
                p = jnp.exp2(s - m_new)
                l = alpha * l + jnp.sum(p, axis=1, keepdims=True)
                acc = alpha * acc + jnp.dot(p.astype(BF16), vj, preferred_element_type=F32)
                m = m_new
        o = acc / l
        d = o[:tq] - lam * o[tq:]
        ms = jnp.mean(d * d, axis=-1, keepdims=True)
        o_ref[0, qi * tq:(qi + 1) * tq, :] = (
            d * lax.rsqrt(ms + LN_EPS) * sw_ref[...] * (1.0 - lam_init)).astype(o_ref.dtype)


def _attn_prompt(q, ktb, vb, bias, lam_vec, subln_w, bsz, seqlen, layer_idx):
    tq = min(ATT_TILE, seqlen)
    nq = seqlen // tq
    hd2 = 2 * ATT_HEAD_DIM
    q3 = q.reshape(bsz, seqlen, ATT_QK_DIM)
    v3 = vb.reshape(bsz, seqlen, ATT_V_DIM)
    out = pl.pallas_call(
        functools.partial(_attn_prompt_kernel, tq=tq, nq=nq, layer_idx=layer_idx),
        grid=(bsz, ATT_HEADS),
        in_specs=[
            pl.BlockSpec((1, seqlen, hd2), lambda b, h: (b, 0, h)),
            pl.BlockSpec((1, nq, hd2, tq), lambda b, h: (b, 0, h, 0)),
            pl.BlockSpec((1, seqlen, hd2), lambda b, h: (b, 0, h)),
            pl.BlockSpec((1, 2, 2 * tq, tq), lambda b, h: (h, 0, 0, 0)),
            pl.BlockSpec(lam_vec.shape, lambda b, h: (0, 0)),
            pl.BlockSpec((1, hd2), lambda b, h: (0, 0)),
        ],
        out_specs=pl.BlockSpec((1, seqlen, hd2), lambda b, h: (b, 0, h)),
        out_shape=jax.ShapeDtypeStruct((bsz, seqlen, ATT_V_DIM), BF16),
        compiler_params=_cparams("parallel", "parallel"),
        name="attn_prompt",
    )(q3, ktb, v3, bias, lam_vec, subln_w)
    return out.reshape(bsz * seqlen, ATT_V_DIM)


def _attn_sample_kernel(pt_ref, q_ref, *refs, nq, n_steps, pp, layer_idx):
    kt_refs = refs[:pp]
    vm_refs = refs[pp:2 * pp]
    (kn_ref, vn_ref, bias_ref, exp_ref, pmask_ref, lam_ref, sw_ref,
     o_ref, qbd, m_s, l_s, acc_s, kpad, vpad) = refs[2 * pp:]
    j = pl.program_id(1)
    maps = 2 * ATT_HEADS
    rows = maps * nq
    hd2 = 2 * ATT_HEAD_DIM
    page = PAGE_SIZE

    @pl.when(j == 0)
    def _():
        m_s[...] = jnp.full_like(m_s, -jnp.inf)
        l_s[...] = jnp.zeros_like(l_s)
        acc_s[...] = jnp.zeros_like(acc_s)
        qt = jnp.concatenate([q_ref[0].astype(F32)] * maps, axis=0)
        ri = lax.broadcasted_iota(jnp.int32, (rows, ATT_QK_DIM), 0)
        ci = lax.broadcasted_iota(jnp.int32, (rows, ATT_QK_DIM), 1)
        keep = (ri // nq) == (ci // ATT_HEAD_DIM)
        qbd[...] = jnp.where(keep, qt, 0.0).astype(qbd.dtype)

    def softmax_update(s):
        m = m_s[...]
        m_new = jnp.maximum(m, jnp.max(s, axis=1, keepdims=True))
        alpha = jnp.exp2(m - m_new)
        p = jnp.exp2(s - m_new)
        l_s[...] = alpha * l_s[...] + jnp.sum(p, axis=1, keepdims=True)
        m_s[...] = m_new
        return alpha, p.astype(BF16)

    def past_pages():
        kt_all = jnp.concatenate(
            [kt_refs[u][0].reshape(ATT_QK_DIM, page).astype(BF16) for u in range(pp)], axis=1)
        s = jnp.dot(qbd[...], kt_all, preferred_element_type=F32)
        last = bias_ref[jnp.where(j == n_steps - 1, 0, 2)]
        s = jnp.concatenate([s[:, :page * (pp - 1)], s[:, page * (pp - 1):] + last], axis=1)
        alpha, pb = softmax_update(s)
        p_rows = jnp.concatenate([pb[:, page * u:page * (u + 1)] for u in range(pp)], axis=0)
        pe = jnp.dot(p_rows, exp_ref[...], preferred_element_type=F32)
        pe_cat = jnp.concatenate(
            [pe[rows * u:rows * (u + 1)].astype(BF16) * pmask_ref[...] for u in range(pp)], axis=1)
        vm_all = jnp.concatenate(
            [vm_refs[u][0].reshape(page * ATT_HEADS, hd2).astype(BF16) for u in range(pp)], axis=0)
        acc_s[...] = alpha * acc_s[...] + jnp.dot(pe_cat, vm_all, preferred_element_type=F32)

    past_pages()

    @pl.when(j == n_steps - 1)
    def _():
        kpad[...] = jnp.zeros_like(kpad)
        vpad[...] = jnp.zeros_like(vpad)
        kpad[0:nq, :] = kn_ref[0]
        vpad[0:nq, :] = vn_ref[0]
        s = _dot_nt(qbd[...], kpad[...].astype(BF16)) + bias_ref[1]
        alpha, pb = softmax_update(s)
        for h in range(ATT_HEADS):
            r0, r1 = 2 * nq * h, 2 * nq * (h + 1)
            pvh = jnp.dot(pb[r0:r1, :], vpad[:, hd2 * h:hd2 * (h + 1)].astype(BF16), preferred_element_type=F32)
            acc_s[r0:r1, :] = alpha[r0:r1, :] * acc_s[r0:r1, :] + pvh
        o = acc_s[...] / l_s[...]
        lam, lam_init = _lambda_value(lam_ref, layer_idx)
        for h in range(ATT_HEADS):
            r0 = 2 * nq * h
            d = o[r0:r0 + nq, :] - lam * o[r0 + nq:r0 + 2 * nq, :]
            ms = jnp.mean(d * d, axis=-1, keepdims=True)
            o_ref[0, :, hd2 * h:hd2 * (h + 1)] = (
                d * lax.rsqrt(ms + LN_EPS) * sw_ref[...] * (1.0 - lam_init)).astype(o_ref.dtype)


def _attn_sample(q, kv, cache_k, cache_v, page_table, bias, lam_vec, subln_w, bsz, nq, layer_idx):
    n_pages = page_table.shape[1]
    page = cache_k.shape[1]
    maps = 2 * ATT_HEADS
    rows = maps * nq
    hd2 = 2 * ATT_HEAD_DIM
    pp = math.gcd(n_pages, SAMPLE_PAGES_PER_STEP)
    n_steps = n_pages // pp
    q3 = q.reshape(bsz, nq, ATT_QK_DIM)
    kv3 = kv.reshape(bsz, nq, ATT_QK_DIM + ATT_V_DIM)
    cache_kt = jnp.transpose(cache_k, (0, 2, 3, 1))
    kk = np.arange(page)[:, None]
    cc = np.arange(page * ATT_HEADS)[None, :]
    expand = jnp.asarray((cc // ATT_HEADS == kk).astype(np.float32), dtype=BF16)
    rr = np.arange(rows)[:, None]
    pmask = jnp.asarray(((rr // (2 * nq)) == (cc % ATT_HEADS)).astype(np.float32), dtype=BF16)

    def page_map(u):
        return lambda b, j, pt: (pt[b, j * pp + u], 0, 0, 0)

    in_specs = [pl.BlockSpec((1, nq, ATT_QK_DIM), lambda b, j, pt: (b, 0, 0))]
    in_specs += [pl.BlockSpec((1, maps, ATT_HEAD_DIM, page), page_map(u)) for u in range(pp)]
    in_specs += [pl.BlockSpec((1, page, ATT_HEADS, hd2), page_map(u)) for u in range(pp)]
    in_specs += [
        pl.BlockSpec((1, nq, ATT_QK_DIM), lambda b, j, pt: (b, 0, 0)),
        pl.BlockSpec((1, nq, ATT_V_DIM), lambda b, j, pt: (b, 0, 1)),
        pl.BlockSpec((3, rows, page), lambda b, j, pt: (0, 0, 0)),
        pl.BlockSpec((page, page * ATT_HEADS), lambda b, j, pt: (0, 0)),
        pl.BlockSpec((rows, page * ATT_HEADS), lambda b, j, pt: (0, 0)),
        pl.BlockSpec(lam_vec.shape, lambda b, j, pt: (0, 0)),
        pl.BlockSpec((1, hd2), lambda b, j, pt: (0, 0)),
    ]
    grid_spec = pltpu.PrefetchScalarGridSpec(
        num_scalar_prefetch=1,
        grid=(bsz, n_steps),
        in_specs=in_specs,
        out_specs=pl.BlockSpec((1, nq, ATT_V_DIM), lambda b, j, pt: (b, 0, 0)),
        scratch_shapes=[
            pltpu.VMEM((rows, ATT_QK_DIM), BF16),
            pltpu.VMEM((rows, 1), F32),
            pltpu.VMEM((rows, 1), F32),
            pltpu.VMEM((rows, hd2), F32),
            pltpu.VMEM((page, ATT_QK_DIM), F32),
            pltpu.VMEM((page, ATT_V_DIM), F32),
        ],
    )
    out = pl.pallas_call(
        functools.partial(_attn_sample_kernel, nq=nq, n_steps=n_steps, pp=pp, layer_idx=layer_idx),
        grid_spec=grid_spec,
        out_shape=jax.ShapeDtypeStruct((bsz, nq, ATT_V_DIM), BF16),
        compiler_params=_cparams("parallel", "arbitrary"),
        name="attn_sample",
    )(page_table, q3, *([cache_kt] * pp), *([cache_v] * pp), kv3, kv3, bias, expand, pmask, lam_vec, subln_w)
    return out.reshape(bsz * nq, ATT_V_DIM)


def _router_params(wg, bg, we, be):
    wexp = jnp.transpose(we, (1, 0, 2)).reshape(D_MODEL, MOE_EXPERTS)
    wr = jnp.concatenate([wg, wexp], axis=1)
    wr = jnp.pad(wr, ((0, 0), (0, LANES - wr.shape[1])))
    w_hi = wr.astype(BF16)
    w_lo = (wr - w_hi.astype(F32)).astype(BF16)
    wr = jnp.concatenate([w_hi, w_lo], axis=1)
    br = jnp.concatenate([bg, be.reshape(MOE_EXPERTS)])
    br = jnp.pad(br, (0, LANES - br.shape[0])).reshape(1, LANES)
    return wr, br


def _head_tables(dt_bias, a_log, d_skip):
    rows = jnp.stack([dt_bias, a_log, d_skip])
    return jnp.pad(rows, ((0, SUBLANES - 3), (0, LANES - SSM_HEADS)))


def _prompt_bias_index(tq):
    r = np.arange(tq)[:, None]
    c = np.arange(tq)[None, :]
    tiles = [_bucket_tile(delta * tq + r - c) for delta in range(3)]
    return np.stack(tiles).reshape(-1)


def _sample_bias_index(nq, page, past_len):
    r = np.arange(nq)[:, None]
    c = np.arange(page)[None, :]
    last_page = _bucket_tile(past_len + r - (past_len - page) - c)
    own = np.where(c < nq, _bucket_tile(r - c), -1).astype(np.int32)
    far = _bucket_tile(np.full((nq, page), MAX_DISTANCE + page))
    return np.stack([last_page, own, far]).reshape(-1)


def kernel(x_prompt, x_sample, state_ssm, state_conv, cache_k, cache_v, page_table, ln_g, ln_b, m_w_in,
           m_conv_w, m_conv_b, m_dt_bias, m_a_log, m_d, m_norm_w, m_w_out, kv_w, a_w_q, a_lambda,
           a_subln_w, a_w_o, rel_bias, moe_wg, moe_bg, moe_we, moe_be, moe_w1, moe_w3, moe_w2):
    bp, lp, _ = x_prompt.shape
    bs, ls, _ = x_sample.shape
    n_pages = page_table.shape[1]
    page = cache_k.shape[1]
    past_len = n_pages * page
    assert page == PAGE_SIZE and page >= MAX_DISTANCE and min(ATT_TILE, lp) >= MAX_DISTANCE

    w_in = m_w_in[0]
    w_zx = w_in[:, :SSM_D_INNER + SSM_CONV_DIM].astype(BF16)
    w_dt = jnp.pad(w_in[:, SSM_D_INNER + SSM_CONV_DIM:], ((0, 0), (0, LANES - SSM_HEADS))).astype(BF16)
    ptab = _head_tables(m_dt_bias[0], m_a_log[0], m_d[0])
    conv_w = m_conv_w[0]
    conv_b = m_conv_b[0].reshape(1, SSM_CONV_DIM)
    norm_w = m_norm_w[0].reshape(1, SSM_D_INNER)
    w_out = m_w_out[0].astype(BF16)
    w_q = (a_w_q[0] * (LOG2E * ATT_HEAD_DIM ** -0.5)).astype(BF16)
    w_kv = kv_w.astype(BF16)
    w_kt = kv_w[:, :ATT_QK_DIM].T.astype(BF16)
    w_v = w_kv[:, ATT_QK_DIM:]
    w_o = a_w_o[0].astype(BF16)
    lam_vec = a_lambda[0]
    subln_w = a_subln_w[0].reshape(1, 2 * ATT_HEAD_DIM)
    routers = [_router_params(moe_wg[l], moe_bg[l], moe_we[l], moe_be[l]) for l in range(DEPTH)]
    experts = (moe_w1.reshape(DEPTH * MOE_EXPERTS, D_MODEL, MOE_FF),
               moe_w3.reshape(DEPTH * MOE_EXPERTS, D_MODEL, MOE_FF),
               moe_w2.reshape(DEPTH * MOE_EXPERTS, MOE_FF, D_MODEL))
    lng = ln_g.reshape(DEPTH, 2, 1, D_MODEL)
    lnb = ln_b.reshape(DEPTH, 2, 1, D_MODEL)

    tq = min(ATT_TILE, lp)
    maps = 2 * ATT_HEADS
    idx_p = _prompt_bias_index(tq)
    idx_s = _sample_bias_index(ls, page, past_len)
    n_p = idx_p.shape[0]
    n_s = idx_s.shape[0]
    n_tot = -(-(n_p + n_s) // 8192) * 8192
    idx_all = np.concatenate([idx_p, idx_s, np.zeros((n_tot - n_p - n_s,), np.int32)])
    tiles = _bias_tiles(rel_bias, idx_all)
    bias_p = tiles[:, :n_p].reshape(ATT_HEADS, 2, 3, tq, tq)
    bias_p = (bias_p[:, :, 0:2] - bias_p[:, :, 2:3]) * LOG2E
    bias_p = jnp.transpose(bias_p, (0, 2, 1, 3, 4)).reshape(ATT_HEADS, 2, 2 * tq, tq)
    bias_s = tiles[:, n_p:n_p + n_s].reshape(maps, 3, ls, page)
    bias_s = (bias_s - bias_s[:, 2:3]) * LOG2E
    bias_s = jnp.transpose(bias_s, (1, 0, 2, 3)).reshape(3, maps * ls, page)

    def mixer0(x, ssm0, conv0, bsz, seqlen):
        xf = x.reshape(bsz * seqlen, D_MODEL)
        zx, dt = _proj(xf, [w_zx, w_dt], [BF16, F32], "in_proj")
        y, new_conv, new_ssm = _mamba_core(zx, dt, conv0, ssm0, ptab, conv_w, conv_b, norm_w, bsz, seqlen)
        part = _post_mixer(y, w_out, xf, lng[0, 0], lnb[0, 0], *routers[0], "mamba_out_ln_router")
        return part, new_ssm[None], new_conv[None]

    def mixer1(x2, bsz, seqlen, past):
        if past is None:
            kt, ktb, v4, vb, qs = _kvq_prompt(x2, w_kt, w_v, w_q, bsz, seqlen)
            k_new = jnp.transpose(kt.reshape(bsz, maps, ATT_HEAD_DIM, seqlen), (0, 3, 1, 2))
            v_new = v4.reshape(bsz, seqlen, ATT_HEADS, 2 * ATT_HEAD_DIM)
            o = _attn_prompt(qs, ktb, vb, bias_p, lam_vec, subln_w, bsz, seqlen, DEPTH - 1)
        else:
            kv, qs = _proj(x2, [w_kv, w_q], [F32, BF16], "kvq_proj")
            k_new = kv[:, :ATT_QK_DIM].reshape(bsz, seqlen, maps, ATT_HEAD_DIM)
            v_new = kv[:, ATT_QK_DIM:].reshape(bsz, seqlen, ATT_HEADS, 2 * ATT_HEAD_DIM)
            o = _attn_sample(qs, kv, cache_k, cache_v, page_table, bias_s, lam_vec, subln_w, bsz, seqlen, DEPTH - 1)
        part = _post_mixer(o, w_o, x2, lng[1, 0], lnb[1, 0], *routers[1], "attn_out_ln_router")
        return part, k_new, v_new

    ssm0_p = jnp.zeros((bp, SSM_HEADS, SSM_HEAD_DIM, SSM_D_STATE), F32)
    conv0_p = jnp.zeros((bp, SSM_CONV - 1, SSM_CONV_DIM), F32)
    part_p, ssm_p, conv_p = mixer0(x_prompt, ssm0_p, conv0_p, bp, lp)
    part_s, ssm_s, conv_s = mixer0(x_sample, state_ssm[0], state_conv[0], bs, ls)
    x2_p, x2_s = _moe([part_p, part_s], *experts, 0, lng[0, 1], lnb[0, 1], "moe0_combine_ln")
    part_p, k_p, v_p = mixer1(x2_p, bp, lp, None)
    part_s, k_s, v_s = mixer1(x2_s, bs, ls, True)
    x4_p, x4_s = _moe([part_p, part_s], *experts, MOE_EXPERTS, lng[1, 1], lnb[1, 1], "moe1_combine_ln")
    y_p = x4_p.reshape(bp, lp, D_MODEL)
    y_s = x4_s.reshape(bs, ls, D_MODEL)
    return (y_p, y_s, ssm_p, conv_p, k_p, v_p, ssm_s, conv_s, k_s, v_s)
```
